```python
import functools
import jax
import jax.numpy as jnp
from jax import lax
import numpy as np

D_MODEL = 2048
BATCH = 4
SEQ = 4096
DEPTH = 1
DEC_BATCH = 8
DEC_SEQ = 16
PAST_LEN = 4096

CHUNK = 64
LEFT_CHUNKS = 8
LEFT = LEFT_CHUNKS * CHUNK
BAND = LEFT + CHUNK
MIX_WIDTH = D_MODEL
ATT_WIDTH = MIX_WIDTH // 2
RWKV_WIDTH = MIX_WIDTH - ATT_WIDTH
ATT_HEAD_DIM = 128
ATT_HEADS = ATT_WIDTH // ATT_HEAD_DIM
REL_CLIP = 256
N_REL = 2 * REL_CLIP + 1
RWKV_HEAD_DIM = 64
RWKV_HEADS = RWKV_WIDTH // RWKV_HEAD_DIM
W_LORA = 64
A_LORA = 64
G_LORA = 128
RWKV_PROJ = 3 * RWKV_WIDTH + W_LORA + A_LORA + G_LORA
PROJ_WIDTH = 3 * ATT_WIDTH + RWKV_PROJ
D_FF = 5632
PLE_DIM = 256
RMS_EPS = 1e-6
GN_EPS = 64e-5
NEG_INF = -1e30

kernel_name = "hybrid_chunkband_rwkv7_streaming_step"


def rms_norm(x, g):
    xf = x.astype(jnp.float32)
    y = xf * lax.rsqrt(jnp.mean(xf * xf, axis=-1, keepdims=True) + RMS_EPS)
    return (y * g.astype(jnp.float32)).astype(x.dtype)


def swiglu(x, w_gate, w_up, w_down):
    return (jax.nn.silu(x @ w_gate) * (x @ w_up)) @ w_down


def split_heads(t, n_heads, head_dim):
    return t.reshape(t.shape[:-1] + (n_heads, head_dim))


def rel_bias_lookup(rel_bias, d):
    idx = jnp.clip(d, -REL_CLIP, REL_CLIP) + REL_CLIP
    return jnp.moveaxis(rel_bias[idx], -1, 0).astype(jnp.float32)


def band_softmax_attention(q, k, v, bias, valid):
    s = jnp.einsum('bqhd,bkhd->bhqk', q, k).astype(jnp.float32) * (ATT_HEAD_DIM ** -0.5) + bias
    s = jnp.where(valid, s, NEG_INF)
    p = jax.nn.softmax(s, axis=-1).astype(v.dtype)
    return jnp.einsum('bhqk,bkhd->bqhd', p, v)


def prompt_chunk_attention(q, k, v, rel_bias):
    B, S, H, Dh = q.shape
    n_chunks = S // CHUNK
    pad = ((0, 0), (LEFT, 0), (0, 0), (0, 0))
    k_pad = jnp.pad(k, pad)
    v_pad = jnp.pad(v, pad)
    i = jnp.arange(CHUNK)
    j = jnp.arange(BAND)
    bias = rel_bias_lookup(rel_bias, i[:, None] + LEFT - j[None, :])
    q_chunks = jnp.swapaxes(q.reshape(B, n_chunks, CHUNK, H, Dh), 0, 1)

    def one_chunk(args):
        q_c, c = args
        start = c * CHUNK
        k_b = lax.dynamic_slice_in_dim(k_pad, start, BAND, axis=1)
        v_b = lax.dynamic_slice_in_dim(v_pad, start, BAND, axis=1)
        valid = ((start - LEFT + j) >= 0)[None, :]
        return band_softmax_attention(q_c, k_b, v_b, bias, valid)

    out = lax.map(one_chunk, (q_chunks, jnp.arange(n_chunks)))
    out = jnp.swapaxes(out, 0, 1).reshape(B, S, H * Dh)
    keep = max(S - LEFT, 0)
    return out, k[:, keep:], v[:, keep:]


def sample_chunk_attention(q, k, v, rel_bias, cache_k, cache_v):
    L = cache_k.shape[1]
    T = q.shape[1]
    k_all = jnp.concatenate([cache_k.astype(k.dtype), k], axis=1)
    v_all = jnp.concatenate([cache_v.astype(v.dtype), v], axis=1)
    k_pos = jnp.concatenate([jnp.arange(L) - L, jnp.arange(T)])
    bias = rel_bias_lookup(rel_bias, jnp.arange(T)[:, None] - k_pos[None, :])
    valid = jnp.ones((1, L + T), dtype=bool)
    out = band_softmax_attention(q, k_all, v_all, bias, valid)
    return out.reshape(q.shape[0], T, ATT_WIDTH), k, v


def wkv7_scan(state0, r, w, k, v, a_vec, b_vec):
    def step(S, inp):
        r_t, w_t, k_t, v_t, a_t, b_t = inp
        Sa = jnp.einsum('bhvk,bhk->bhv', S, a_t)
        S = S * w_t[:, :, None, :] + Sa[..., None] * b_t[:, :, None, :] + v_t[..., None] * k_t[:, :, None, :]
        return S, jnp.einsum('bhvk,bhk->bhv', S, r_t)
    xs = tuple(jnp.moveaxis(t, 1, 0) for t in (r, w, k, v, a_vec, b_vec))
    S, ys = lax.scan(step, state0.astype(jnp.float32), xs)
    return S, jnp.moveaxis(ys, 0, 1)


def rwkv7_time_mix(p_r, shift_prev, wkv_prev, lw):
    B, T, _ = p_r.shape
    f32 = jnp.float32
    shifted = jnp.concatenate([shift_prev.astype(p_r.dtype), p_r[:, :-1]], axis=1)
    xm = (p_r + lw['rwkv_mu'] * (shifted - p_r)).astype(f32)
    o1, o2, o3 = RWKV_WIDTH, 2 * RWKV_WIDTH, 3 * RWKV_WIDTH
    r = xm[..., :o1]
    k = xm[..., o1:o2]
    v = xm[..., o2:o3]
    w_lo = xm[..., o3:o3 + W_LORA]
    a_lo = xm[..., o3 + W_LORA:o3 + W_LORA + A_LORA]
    g_lo = xm[..., o3 + W_LORA + A_LORA:]
    w = -jax.nn.softplus(-(lw['rwkv_w0'].astype(f32) + jnp.tanh(w_lo) @ lw['rwkv_w_lora'].astype(f32))) - 0.5
    decay = jnp.exp(-jnp.exp(w))
    a = jax.nn.sigmoid(lw['rwkv_a0'].astype(f32) + a_lo @ lw['rwkv_a_lora'].astype(f32))
    g = jax.nn.sigmoid(g_lo) @ lw['rwkv_g_lora'].astype(f32)
    hs = lambda t: split_heads(t, RWKV_HEADS, RWKV_HEAD_DIM)
    kk = hs(k * lw['rwkv_k_k'].astype(f32))
    kk = kk / jnp.maximum(jnp.sqrt(jnp.sum(kk * kk, axis=-1, keepdims=True)), 1e-12)
    k = k * (1.0 + (a - 1.0) * lw['rwkv_k_a'].astype(f32))
    rh, kh, vh, ah = hs(r), hs(k), hs(v), hs(a)
    S, y = wkv7_scan(wkv_prev, rh, hs(decay), kh, vh, -kk, kk * ah)
    mu = jnp.mean(y, axis=-1, keepdims=True)
    var = jnp.mean(jnp.square(y - mu), axis=-1, keepdims=True)
    y = (y - mu) * lax.rsqrt(var + GN_EPS)
    y = y * hs(lw['rwkv_lnx_w'].astype(f32)) + hs(lw['rwkv_lnx_b'].astype(f32))
    y = y + jnp.sum(rh * kh * lw['rwkv_r_k'].astype(f32), axis=-1, keepdims=True) * vh
    out = (y.reshape(B, T, RWKV_WIDTH) * g).astype(p_r.dtype)
    return out, p_r[:, -1:], S.astype(wkv_prev.dtype)


def trunk_layer(x, p, lw, attend, shift_prev, wkv_prev):
    h = x + 0.5 * swiglu(rms_norm(x, lw['norm_ffn1']), lw['ffn1_gate'], lw['ffn1_up'], lw['ffn1_down'])
    proj = rms_norm(h, lw['norm_mix']) @ lw['w_in']
    q = rms_norm(split_heads(proj[..., :ATT_WIDTH], ATT_HEADS, ATT_HEAD_DIM), lw['q_norm'])
    k = rms_norm(split_heads(proj[..., ATT_WIDTH:2 * ATT_WIDTH], ATT_HEADS, ATT_HEAD_DIM), lw['k_norm'])
    v = split_heads(proj[..., 2 * ATT_WIDTH:3 * ATT_WIDTH], ATT_HEADS, ATT_HEAD_DIM)
    att, k_keep, v_keep = attend(q, k, v, lw['rel_bias'])
    rw, shift_new, wkv_new = rwkv7_time_mix(proj[..., 3 * ATT_WIDTH:], shift_prev, wkv_prev, lw)
    h = h + jnp.concatenate([att, rw], axis=-1) @ lw['w_out']
    h = h + 0.5 * swiglu(rms_norm(h, lw['norm_ffn2']), lw['ffn2_gate'], lw['ffn2_up'], lw['ffn2_down'])
    gate = jax.nn.sigmoid(rms_norm(h, lw['norm_ple']) @ lw['ple_gate'])
    h = h + gate * (p @ lw['ple_proj'])
    return h, (k_keep, v_keep, wkv_new, shift_new)


def setup_inputs(seed: int = 0) -> dict:
    key = jax.random.key(seed)
    ks = iter(jax.random.split(key, 48))
    f32 = jnp.float32

    def nrm(shape, scale):
        return jax.random.normal(next(ks), shape, f32) * scale

    def gain(shape):
        return 1.0 + nrm(shape, 0.05)

    att_len = min(LEFT, PAST_LEN)
    ch = (jnp.arange(RWKV_WIDTH) % RWKV_HEAD_DIM).astype(f32) / (RWKV_HEAD_DIM - 1)
    return {
        'x_prompt': nrm((BATCH, SEQ, D_MODEL), 1.0),
        'x_sample': nrm((DEC_BATCH, DEC_SEQ, D_MODEL), 1.0),
        'cache_att_k': nrm((DEPTH, DEC_BATCH, att_len, ATT_HEADS, ATT_HEAD_DIM), 1.0),
        'cache_att_v': nrm((DEPTH, DEC_BATCH, att_len, ATT_HEADS, ATT_HEAD_DIM), 1.0),
        'state_wkv': nrm((DEPTH, DEC_BATCH, RWKV_HEADS, RWKV_HEAD_DIM, RWKV_HEAD_DIM), 0.5),
        'state_shift': nrm((DEPTH, DEC_BATCH, 1, RWKV_PROJ), 1.0),
        'p_prompt': nrm((DEPTH, BATCH, SEQ, PLE_DIM), 1.0),
        'p_sample': nrm((DEPTH, DEC_BATCH, DEC_SEQ, PLE_DIM), 1.0),
        'norm_ffn1': gain((DEPTH, D_MODEL)),
        'ffn1_gate': nrm((DEPTH, D_MODEL, D_FF), D_MODEL ** -0.5),
        'ffn1_up': nrm((DEPTH, D_MODEL, D_FF), D_MODEL ** -0.5),
        'ffn1_down': nrm((DEPTH, D_FF, D_MODEL), D_FF ** -0.5),
        'norm_mix': gain((DEPTH, D_MODEL)),
        'w_in': nrm((DEPTH, D_MODEL, PROJ_WIDTH), D_MODEL ** -0.5),
        'q_norm': gain((DEPTH, ATT_HEAD_DIM)),
        'k_norm': gain((DEPTH, ATT_HEAD_DIM)),
        'rel_bias': nrm((DEPTH, N_REL, ATT_HEADS), 0.5),
        'rwkv_mu': jax.random.uniform(next(ks), (DEPTH, RWKV_PROJ), f32),
        'rwkv_w0': -6.0 + 5.0 * ch[None, :] + nrm((DEPTH, RWKV_WIDTH), 0.1),
        'rwkv_w_lora': nrm((DEPTH, W_LORA, RWKV_WIDTH), 0.3 * W_LORA ** -0.5),
        'rwkv_a0': nrm((DEPTH, RWKV_WIDTH), 0.1),
        'rwkv_a_lora': nrm((DEPTH, A_LORA, RWKV_WIDTH), 0.5 * A_LORA ** -0.5),
        'rwkv_g_lora': nrm((DEPTH, G_LORA, RWKV_WIDTH), G_LORA ** -0.5),
        'rwkv_k_k': 0.85 + nrm((DEPTH, RWKV_WIDTH), 0.05),
        'rwkv_k_a': 1.0 + nrm((DEPTH, RWKV_WIDTH), 0.05),
        'rwkv_r_k': nrm((DEPTH, RWKV_HEADS, RWKV_HEAD_DIM), 0.1),
        'rwkv_lnx_w': gain((DEPTH, RWKV_WIDTH)),
        'rwkv_lnx_b': nrm((DEPTH, RWKV_WIDTH), 0.02),
        'w_out': nrm((DEPTH, MIX_WIDTH, D_MODEL), MIX_WIDTH ** -0.5),
        'norm_ffn2': gain((DEPTH, D_MODEL)),
        'ffn2_gate': nrm((DEPTH, D_MODEL, D_FF), D_MODEL ** -0.5),
        'ffn2_up': nrm((DEPTH, D_MODEL, D_FF), D_MODEL ** -0.5),
        'ffn2_down': nrm((DEPTH, D_FF, D_MODEL), D_FF ** -0.5),
        'norm_ple': gain((DEPTH, D_MODEL)),
        'ple_gate': nrm((DEPTH, D_MODEL, D_MODEL), D_MODEL ** -0.5),
        'ple_proj': nrm((DEPTH, PLE_DIM, D_MODEL), PLE_DIM ** -0.5),
    }


def reference(x_prompt, x_sample, cache_att_k, cache_att_v, state_wkv, state_shift, p_prompt, p_sample,
              norm_ffn1, ffn1_gate, ffn1_up, ffn1_down, norm_mix, w_in, q_norm, k_norm, rel_bias,
              rwkv_mu, rwkv_w0, rwkv_w_lora, rwkv_a0, rwkv_a_lora, rwkv_g_lora, rwkv_k_k, rwkv_k_a,
              rwkv_r_k, rwkv_lnx_w, rwkv_lnx_b, w_out, norm_ffn2, ffn2_gate, ffn2_up, ffn2_down,
              norm_ple, ple_gate, ple_proj):
    hp, hs = x_prompt, x_sample
    prompt_states, sample_states = [], []
    for i in range(DEPTH):
        lw = {
            'norm_ffn1': norm_ffn1[i], 'ffn1_gate': ffn1_gate[i], 'ffn1_up': ffn1_up[i], 'ffn1_down': ffn1_down[i],
            'norm_mix': norm_mix[i], 'w_in': w_in[i], 'q_norm': q_norm[i], 'k_norm': k_norm[i],
            'rel_bias': rel_bias[i], 'rwkv_mu': rwkv_mu[i], 'rwkv_w0': rwkv_w0[i],
            'rwkv_w_lora': rwkv_w_lora[i], 'rwkv_a0': rwkv_a0[i], 'rwkv_a_lora': rwkv_a_lora[i],
            'rwkv_g_lora': rwkv_g_lora[i], 'rwkv_k_k': rwkv_k_k[i], 'rwkv_k_a': rwkv_k_a[i],
            'rwkv_r_k': rwkv_r_k[i], 'rwkv_lnx_w': rwkv_lnx_w[i], 'rwkv_lnx_b': rwkv_lnx_b[i],
            'w_out': w_out[i], 'norm_ffn2': norm_ffn2[i], 'ffn2_gate': ffn2_gate[i], 'ffn2_up': ffn2_up[i],
            'ffn2_down': ffn2_down[i], 'norm_ple': norm_ple[i], 'ple_gate': ple_gate[i], 'ple_proj': ple_proj[i],
        }
        b = hp.shape[0]
        zero_shift = jnp.zeros((b, 1, RWKV_PROJ), hp.dtype)
        zero_wkv = jnp.zeros((b, RWKV_HEADS, RWKV_HEAD_DIM, RWKV_HEAD_DIM), hp.dtype)
        hp, sp = trunk_layer(hp, p_prompt[i], lw, prompt_chunk_attention, zero_shift, zero_wkv)
        sample_attend = functools.partial(sample_chunk_attention, cache_k=cache_att_k[i], cache_v=cache_att_v[i])
        hs, ss = trunk_layer(hs, p_sample[i], lw, sample_attend, state_shift[i], state_wkv[i])
        prompt_states.append(sp)
        sample_states.append(ss)
    att_k_prompt = jnp.stack([s[0] for s in prompt_states])
    att_v_prompt = jnp.stack([s[1] for s in prompt_states])
    wkv_prompt = jnp.stack([s[2] for s in prompt_states])
    shift_prompt = jnp.stack([s[3] for s in prompt_states])
    att_k_sample = jnp.stack([s[0] for s in sample_states])
    att_v_sample = jnp.stack([s[1] for s in sample_states])
    wkv_sample = jnp.stack([s[2] for s in sample_states])
    shift_sample = jnp.stack([s[3] for s in sample_states])
    return (hp, hs, att_k_prompt, att_v_prompt, wkv_prompt, shift_prompt,
            att_k_sample, att_v_sample, wkv_sample, shift_sample)
```

```python
import functools
import math

import jax
import jax.numpy as jnp
from jax import lax
from jax.experimental import pallas as pl
from jax.experimental.pallas import tpu as pltpu

F32 = jnp.float32
BF16 = jnp.bfloat16

CHUNK = 64
LEFT = 512
ATT_HEAD_DIM = 128
REL_CLIP = 256
RWKV_HEAD_DIM = 64
W_LORA = 64
A_LORA = 64
G_LORA = 128
RMS_EPS = 1e-6
GN_EPS = 64e-5
NEG_INF = -1e30

V7X_VMEM_BYTES = 64 * 1024 * 1024
VMEM_LIMIT_CAP = 56 * 1024 * 1024


def _vmem_limit(block_bytes, scratch_bytes, temp_bytes):
    need = 2 * block_bytes + scratch_bytes + temp_bytes + (4 << 20)
    return int(min(max(need, 16 << 20), VMEM_LIMIT_CAP))


def _nbytes(shape, dtype):
    return math.prod(shape) * jnp.dtype(dtype).itemsize


def _rms_norm_rows(x, gain):
    ms = jnp.mean(x * x, axis=-1, keepdims=True)
    return x * lax.rsqrt(ms + RMS_EPS) * gain


def _mm(a, b):
    return jnp.dot(a, b, preferred_element_type=F32)


def _mm_nt(a, b):
    return lax.dot_general(a, b, (((1,), (1,)), ((), ())), preferred_element_type=F32)


def _mm_tn(a, b):
    return lax.dot_general(a, b, (((0,), (0,)), ((), ())), preferred_element_type=F32)


def _ffn_body(x_ref, g_ref, wg_ref, wu_ref, wd_ref, o_ref, xn_ref):
    j = pl.program_id(1)
    last = pl.num_programs(1) - 1

    @pl.when(j == 0)
    def _():
        xn_ref[...] = _rms_norm_rows(x_ref[...], g_ref[...]).astype(BF16)

    xn = xn_ref[...]
    a = _mm(xn, wg_ref[...])
    u = _mm(xn, wu_ref[...])
    h = (a * jax.nn.sigmoid(a) * u).astype(BF16)
    d = _mm(h, wd_ref[...])

    @pl.when(j == 0)
    def _():
        o_ref[...] = d

    @pl.when(j > 0)
    def _():
        o_ref[...] += d

    @pl.when(j == last)
    def _():
        o_ref[...] = x_ref[...] + 0.5 * o_ref[...]


def _ffn(x, gain, wg, wu, wd, *, tm, tf):
    m, d = x.shape
    f = wg.shape[1]
    assert m % tm == 0 and f % tf == 0
    blocks = (_nbytes((tm, d), F32) * 2 + _nbytes((d, tf), BF16) * 3)
    temps = _nbytes((tm, tf), F32) * 4
    return pl.pallas_call(
        _ffn_body,
        grid=(m // tm, f // tf),
        in_specs=[
            pl.BlockSpec((tm, d), lambda i, j: (i, 0)),
            pl.BlockSpec((1, d), lambda i, j: (0, 0)),
            pl.BlockSpec((d, tf), lambda i, j: (0, j)),
            pl.BlockSpec((d, tf), lambda i, j: (0, j)),
            pl.BlockSpec((tf, d), lambda i, j: (j, 0)),
        ],
        out_specs=pl.BlockSpec((tm, d), lambda i, j: (i, 0)),
        out_shape=jax.ShapeDtypeStruct((m, d), F32),
        scratch_shapes=[pltpu.VMEM((tm, d), BF16)],
        compiler_params=pltpu.CompilerParams(
            dimension_semantics=("parallel", "arbitrary"),
            vmem_limit_bytes=_vmem_limit(blocks, _nbytes((tm, d), BF16), temps)),
        name="ffn",
    )(x, gain.reshape(1, d), wg, wu, wd)


def _proj_body(x_ref, g_ref, w_ref, hg_ref, o_ref, xn_ref, *, n_norm_blocks, head_dim):
    j = pl.program_id(1)

    @pl.when(j == 0)
    def _():
        xn_ref[...] = _rms_norm_rows(x_ref[...], g_ref[...]).astype(BF16)

    acc = _mm(xn_ref[...], w_ref[...])
    tn = acc.shape[1]
    if n_norm_blocks == 0:
        o_ref[...] = acc
        return

    @pl.when(j < n_norm_blocks)
    def _():
        for c in range(tn // head_dim):
            sl = slice(c * head_dim, (c + 1) * head_dim)
            o_ref[:, sl] = _rms_norm_rows(acc[:, sl], hg_ref[:, sl])

    @pl.when(j >= n_norm_blocks)
    def _():
        o_ref[...] = acc


def _proj(x, gain, w, head_gain, *, tm, tn, n_norm_cols):
    m, d = x.shape
    n = w.shape[1]
    assert m % tm == 0 and n % tn == 0 and n_norm_cols % tn == 0 and tn % ATT_HEAD_DIM == 0
    blocks = _nbytes((tm, d), F32) + _nbytes((d, tn), BF16) + _nbytes((tm, tn), F32)
    return pl.pallas_call(
        functools.partial(_proj_body, n_norm_blocks=n_norm_cols // tn, head_dim=ATT_HEAD_DIM),
        grid=(m // tm, n // tn),
        in_specs=[
            pl.BlockSpec((tm, d), lambda i, j: (i, 0)),
            pl.BlockSpec((1, d), lambda i, j: (0, 0)),
            pl.BlockSpec((d, tn), lambda i, j: (0, j)),
            pl.BlockSpec((1, tn), lambda i, j: (0, j)),
        ],
        out_specs=pl.BlockSpec((tm, tn), lambda i, j: (i, j)),
        out_shape=jax.ShapeDtypeStruct((m, n), F32),
        scratch_shapes=[pltpu.VMEM((tm, d), BF16)],
        compiler_params=pltpu.CompilerParams(
            dimension_semantics=("parallel", "arbitrary"),
            vmem_limit_bytes=_vmem_limit(blocks, _nbytes((tm, d), BF16), 3 * _nbytes((tm, tn), F32))),
        name="proj",
    )(x, gain.reshape(1, d), w, head_gain.reshape(1, n))


def _attn_body(q_ref, k_ref, v_ref, ck_ref, cv_ref, bias_ref, o_ref, kp_ref, vp_ref,
               *, seq, tq, chunk, left_valid):
    kp_ref[0:LEFT, :] = ck_ref[0].astype(BF16)
    vp_ref[0:LEFT, :] = cv_ref[0].astype(BF16)
    kp_ref[LEFT:LEFT + seq, :] = k_ref[0].astype(BF16)
    vp_ref[LEFT:LEFT + seq, :] = v_ref[0].astype(BF16)

    width = LEFT + tq
    row = lax.broadcasted_iota(jnp.int32, (tq, width), 0)
    col = lax.broadcasted_iota(jnp.int32, (tq, width), 1)
    chunk_start = (row // chunk) * chunk
    in_band = (col >= chunk_start) & (col < chunk_start + LEFT + chunk)
    scale = ATT_HEAD_DIM ** -0.5

    def block(t, carry):
        base = pl.multiple_of(t * tq, tq)
        q = q_ref[0, pl.ds(base, tq), :].astype(BF16)
        kb = kp_ref[pl.ds(base, width), :]
        vb = vp_ref[pl.ds(base, width), :]
        s = _mm_nt(q, kb) * scale + bias_ref[0]
        ok = in_band if left_valid else in_band & (col >= LEFT - base)
        s = jnp.where(ok, s, NEG_INF)
        e = jnp.exp(s - jnp.max(s, axis=-1, keepdims=True))
        p = e / jnp.sum(e, axis=-1, keepdims=True)
        o_ref[0, pl.ds(base, tq), :] = _mm(p.astype(BF16), vb)
        return carry

    lax.fori_loop(0, seq // tq, block, 0)


def _attention(qkv, ctx_k, ctx_v, bias, *, tq, chunk, left_valid):
    b, seq, three_w = qkv.shape
    dh = ATT_HEAD_DIM
    heads = three_w // (3 * dh)
    assert seq % tq == 0 and tq % chunk == 0
    per_batch_ctx = ctx_k.shape[0] == b
    ctx_map = (lambda bi, h: (bi, 0, h)) if per_batch_ctx else (lambda bi, h: (0, 0, h))
    blocks = 4 * _nbytes((seq, dh), F32) + 2 * _nbytes((LEFT, dh), F32) + _nbytes((tq, LEFT + tq), F32)
    scratch = 2 * _nbytes((LEFT + seq, dh), BF16)
    return pl.pallas_call(
        functools.partial(_attn_body, seq=seq, tq=tq, chunk=chunk, left_valid=left_valid),
        grid=(b, heads),
        in_specs=[
            pl.BlockSpec((1, seq, dh), lambda bi, h: (bi, 0, h)),
            pl.BlockSpec((1, seq, dh), lambda bi, h: (bi, 0, heads + h)),
            pl.BlockSpec((1, seq, dh), lambda bi, h: (bi, 0, 2 * heads + h)),
            pl.BlockSpec((1, LEFT, dh), ctx_map),
            pl.BlockSpec((1, LEFT, dh), ctx_map),
            pl.BlockSpec((1, tq, LEFT + tq), lambda bi, h: (h, 0, 0)),
        ],
        out_specs=pl.BlockSpec((1, seq, dh), lambda bi, h: (bi, 0, h)),
        out_shape=jax.ShapeDtypeStruct((b, seq, heads * dh), F32),
        scratch_shapes=[pltpu.VMEM((LEFT + seq, dh), BF16), pltpu.VMEM((LEFT + seq, dh), BF16)],
        compiler_params=pltpu.CompilerParams(
            dimension_semantics=("parallel", "parallel"),
            vmem_limit_bytes=_vmem_limit(blocks, scratch, 8 * _nbytes((tq, LEFT + tq), F32))),
        name="attn",
    )(qkv, qkv, qkv, ctx_k, ctx_v, bias)


def _band_bias(rel_bias, tq):
    i = jnp.arange(tq)[:, None]
    c = jnp.arange(LEFT + tq)[None, :]
    idx = jnp.clip(i + LEFT - c, -REL_CLIP, REL_CLIP) + REL_CLIP
    return jnp.moveaxis(rel_bias[idx], -1, 0).astype(F32)


def _split3(x):
    h1 = x.astype(BF16)
    r1 = x - h1.astype(F32)
    h2 = r1.astype(BF16)
    h3 = (r1 - h2.astype(F32)).astype(BF16)
    return h1, h2, h3


def _wkv_body(x_ref, sp_ref, s0_ref, mu_ref, w0_ref, wl_ref, a0_ref, al_ref, gl_ref, kk_ref, ka_ref,
              rk_ref, lnw_ref, lnb_ref, o_ref, so_ref, carry_ref, s_ref, *, clen, heads):
    c = pl.program_id(1)
    n = RWKV_HEAD_DIM
    width = heads * n

    @pl.when(c == 0)
    def _():
        carry_ref[...] = sp_ref[0]
        s_ref[...] = s0_ref[0]

    xb = x_ref[0]
    trow = lax.broadcasted_iota(jnp.int32, (clen, 1), 0)
    shifted = jnp.where(trow == 0, carry_ref[...], pltpu.roll(xb, 1, 0))
    carry_ref[...] = xb[clen - 1:clen, :]
    xm = xb + mu_ref[...] * (shifted - xb)

    r = xm[:, 0:width]
    k = xm[:, width:2 * width]
    v = xm[:, 2 * width:3 * width]
    o3 = 3 * width
    w_lo = xm[:, o3:o3 + W_LORA]
    a_lo = xm[:, o3 + W_LORA:o3 + W_LORA + A_LORA]
    g_lo = xm[:, o3 + W_LORA + A_LORA:o3 + W_LORA + A_LORA + G_LORA]

    z = -(w0_ref[...] + _mm(jnp.tanh(w_lo).astype(BF16), wl_ref[...]))
    softplus_z = jnp.maximum(z, 0.0) + jnp.log1p(jnp.exp(-jnp.abs(z)))
    log_decay = -jnp.exp(-softplus_z - 0.5)
    a = jax.nn.sigmoid(a0_ref[...] + _mm(a_lo.astype(BF16), al_ref[...]))
    g = _mm(jax.nn.sigmoid(g_lo).astype(BF16), gl_ref[...])
    kk = k * kk_ref[...]
    k = k * (1.0 + (a - 1.0) * ka_ref[...])
    rk = r * k * rk_ref[...]

    ti = lax.broadcasted_iota(jnp.int32, (clen, clen), 0)
    si = lax.broadcasted_iota(jnp.int32, (clen, clen), 1)
    lower = si <= ti
    strict = si < ti
    tri = lower.astype(BF16)
    d1, d2, d3 = _split3(log_decay)
    cum = _mm(tri, d1) + _mm(tri, d2) + _mm(tri, d3)
    g_in = jnp.exp(cum)
    g_inv = jnp.exp(-cum)
    g_ex = jnp.exp(cum - log_decay)
    g_last = g_in[clen - 1:clen, :]

    r_t = r * g_in
    k_h = k * g_inv
    k_l = k_h * g_last
    levels = max(1, (clen - 1).bit_length())

    outs = []
    for h in range(heads):
        sl = slice(h * n, (h + 1) * n)
        kk_h = kk[:, sl]
        kk_h = kk_h / jnp.maximum(jnp.sqrt(jnp.sum(kk_h * kk_h, axis=-1, keepdims=True)), 1e-12)
        b_hat = kk_h * a[:, sl] * g_inv[:, sl]
        a_t = -kk_h * g_ex[:, sl]
        v_h = v[:, sl]
        s_prev = s_ref[h]
        s_bf = s_prev.astype(BF16)

        x1 = jnp.concatenate([a_t, r_t[:, sl]], axis=0).astype(BF16)
        x2 = jnp.concatenate([b_hat, k_h[:, sl]], axis=0).astype(BF16)
        p = _mm_nt(x1, x2)
        a_ab = jnp.where(strict, p[:clen, :clen], 0.0)
        a_ak = jnp.where(strict, p[:clen, clen:], 0.0)
        a_rb = jnp.where(lower, p[clen:, :clen], 0.0)
        a_rk = jnp.where(lower, p[clen:, clen:], 0.0)

        sa = _mm_nt(a_t.astype(BF16), s_bf) + _mm(a_ak.astype(BF16), v_h.astype(BF16))
        nk = a_ab
        for lvl in range(levels):
            nk_bf = nk.astype(BF16)
            sa = sa + _mm(nk_bf, sa.astype(BF16))
            if lvl + 1 < levels:
                nk = _mm(nk_bf, nk_bf)

        sa_bf = sa.astype(BF16)
        y = (_mm_nt(r_t[:, sl].astype(BF16), s_bf) + _mm(a_rb.astype(BF16), sa_bf)
             + _mm(a_rk.astype(BF16), v_h.astype(BF16)))
        s_ref[h] = (s_prev * g_last[:, sl] + _mm_tn(sa_bf, (b_hat * g_last[:, sl]).astype(BF16))
                    + _mm_tn(v_h.astype(BF16), k_l[:, sl].astype(BF16)))

        mean = jnp.mean(y, axis=-1, keepdims=True)
        var = jnp.mean(jnp.square(y - mean), axis=-1, keepdims=True)
        y = (y - mean) * lax.rsqrt(var + GN_EPS)
        y = y * lnw_ref[:, sl] + lnb_ref[:, sl]
        y = y + jnp.sum(rk[:, sl], axis=-1, keepdims=True) * v_h
        outs.append(y)

    o_ref[0] = jnp.concatenate(outs, axis=1) * g

    @pl.when(c == pl.num_programs(1) - 1)
    def _():
        so_ref[0] = s_ref[...]


def _wkv(pr, shift_prev, state0, mu, w0, w_lora, a0, a_lora, g_lora, k_k, k_a, r_k, lnx_w, lnx_b, *, clen):
    b, t, pw = pr.shape
    heads = state0.shape[1]
    n = RWKV_HEAD_DIM
    width = heads * n
    assert t % clen == 0 and pw == 3 * width + W_LORA + A_LORA + G_LORA
    row = lambda a: a.reshape(1, -1)
    full = lambda shape: pl.BlockSpec(shape, lambda bi, c: (0,) * len(shape))
    blocks = _nbytes((clen, pw), F32) + _nbytes((clen, width), F32) + 2 * _nbytes((heads, n, 128), F32)
    return pl.pallas_call(
        functools.partial(_wkv_body, clen=clen, heads=heads),
        grid=(b, t // clen),
        in_specs=[
            pl.BlockSpec((1, clen, pw), lambda bi, c: (bi, c, 0)),
            pl.BlockSpec((1, 1, pw), lambda bi, c: (bi, 0, 0)),
            pl.BlockSpec((1, heads, n, n), lambda bi, c: (bi, 0, 0, 0)),
            full((1, pw)), full((1, width)), full((W_LORA, width)), full((1, width)), full((A_LORA, width)),
            full((G_LORA, width)), full((1, width)), full((1, width)), full((1, width)), full((1, width)),
            full((1, width)),
        ],
        out_specs=[
            pl.BlockSpec((1, clen, width), lambda bi, c: (bi, c, 0)),
            pl.BlockSpec((1, heads, n, n), lambda bi, c: (bi, 0, 0, 0)),
        ],
        out_shape=[jax.ShapeDtypeStruct((b, t, width), F32), jax.ShapeDtypeStruct((b, heads, n, n), F32)],
        scratch_shapes=[pltpu.VMEM((1, pw), F32), pltpu.VMEM((heads, n, n), F32)],
        compiler_params=pltpu.CompilerParams(
            dimension_semantics=("parallel", "arbitrary"),
            vmem_limit_bytes=_vmem_limit(blocks, 2 * _nbytes((heads, n, 128), F32),
                                         40 * _nbytes((clen, width), F32))),
        name="wkv",
    )(pr, shift_prev, state0, row(mu), row(w0), w_lora, row(a0), a_lora, g_lora, row(k_k), row(k_a),
      row(r_k), row(lnx_w), row(lnx_b))


def _outproj_body(h_ref, att_ref, rw_ref, wa_ref, wr_ref, o_ref):
    o_ref[...] = (h_ref[...] + _mm(att_ref[...].astype(BF16), wa_ref[...])
                  + _mm(rw_ref[...].astype(BF16), wr_ref[...]))


def _outproj(h, att, rw, w_att, w_rw, *, tm):
    m, d = h.shape
    ka, kr = att.shape[1], rw.shape[1]
    assert m % tm == 0
    blocks = (2 * _nbytes((tm, d), F32) + _nbytes((tm, ka), F32) + _nbytes((tm, kr), F32)
              + _nbytes((ka, d), BF16) + _nbytes((kr, d), BF16))
    return pl.pallas_call(
        _outproj_body,
        grid=(m // tm,),
        in_specs=[
            pl.BlockSpec((tm, d), lambda i: (i, 0)),
            pl.BlockSpec((tm, ka), lambda i: (i, 0)),
            pl.BlockSpec((tm, kr), lambda i: (i, 0)),
            pl.BlockSpec((ka, d), lambda i: (0, 0)),
            pl.BlockSpec((kr, d), lambda i: (0, 0)),
        ],
        out_specs=pl.BlockSpec((tm, d), lambda i: (i, 0)),
        out_shape=jax.ShapeDtypeStruct((m, d), F32),
        compiler_params=pltpu.CompilerParams(
            dimension_semantics=("parallel",),
            vmem_limit_bytes=_vmem_limit(blocks, 0, 2 * _nbytes((tm, d), F32))),
        name="outproj",
    )(h, att, rw, w_att, w_rw)


def _ple_body(x_ref, g_ref, wg_ref, p_ref, wp_ref, o_ref, xn_ref):
    j = pl.program_id(1)
    tn = o_ref.shape[1]

    @pl.when(j == 0)
    def _():
        xn_ref[...] = _rms_norm_rows(x_ref[...], g_ref[...]).astype(BF16)

    gate = jax.nn.sigmoid(_mm(xn_ref[...], wg_ref[...]))
    emb = _mm(p_ref[...].astype(BF16), wp_ref[...])
    col = pl.multiple_of(j * tn, tn)
    o_ref[...] = x_ref[:, pl.ds(col, tn)] + gate * emb


def _ple(x, gain, w_gate, p, w_proj, *, tm, tn):
    m, d = x.shape
    pd = p.shape[1]
    assert m % tm == 0 and d % tn == 0
    blocks = (_nbytes((tm, d), F32) + _nbytes((d, tn), BF16) + _nbytes((tm, pd), F32)
              + _nbytes((pd, tn), BF16) + _nbytes((tm, tn), F32))
    return pl.pallas_call(
        _ple_body,
        grid=(m // tm, d // tn),
        in_specs=[
            pl.BlockSpec((tm, d), lambda i, j: (i, 0)),
            pl.BlockSpec((1, d), lambda i, j: (0, 0)),
            pl.BlockSpec((d, tn), lambda i, j: (0, j)),
            pl.BlockSpec((tm, pd), lambda i, j: (i, 0)),
            pl.BlockSpec((pd, tn), lambda i, j: (0, j)),
        ],
        out_specs=pl.BlockSpec((tm, tn), lambda i, j: (i, j)),
        out_shape=jax.ShapeDtypeStruct((m, d), F32),
        scratch_shapes=[pltpu.VMEM((tm, d), BF16)],
        compiler_params=pltpu.CompilerParams(
            dimension_semantics=("parallel", "arbitrary"),
            vmem_limit_bytes=_vmem_limit(blocks, _nbytes((tm, d), BF16), 4 * _nbytes((tm, tn), F32))),
        name="ple",
    )(x, gain.reshape(1, d), w_gate, p, w_proj)


def _pick(m, candidates):
    for c in candidates:
        if m % c == 0:
            return c
    raise ValueError(f"no tile for {m}")


def _trunk_layer(x, p, lw, ctx_k, ctx_v, left_valid, shift_prev, wkv_prev):
    b, t, d = x.shape
    m = b * t
    att_w = lw["w_att_out"].shape[0]
    tm = _pick(m, (512, 256, 128))
    x2 = x.reshape(m, d)

    h = _ffn(x2, lw["norm_ffn1"], lw["ffn1_gate"], lw["ffn1_up"], lw["ffn1_down"], tm=tm, tf=512)
    qkv = _proj(h, lw["norm_mix"], lw["w_in_att"], lw["qk_gain"], tm=tm, tn=512, n_norm_cols=2 * att_w)
    pr = _proj(h, lw["norm_mix"], lw["w_in_rwkv"], lw["rwkv_gain"], tm=tm, tn=lw["w_in_rwkv"].shape[1] // 2,
               n_norm_cols=0)

    clen = _pick(t, (CHUNK, 16))
    tq = _pick(t, (CHUNK, 16))
    att = _attention(qkv.reshape(b, t, 3 * att_w), ctx_k, ctx_v, _band_bias(lw["rel_bias"], tq),
                     tq=tq, chunk=min(CHUNK, tq), left_valid=left_valid)
    pr3 = pr.reshape(b, t, -1)
    rw, wkv_new = _wkv(pr3, shift_prev, wkv_prev, lw["rwkv_mu"], lw["rwkv_w0"], lw["rwkv_w_lora"],
                       lw["rwkv_a0"], lw["rwkv_a_lora"], lw["rwkv_g_lora"], lw["rwkv_k_k"], lw["rwkv_k_a"],
                       lw["rwkv_r_k"], lw["rwkv_lnx_w"], lw["rwkv_lnx_b"], clen=clen)

    h = _outproj(h, att.reshape(m, att_w), rw.reshape(m, -1), lw["w_att_out"], lw["w_rw_out"],
                 tm=_pick(m, (256, 128)))
    h = _ffn(h, lw["norm_ffn2"], lw["ffn2_gate"], lw["ffn2_up"], lw["ffn2_down"], tm=tm, tf=512)
    h = _ple(h, lw["norm_ple"], lw["ple_gate"], p.reshape(m, -1), lw["ple_proj"], tm=tm, tn=512)

    heads = att_w // ATT_HEAD_DIM
    keep = max(t - LEFT, 0)
    qkv3 = qkv.reshape(b, t, 3 * att_w)
    k_keep = qkv3[:, keep:, att_w:2 * att_w].reshape(b, t - keep, heads, ATT_HEAD_DIM)
    v_keep = qkv3[:, keep:, 2 * att_w:].reshape(b, t - keep, heads, ATT_HEAD_DIM)
    return h.reshape(b, t, d), (k_keep, v_keep, wkv_new, pr3[:, -1:])


def kernel(x_prompt, x_sample, cache_att_k, cache_att_v, state_wkv, state_shift, p_prompt, p_sample, norm_ffn1, ffn1_gate, ffn1_up, ffn1_down, norm_mix, w_in, q_norm, k_norm, rel_bias, rwkv_mu, rwkv_w0, rwkv_w_lora, rwkv_a0, rwkv_a_lora, rwkv_g_lora, rwkv_k_k, rwkv_k_a, rwkv_r_k, rwkv_lnx_w, rwkv_lnx_b, w_out, norm_ffn2, ffn2_gate, ffn2_up, ffn2_down, norm_ple, ple_gate, ple_proj):
    depth = norm_ffn1.shape[0]
    hp, hs = x_prompt, x_sample
    bp = hp.shape[0]
    bs = hs.shape[0]
    att_w = w_out.shape[1] // 2
    heads = att_w // ATT_HEAD_DIM
    rwkv_heads = rwkv_r_k.shape[1]
    rwkv_pw = rwkv_mu.shape[1]
    prompt_states, sample_states = [], []
    for i in range(depth):
        lw = {
            "norm_ffn1": norm_ffn1[i], "ffn1_gate": ffn1_gate[i].astype(BF16), "ffn1_up": ffn1_up[i].astype(BF16),
            "ffn1_down": ffn1_down[i].astype(BF16), "norm_mix": norm_mix[i],
            "w_in_att": w_in[i][:, :3 * att_w].astype(BF16), "w_in_rwkv": w_in[i][:, 3 * att_w:].astype(BF16),
            "qk_gain": jnp.concatenate([jnp.tile(q_norm[i], heads), jnp.tile(k_norm[i], heads),
                                        jnp.ones((att_w,), F32)]),
            "rwkv_gain": jnp.ones((rwkv_pw,), F32),
            "rel_bias": rel_bias[i], "rwkv_mu": rwkv_mu[i], "rwkv_w0": rwkv_w0[i],
            "rwkv_w_lora": rwkv_w_lora[i].astype(BF16), "rwkv_a0": rwkv_a0[i],
            "rwkv_a_lora": rwkv_a_lora[i].astype(BF16), "rwkv_g_lora": rwkv_g_lora[i].astype(BF16),
            "rwkv_k_k": rwkv_k_k[i], "rwkv_k_a": rwkv_k_a[i], "rwkv_r_k": rwkv_r_k[i].reshape(-1),
            "rwkv_lnx_w": rwkv_lnx_w[i], "rwkv_lnx_b": rwkv_lnx_b[i],
            "w_att_out": w_out[i][:att_w].astype(BF16), "w_rw_out": w_out[i][att_w:].astype(BF16),
            "norm_ffn2": norm_ffn2[i], "ffn2_gate": ffn2_gate[i].astype(BF16), "ffn2_up": ffn2_up[i].astype(BF16),
            "ffn2_down": ffn2_down[i].astype(BF16), "norm_ple": norm_ple[i],
            "ple_gate": ple_gate[i].astype(BF16), "ple_proj": ple_proj[i].astype(BF16),
        }
        zero_ctx = jnp.zeros((1, LEFT, att_w), F32)
        zero_shift = jnp.zeros((bp, 1, rwkv_pw), hp.dtype)
        zero_wkv = jnp.zeros((bp, rwkv_heads, RWKV_HEAD_DIM, RWKV_HEAD_DIM), hp.dtype)
        hp, sp = _trunk_layer(hp, p_prompt[i], lw, zero_ctx, zero_ctx, False, zero_shift, zero_wkv)
        ctx_k = cache_att_k[i].reshape(bs, -1, att_w)
        ctx_v = cache_att_v[i].reshape(bs, -1, att_w)
        hs, ss = _trunk_layer(hs, p_sample[i], lw, ctx_k, ctx_v, True, state_shift[i], state_wkv[i])
        prompt_states.append(sp)
        sample_states.append(ss)
    stack = lambda states, j: jnp.stack([s[j] for s in states])
    return (hp, hs, stack(prompt_states, 0), stack(prompt_states, 1), stack(prompt_states, 2),
            stack(prompt_states, 3), stack(sample_states, 0), stack(sample_states, 1),
            stack(sample_states, 2), stack(sample_states, 3))
```

```python
import functools
import math

import jax
import jax.numpy as jnp
from jax import lax
from jax.experimental import pallas as pl
from jax.experimental.pallas import tpu as pltpu

F32 = jnp.float32
BF16 = jnp.bfloat16

CHUNK = 64
LEFT = 512
ATT_HEAD_DIM = 128
REL_CLIP = 256
RWKV_HEAD_DIM = 64
W_LORA = 64
A_LORA = 64
G_LORA = 128
RMS_EPS = 1e-6
GN_EPS = 64e-5
NEG_INF = -1e30

V7X_VMEM_BYTES = 64 * 1024 * 1024
VMEM_LIMIT_CAP = 56 * 1024 * 1024


def _vmem_limit(block_bytes, scratch_bytes, temp_bytes):
    need = 2 * block_bytes + scratch_bytes + temp_bytes + (4 << 20)
    return int(min(max(need, 16 << 20), VMEM_LIMIT_CAP))


def _nbytes(shape, dtype):
    return math.prod(shape) * jnp.dtype(dtype).itemsize


def _rms_norm_rows(x, gain):
    ms = jnp.mean(x * x, axis=-1, keepdims=True)
    return x * lax.rsqrt(ms + RMS_EPS) * gain


def _mm(a, b):
    return jnp.dot(a, b, preferred_element_type=F32)


def _mm_nt(a, b):
    return lax.dot_general(a, b, (((1,), (1,)), ((), ())), preferred_element_type=F32)


def _mm_tn(a, b):
    return lax.dot_general(a, b, (((0,), (0,)), ((), ())), preferred_element_type=F32)


def _ffn_body(x_ref, g_ref, wg_ref, wu_ref, wd_ref, o_ref, xn_ref):
    j = pl.program_id(1)
    last = pl.num_programs(1) - 1

    @pl.when(j == 0)
    def _():
        xn_ref[...] = _rms_norm_rows(x_ref[...], g_ref[...]).astype(BF16)

    xn = xn_ref[...]
    a = _mm(xn, wg_ref[...])
    u = _mm(xn, wu_ref[...])
    h = (a * jax.nn.sigmoid(a) * u).astype(BF16)
    d = _mm(h, wd_ref[...])

    @pl.when(j == 0)
    def _():
        o_ref[...] = d

    @pl.when(j > 0)
    def _():
        o_ref[...] += d

    @pl.when(j == last)
    def _():
        o_ref[...] = x_ref[...] + 0.5 * o_ref[...]


def _ffn(x, gain, wg, wu, wd, *, tm, tf):
    m, d = x.shape
    f = wg.shape[1]
    assert m % tm == 0 and f % tf == 0
    blocks = (_nbytes((tm, d), F32) * 2 + _nbytes((d, tf), BF16) * 3)
    temps = _nbytes((tm, tf), F32) * 4
    return pl.pallas_call(
        _ffn_body,
        grid=(m // tm, f // tf),
        in_specs=[
            pl.BlockSpec((tm, d), lambda i, j: (i, 0)),
            pl.BlockSpec((1, d), lambda i, j: (0, 0)),
            pl.BlockSpec((d, tf), lambda i, j: (0, j)),
            pl.BlockSpec((d, tf), lambda i, j: (0, j)),
            pl.BlockSpec((tf, d), lambda i, j: (j, 0)),
        ],
        out_specs=pl.BlockSpec((tm, d), lambda i, j: (i, 0)),
        out_shape=jax.ShapeDtypeStruct((m, d), F32),
        scratch_shapes=[pltpu.VMEM((tm, d), BF16)],
        compiler_params=pltpu.CompilerParams(
            dimension_semantics=("parallel", "arbitrary"),
            vmem_limit_bytes=_vmem_limit(blocks, _nbytes((tm, d), BF16), temps)),
        name="ffn",
    )(x, gain.reshape(1, d), wg, wu, wd)


def _proj_body(x_ref, g_ref, w_ref, hg_ref, o_ref, xn_ref, *, n_norm_blocks, head_dim):
    j = pl.program_id(1)

    @pl.when(j == 0)
    def _():
        xn_ref[...] = _rms_norm_rows(x_ref[...], g_ref[...]).astype(BF16)

    acc = _mm(xn_ref[...], w_ref[...])
    tn = acc.shape[1]
    if n_norm_blocks == 0:
        o_ref[...] = acc
        return

    @pl.when(j < n_norm_blocks)
    def _():
        for c in range(tn // head_dim):
            sl = slice(c * head_dim, (c + 1) * head_dim)
            o_ref[:, sl] = _rms_norm_rows(acc[:, sl], hg_ref[:, sl])

    @pl.when(j >= n_norm_blocks)
    def _():
        o_ref[...] = acc


def _proj(x, gain, w, head_gain, *, tm, tn, n_norm_cols):
    m, d = x.shape
    n = w.shape[1]
    assert m % tm == 0 and n % tn == 0 and n_norm_cols % tn == 0 and tn % ATT_HEAD_DIM == 0
    blocks = _nbytes((tm, d), F32) + _nbytes((d, tn), BF16) + _nbytes((tm, tn), F32)
    return pl.pallas_call(
        functools.partial(_proj_body, n_norm_blocks=n_norm_cols // tn, head_dim=ATT_HEAD_DIM),
        grid=(m // tm, n // tn),
        in_specs=[
            pl.BlockSpec((tm, d), lambda i, j: (i, 0)),
            pl.BlockSpec((1, d), lambda i, j: (0, 0)),
            pl.BlockSpec((d, tn), lambda i, j: (0, j)),
            pl.BlockSpec((1, tn), lambda i, j: (0, j)),
        ],
        out_specs=pl.BlockSpec((tm, tn), lambda i, j: (i, j)),
        out_shape=jax.ShapeDtypeStruct((m, n), F32),
        scratch_shapes=[pltpu.VMEM((tm, d), BF16)],
        compiler_params=pltpu.CompilerParams(
            dimension_semantics=("parallel", "arbitrary"),
            vmem_limit_bytes=_vmem_limit(blocks, _nbytes((tm, d), BF16), 3 * _nbytes((tm, tn), F32))),
        name="proj",
    )(x, gain.reshape(1, d), w, head_gain.reshape(1, n))


def _attn_body(q_ref, k_ref, v_ref, ck_ref, cv_ref, bvec_ref, o_ref, kp_ref, vp_ref, bias_ref,
               *, seq, tq, chunk, left_valid, group):
    kp_ref[0:LEFT, :] = ck_ref[0].astype(BF16)
    vp_ref[0:LEFT, :] = cv_ref[0].astype(BF16)
    kp_ref[LEFT:LEFT + seq, :] = k_ref[0].astype(BF16)
    vp_ref[LEFT:LEFT + seq, :] = v_ref[0].astype(BF16)

    width = LEFT + tq
    bw = bvec_ref.shape[-1]
    bias = pltpu.roll(jnp.broadcast_to(bvec_ref[0], (tq, bw)), 0, 1, stride=1, stride_axis=0)[:, :width]
    if tq > chunk:
        row = lax.broadcasted_iota(jnp.int32, (tq, width), 0)
        col = lax.broadcasted_iota(jnp.int32, (tq, width), 1)
        chunk_start = (row // chunk) * chunk
        in_band = (col >= chunk_start) & (col < chunk_start + LEFT + chunk)
        bias = jnp.where(in_band, bias, NEG_INF)
    bias_ref[...] = bias
    scale = ATT_HEAD_DIM ** -0.5

    def blocks(t, mask_left):
        bases = [pl.multiple_of((t * group + u) * tq, tq) for u in range(group)]
        scores = [_mm_nt(q_ref[0, pl.ds(base, tq), :].astype(BF16), kp_ref[pl.ds(base, width), :])
                  for base in bases]
        probs = []
        for s, base in zip(scores, bases):
            band = bias_ref[...]
            ok = band > 0.5 * NEG_INF
            if mask_left:
                col = lax.broadcasted_iota(jnp.int32, (tq, width), 1)
                ok = ok & (col >= LEFT - base)
            s = jnp.where(ok, s * scale + band, NEG_INF)
            e = jnp.exp(s - jnp.max(s, axis=-1, keepdims=True))
            probs.append((e * (1.0 / jnp.sum(e, axis=-1, keepdims=True))).astype(BF16))
        for p, base in zip(probs, bases):
            o_ref[0, pl.ds(base, tq), :] = _mm(p, vp_ref[pl.ds(base, width), :])

    n_iter = seq // (tq * group)
    n_left = 0 if left_valid else min(n_iter, -(-LEFT // (tq * group)))

    def masked(t, carry):
        blocks(t, True)
        return carry

    def plain(t, carry):
        blocks(t, False)
        return carry

    lax.fori_loop(0, n_left, masked, 0)
    lax.fori_loop(n_left, n_iter, plain, 0)


def _attention(qkv, ctx_k, ctx_v, bias_vec, *, tq, chunk, left_valid, group):
    b, seq, three_w = qkv.shape
    dh = ATT_HEAD_DIM
    heads = three_w // (3 * dh)
    bw = bias_vec.shape[-1]
    assert seq % (tq * group) == 0 and tq % chunk == 0 and ctx_k.shape[1] == LEFT
    per_batch_ctx = ctx_k.shape[0] == b
    ctx_map = (lambda bi, h: (bi, 0, h)) if per_batch_ctx else (lambda bi, h: (0, 0, h))
    blocks = 4 * _nbytes((seq, dh), F32) + 2 * _nbytes((LEFT, dh), F32) + _nbytes((8, bw), F32)
    scratch = 2 * _nbytes((LEFT + seq, dh), BF16) + _nbytes((tq, LEFT + tq), F32)
    return pl.pallas_call(
        functools.partial(_attn_body, seq=seq, tq=tq, chunk=chunk, left_valid=left_valid, group=group),
        grid=(b, heads),
        in_specs=[
            pl.BlockSpec((1, seq, dh), lambda bi, h: (bi, 0, h)),
            pl.BlockSpec((1, seq, dh), lambda bi, h: (bi, 0, heads + h)),
            pl.BlockSpec((1, seq, dh), lambda bi, h: (bi, 0, 2 * heads + h)),
            pl.BlockSpec((1, LEFT, dh), ctx_map),
            pl.BlockSpec((1, LEFT, dh), ctx_map),
            pl.BlockSpec((1, 1, bw), lambda bi, h: (h, 0, 0)),
        ],
        out_specs=pl.BlockSpec((1, seq, dh), lambda bi, h: (bi, 0, h)),
        out_shape=jax.ShapeDtypeStruct((b, seq, heads * dh), F32),
        scratch_shapes=[pltpu.VMEM((LEFT + seq, dh), BF16), pltpu.VMEM((LEFT + seq, dh), BF16),
                        pltpu.VMEM((tq, LEFT + tq), F32)],
        compiler_params=pltpu.CompilerParams(
            dimension_semantics=("parallel", "parallel"),
            vmem_limit_bytes=_vmem_limit(blocks, scratch, 8 * group * _nbytes((tq, LEFT + tq), F32))),
        name="attn",
    )(qkv, qkv, qkv, ctx_k, ctx_v, bias_vec)


def _band_bias_vec(rel_bias, tq):
    width = LEFT + tq
    bw = -(-(width + tq) // 128) * 128
    m = jnp.arange(bw)
    d = jnp.where(m < width, m, m - bw)
    idx = jnp.clip(LEFT - d, -REL_CLIP, REL_CLIP) + REL_CLIP
    return rel_bias[idx].T.astype(F32)[:, None, :]


def _split3(x):
    h1 = x.astype(BF16)
    r1 = x - h1.astype(F32)
    h2 = r1.astype(BF16)
    h3 = (r1 - h2.astype(F32)).astype(BF16)
    return h1, h2, h3


def _wkv_body(x_ref, sp_ref, s0_ref, mu_ref, w0_ref, wl_ref, a0_ref, al_ref, gl_ref, kk_ref, ka_ref,
              rk_ref, lnw_ref, lnb_ref, o_ref, so_ref, carry_ref, s_ref, *, clen, heads):
    c = pl.program_id(1)
    n = RWKV_HEAD_DIM
    width = heads * n

    @pl.when(c == 0)
    def _():
        carry_ref[...] = sp_ref[0]
        s_ref[...] = s0_ref[0]

    xb = x_ref[0]
    trow = lax.broadcasted_iota(jnp.int32, (clen, 1), 0)
    shifted = jnp.where(trow == 0, carry_ref[...], pltpu.roll(xb, 1, 0))
    carry_ref[...] = xb[clen - 1:clen, :]
    xm = xb + mu_ref[...] * (shifted - xb)

    r = xm[:, 0:width]
    k = xm[:, width:2 * width]
    v = xm[:, 2 * width:3 * width]
    o3 = 3 * width
    w_lo = xm[:, o3:o3 + W_LORA]
    a_lo = xm[:, o3 + W_LORA:o3 + W_LORA + A_LORA]
    g_lo = xm[:, o3 + W_LORA + A_LORA:o3 + W_LORA + A_LORA + G_LORA]

    z = -(w0_ref[...] + _mm(jnp.tanh(w_lo).astype(BF16), wl_ref[...]))
    softplus_z = jnp.maximum(z, 0.0) + jnp.log1p(jnp.exp(-jnp.abs(z)))
    log_decay = -jnp.exp(-softplus_z - 0.5)
    a = jax.nn.sigmoid(a0_ref[...] + _mm(a_lo.astype(BF16), al_ref[...]))
    g = _mm(jax.nn.sigmoid(g_lo).astype(BF16), gl_ref[...])
    kk = k * kk_ref[...]
    k = k * (1.0 + (a - 1.0) * ka_ref[...])
    rk = r * k * rk_ref[...]

    ti = lax.broadcasted_iota(jnp.int32, (clen, clen), 0)
    si = lax.broadcasted_iota(jnp.int32, (clen, clen), 1)
    lower = si <= ti
    strict = si < ti
    tri = lower.astype(BF16)
    d1, d2, d3 = _split3(log_decay)
    cum = _mm(tri, d1) + _mm(tri, d2) + _mm(tri, d3)
    g_in = jnp.exp(cum)
    g_inv = jnp.exp(-cum)
    g_ex = jnp.exp(cum - log_decay)
    g_last = g_in[clen - 1:clen, :]

    r_t = r * g_in
    k_h = k * g_inv
    k_l = k_h * g_last
    levels = max(1, (clen - 1).bit_length())

    hs = range(heads)
    sls = [slice(h * n, (h + 1) * n) for h in hs]
    bf = lambda t: t.astype(BF16)

    b_hat, a_t, v_bf, s_prev, s_bf = [], [], [], [], []
    for h, sl in zip(hs, sls):
        kk_h = kk[:, sl]
        kk_h = kk_h / jnp.maximum(jnp.sqrt(jnp.sum(kk_h * kk_h, axis=-1, keepdims=True)), 1e-12)
        b_hat.append(kk_h * a[:, sl] * g_inv[:, sl])
        a_t.append(-kk_h * g_ex[:, sl])
        v_bf.append(bf(v[:, sl]))
        s_prev.append(s_ref[h])
        s_bf.append(bf(s_prev[h]))

    p = [_mm_nt(bf(jnp.concatenate([a_t[h], r_t[:, sls[h]]], axis=0)),
                bf(jnp.concatenate([b_hat[h], k_h[:, sls[h]]], axis=0))) for h in hs]
    nk = [bf(jnp.where(strict, p[h][:clen, :clen], 0.0)) for h in hs]
    a_ak = [bf(jnp.where(strict, p[h][:clen, clen:], 0.0)) for h in hs]
    a_rb = [bf(jnp.where(lower, p[h][clen:, :clen], 0.0)) for h in hs]
    a_rk = [bf(jnp.where(lower, p[h][clen:, clen:], 0.0)) for h in hs]

    sa = [_mm_nt(bf(a_t[h]), s_bf[h]) + _mm(a_ak[h], v_bf[h]) for h in hs]
    for lvl in range(levels):
        sa = [sa[h] + _mm(nk[h], bf(sa[h])) for h in hs]
        if lvl + 1 < levels:
            nk = [bf(_mm(nk[h], nk[h])) for h in hs]

    sa_bf = [bf(sa[h]) for h in hs]
    ys = [_mm_nt(bf(r_t[:, sls[h]]), s_bf[h]) + _mm(a_rb[h], sa_bf[h]) + _mm(a_rk[h], v_bf[h]) for h in hs]
    for h, sl in zip(hs, sls):
        s_ref[h] = (s_prev[h] * g_last[:, sl] + _mm_tn(sa_bf[h], bf(b_hat[h] * g_last[:, sl]))
                    + _mm_tn(v_bf[h], bf(k_l[:, sl])))

    outs = []
    for h, sl in zip(hs, sls):
        y = ys[h]
        mean = jnp.mean(y, axis=-1, keepdims=True)
        var = jnp.mean(jnp.square(y - mean), axis=-1, keepdims=True)
        y = (y - mean) * lax.rsqrt(var + GN_EPS)
        y = y * lnw_ref[:, sl] + lnb_ref[:, sl]
        y = y + jnp.sum(rk[:, sl], axis=-1, keepdims=True) * v[:, sl]
        outs.append(y)

    o_ref[0] = jnp.concatenate(outs, axis=1) * g

    @pl.when(c == pl.num_programs(1) - 1)
    def _():
        so_ref[0] = s_ref[...]


def _wkv(pr, shift_prev, state0, mu, w0, w_lora, a0, a_lora, g_lora, k_k, k_a, r_k, lnx_w, lnx_b, *, clen):
    b, t, pw = pr.shape
    heads = state0.shape[1]
    n = RWKV_HEAD_DIM
    width = heads * n
    assert t % clen == 0 and pw == 3 * width + W_LORA + A_LORA + G_LORA
    row = lambda a: a.reshape(1, -1)
    full = lambda shape: pl.BlockSpec(shape, lambda bi, c: (0,) * len(shape))
    blocks = _nbytes((clen, pw), F32) + _nbytes((clen, width), F32) + 2 * _nbytes((heads, n, 128), F32)
    return pl.pallas_call(
        functools.partial(_wkv_body, clen=clen, heads=heads),
        grid=(b, t // clen),
        in_specs=[
            pl.BlockSpec((1, clen, pw), lambda bi, c: (bi, c, 0)),
            pl.BlockSpec((1, 1, pw), lambda bi, c: (bi, 0, 0)),
            pl.BlockSpec((1, heads, n, n), lambda bi, c: (bi, 0, 0, 0)),
            full((1, pw)), full((1, width)), full((W_LORA, width)), full((1, width)), full((A_LORA, width)),
            full((G_LORA, width)), full((1, width)), full((1, width)), full((1, width)), full((1, width)),
            full((1, width)),
        ],
        out_specs=[
            pl.BlockSpec((1, clen, width), lambda bi, c: (bi, c, 0)),
            pl.BlockSpec((1, heads, n, n), lambda bi, c: (bi, 0, 0, 0)),
        ],
        out_shape=[jax.ShapeDtypeStruct((b, t, width), F32), jax.ShapeDtypeStruct((b, heads, n, n), F32)],
        scratch_shapes=[pltpu.VMEM((1, pw), F32), pltpu.VMEM((heads, n, n), F32)],
        compiler_params=pltpu.CompilerParams(
            dimension_semantics=("parallel", "arbitrary"),
            vmem_limit_bytes=_vmem_limit(blocks, 2 * _nbytes((heads, n, 128), F32),
                                         40 * _nbytes((clen, width), F32))),
        name="wkv",
    )(pr, shift_prev, state0, row(mu), row(w0), w_lora, row(a0), a_lora, g_lora, row(k_k), row(k_a),
      row(r_k), row(lnx_w), row(lnx_b))


def _outproj_body(h_ref, att_ref, rw_ref, wa_ref, wr_ref, o_ref):
    o_ref[...] = (h_ref[...] + _mm(att_ref[...].astype(BF16), wa_ref[...])
                  + _mm(rw_ref[...].astype(BF16), wr_ref[...]))


def _outproj(h, att, rw, w_att, w_rw, *, tm):
    m, d = h.shape
    ka, kr = att.shape[1], rw.shape[1]
    assert m % tm == 0
    blocks = (2 * _nbytes((tm, d), F32) + _nbytes((tm, ka), F32) + _nbytes((tm, kr), F32)
              + _nbytes((ka, d), BF16) + _nbytes((kr, d), BF16))
    return pl.pallas_call(
        _outproj_body,
        grid=(m // tm,),
        in_specs=[
            pl.BlockSpec((tm, d), lambda i: (i, 0)),
            pl.BlockSpec((tm, ka), lambda i: (i, 0)),
            pl.BlockSpec((tm, kr), lambda i: (i, 0)),
            pl.BlockSpec((ka, d), lambda i: (0, 0)),
            pl.BlockSpec((kr, d), lambda i: (0, 0)),
        ],
        out_specs=pl.BlockSpec((tm, d), lambda i: (i, 0)),
        out_shape=jax.ShapeDtypeStruct((m, d), F32),
        compiler_params=pltpu.CompilerParams(
            dimension_semantics=("parallel",),
            vmem_limit_bytes=_vmem_limit(blocks, 0, 2 * _nbytes((tm, d), F32))),
        name="outproj",
    )(h, att, rw, w_att, w_rw)


def _ple_body(x_ref, g_ref, wg_ref, p_ref, wp_ref, o_ref, xn_ref):
    j = pl.program_id(1)
    tn = o_ref.shape[1]

    @pl.when(j == 0)
    def _():
        xn_ref[...] = _rms_norm_rows(x_ref[...], g_ref[...]).astype(BF16)

    gate = jax.nn.sigmoid(_mm(xn_ref[...], wg_ref[...]))
    emb = _mm(p_ref[...].astype(BF16), wp_ref[...])
    col = pl.multiple_of(j * tn, tn)
    o_ref[...] = x_ref[:, pl.ds(col, tn)] + gate * emb


def _ple(x, gain, w_gate, p, w_proj, *, tm, tn):
    m, d = x.shape
    pd = p.shape[1]
    assert m % tm == 0 and d % tn == 0
    blocks = (_nbytes((tm, d), F32) + _nbytes((d, tn), BF16) + _nbytes((tm, pd), F32)
              + _nbytes((pd, tn), BF16) + _nbytes((tm, tn), F32))
    return pl.pallas_call(
        _ple_body,
        grid=(m // tm, d // tn),
        in_specs=[
            pl.BlockSpec((tm, d), lambda i, j: (i, 0)),
            pl.BlockSpec((1, d), lambda i, j: (0, 0)),
            pl.BlockSpec((d, tn), lambda i, j: (0, j)),
            pl.BlockSpec((tm, pd), lambda i, j: (i, 0)),
            pl.BlockSpec((pd, tn), lambda i, j: (0, j)),
        ],
        out_specs=pl.BlockSpec((tm, tn), lambda i, j: (i, j)),
        out_shape=jax.ShapeDtypeStruct((m, d), F32),
        scratch_shapes=[pltpu.VMEM((tm, d), BF16)],
        compiler_params=pltpu.CompilerParams(
            dimension_semantics=("parallel", "arbitrary"),
            vmem_limit_bytes=_vmem_limit(blocks, _nbytes((tm, d), BF16), 4 * _nbytes((tm, tn), F32))),
        name="ple",
    )(x, gain.reshape(1, d), w_gate, p, w_proj)


def _pick(m, candidates):
    for c in candidates:
        if m % c == 0:
            return c
    raise ValueError(f"no tile for {m}")


def _trunk_layer(x, p, lw, ctx_k, ctx_v, left_valid, shift_prev, wkv_prev):
    b, t, d = x.shape
    m = b * t
    att_w = lw["w_att_out"].shape[0]
    tm = _pick(m, (512, 256, 128))
    x2 = x.reshape(m, d)

    h = _ffn(x2, lw["norm_ffn1"], lw["ffn1_gate"], lw["ffn1_up"], lw["ffn1_down"], tm=tm, tf=512)
    qkv = _proj(h, lw["norm_mix"], lw["w_in_att"], lw["qk_gain"], tm=tm, tn=512, n_norm_cols=2 * att_w)
    pr = _proj(h, lw["norm_mix"], lw["w_in_rwkv"], lw["rwkv_gain"], tm=tm, tn=lw["w_in_rwkv"].shape[1] // 2,
               n_norm_cols=0)

    clen = _pick(t, (CHUNK, 16))
    tq = _pick(t, (4 * CHUNK, CHUNK, 16))
    group = 2 if t % (2 * tq) == 0 and LEFT % (2 * tq) == 0 else 1
    att = _attention(qkv.reshape(b, t, 3 * att_w), ctx_k, ctx_v, _band_bias_vec(lw["rel_bias"], tq),
                     tq=tq, chunk=min(CHUNK, tq), left_valid=left_valid, group=group)
    pr3 = pr.reshape(b, t, -1)
    rw, wkv_new = _wkv(pr3, shift_prev, wkv_prev, lw["rwkv_mu"], lw["rwkv_w0"], lw["rwkv_w_lora"],
                       lw["rwkv_a0"], lw["rwkv_a_lora"], lw["rwkv_g_lora"], lw["rwkv_k_k"], lw["rwkv_k_a"],
                       lw["rwkv_r_k"], lw["rwkv_lnx_w"], lw["rwkv_lnx_b"], clen=clen)

    h = _outproj(h, att.reshape(m, att_w), rw.reshape(m, -1), lw["w_att_out"], lw["w_rw_out"],
                 tm=_pick(m, (256, 128)))
    h = _ffn(h, lw["norm_ffn2"], lw["ffn2_gate"], lw["ffn2_up"], lw["ffn2_down"], tm=tm, tf=512)
    h = _ple(h, lw["norm_ple"], lw["ple_gate"], p.reshape(m, -1), lw["ple_proj"], tm=tm, tn=512)

    heads = att_w // ATT_HEAD_DIM
    keep = max(t - LEFT, 0)
    qkv3 = qkv.reshape(b, t, 3 * att_w)
    k_keep = qkv3[:, keep:, att_w:2 * att_w].reshape(b, t - keep, heads, ATT_HEAD_DIM)
    v_keep = qkv3[:, keep:, 2 * att_w:].reshape(b, t - keep, heads, ATT_HEAD_DIM)
    return h.reshape(b, t, d), (k_keep, v_keep, wkv_new, pr3[:, -1:])


def kernel(x_prompt, x_sample, cache_att_k, cache_att_v, state_wkv, state_shift, p_prompt, p_sample, norm_ffn1, ffn1_gate, ffn1_up, ffn1_down, norm_mix, w_in, q_norm, k_norm, rel_bias, rwkv_mu, rwkv_w0, rwkv_w_lora, rwkv_a0, rwkv_a_lora, rwkv_g_lora, rwkv_k_k, rwkv_k_a, rwkv_r_k, rwkv_lnx_w, rwkv_lnx_b, w_out, norm_ffn2, ffn2_gate, ffn2_up, ffn2_down, norm_ple, ple_gate, ple_proj):
    depth = norm_ffn1.shape[0]
    hp, hs = x_prompt, x_sample
    bp = hp.shape[0]
    bs = hs.shape[0]
    att_w = w_out.shape[1] // 2
    heads = att_w // ATT_HEAD_DIM
    rwkv_heads = rwkv_r_k.shape[1]
    rwkv_pw = rwkv_mu.shape[1]
    prompt_states, sample_states = [], []
    for i in range(depth):
        lw = {
            "norm_ffn1": norm_ffn1[i], "ffn1_gate": ffn1_gate[i].astype(BF16), "ffn1_up": ffn1_up[i].astype(BF16),
            "ffn1_down": ffn1_down[i].astype(BF16), "norm_mix": norm_mix[i],
            "w_in_att": w_in[i][:, :3 * att_w].astype(BF16), "w_in_rwkv": w_in[i][:, 3 * att_w:].astype(BF16),
            "qk_gain": jnp.concatenate([jnp.tile(q_norm[i], heads), jnp.tile(k_norm[i], heads),
                                        jnp.ones((att_w,), F32)]),
            "rwkv_gain": jnp.ones((rwkv_pw,), F32),
            "rel_bias": rel_bias[i], "rwkv_mu": rwkv_mu[i], "rwkv_w0": rwkv_w0[i],
            "rwkv_w_lora": rwkv_w_lora[i].astype(BF16), "rwkv_a0": rwkv_a0[i],
            "rwkv_a_lora": rwkv_a_lora[i].astype(BF16), "rwkv_g_lora": rwkv_g_lora[i].astype(BF16),
            "rwkv_k_k": rwkv_k_k[i], "rwkv_k_a": rwkv_k_a[i], "rwkv_r_k": rwkv_r_k[i].reshape(-1),
            "rwkv_lnx_w": rwkv_lnx_w[i], "rwkv_lnx_b": rwkv_lnx_b[i],
            "w_att_out": w_out[i][:att_w].astype(BF16), "w_rw_out": w_out[i][att_w:].astype(BF16),
            "norm_ffn2": norm_ffn2[i], "ffn2_gate": ffn2_gate[i].astype(BF16), "ffn2_up": ffn2_up[i].astype(BF16),
            "ffn2_down": ffn2_down[i].astype(BF16), "norm_ple": norm_ple[i],
            "ple_gate": ple_gate[i].astype(BF16), "ple_proj": ple_proj[i].astype(BF16),
        }
        zero_ctx = jnp.zeros((1, LEFT, att_w), F32)
        zero_shift = jnp.zeros((bp, 1, rwkv_pw), hp.dtype)
        zero_wkv = jnp.zeros((bp, rwkv_heads, RWKV_HEAD_DIM, RWKV_HEAD_DIM), hp.dtype)
        hp, sp = _trunk_layer(hp, p_prompt[i], lw, zero_ctx, zero_ctx, False, zero_shift, zero_wkv)
        ctx_k = cache_att_k[i].reshape(bs, -1, att_w)
        ctx_v = cache_att_v[i].reshape(bs, -1, att_w)
        hs, ss = _trunk_layer(hs, p_sample[i], lw, ctx_k, ctx_v, True, state_shift[i], state_wkv[i])
        prompt_states.append(sp)
        sample_states.append(ss)
    stack = lambda states, j: jnp.stack([s[j] for s in states])
    return (hp, hs, stack(prompt_states, 0), stack(prompt_states, 1), stack(prompt_states, 2),
            stack(prompt_states, 3), stack(sample_states, 0), stack(sample_states, 1),
            stack(sample_states, 2), stack(sample_states, 3))
```

```python
import functools
import math

import jax
import jax.numpy as jnp
from jax import lax
from jax.experimental import pallas as pl
from jax.experimental.pallas import tpu as pltpu

F32 = jnp.float32
BF16 = jnp.bfloat16

CHUNK = 64
LEFT = 512
ATT_HEAD_DIM = 128
REL_CLIP = 256
RWKV_HEAD_DIM = 64
W_LORA = 64
A_LORA = 64
G_LORA = 128
RMS_EPS = 1e-6
GN_EPS = 64e-5
NEG_INF = -1e30

LANES = 128
V7X_VMEM_BYTES = 64 * 1024 * 1024
VMEM_LIMIT_CAP = V7X_VMEM_BYTES - 8 * 1024 * 1024


def _vmem_limit(block_bytes, scratch_bytes, temp_bytes):
    need = 2 * block_bytes + scratch_bytes + temp_bytes + (4 << 20)
    return int(min(max(need, 16 << 20), VMEM_LIMIT_CAP))


def _nbytes(shape, dtype):
    return math.prod(shape) * jnp.dtype(dtype).itemsize


def _rms_norm_rows(x, gain):
    ms = jnp.mean(x * x, axis=-1, keepdims=True)
    return x * lax.rsqrt(ms + RMS_EPS) * gain


def _mm(a, b):
    return jnp.dot(a, b, preferred_element_type=F32)


def _mm_nt(a, b):
    return lax.dot_general(a, b, (((1,), (1,)), ((), ())), preferred_element_type=F32)


def _mm_tn(a, b):
    return lax.dot_general(a, b, (((0,), (0,)), ((), ())), preferred_element_type=F32)


def _ffn_body(x_ref, g_ref, wg_ref, wu_ref, wd_ref, o_ref, xn_ref):
    j = pl.program_id(1)
    last = pl.num_programs(1) - 1

    @pl.when(j == 0)
    def _():
        xn_ref[...] = _rms_norm_rows(x_ref[...], g_ref[...]).astype(BF16)
        o_ref[...] = jnp.zeros_like(o_ref)

    xn = xn_ref[...]
    a = _mm(xn, wg_ref[...])
    u = _mm(xn, wu_ref[...])
    h = (a * jax.nn.sigmoid(a) * u).astype(BF16)
    o_ref[...] += _mm(h, wd_ref[...])

    @pl.when(j == last)
    def _():
        o_ref[...] = x_ref[...] + 0.5 * o_ref[...]


def _ffn(x, gain, wg, wu, wd, *, tm, tf):
    m, d = x.shape
    f = wg.shape[1]
    assert m % tm == 0 and f % tf == 0
    blocks = (_nbytes((tm, d), F32) * 2 + _nbytes((d, tf), BF16) * 3)
    temps = _nbytes((tm, tf), F32) * 4
    return pl.pallas_call(
        _ffn_body,
        grid=(m // tm, f // tf),
        in_specs=[
            pl.BlockSpec((tm, d), lambda i, j: (i, 0)),
            pl.BlockSpec((1, d), lambda i, j: (0, 0)),
            pl.BlockSpec((d, tf), lambda i, j: (0, j)),
            pl.BlockSpec((d, tf), lambda i, j: (0, j)),
            pl.BlockSpec((tf, d), lambda i, j: (j, 0)),
        ],
        out_specs=pl.BlockSpec((tm, d), lambda i, j: (i, 0)),
        out_shape=jax.ShapeDtypeStruct((m, d), F32),
        scratch_shapes=[pltpu.VMEM((tm, d), BF16)],
        compiler_params=pltpu.CompilerParams(
            dimension_semantics=("parallel", "arbitrary"),
            vmem_limit_bytes=_vmem_limit(blocks, _nbytes((tm, d), BF16), temps)),
        name="ffn",
    )(x, gain.reshape(1, d), wg, wu, wd)


def _proj_body(x_ref, g_ref, w_ref, o_ref, xn_ref):
    @pl.when(pl.program_id(1) == 0)
    def _():
        xn_ref[...] = _rms_norm_rows(x_ref[...], g_ref[...]).astype(BF16)

    o_ref[...] = _mm(xn_ref[...], w_ref[...])


def _proj(x, gain, w, *, tm, tn):
    m, d = x.shape
    n = w.shape[1]
    assert m % tm == 0 and n % tn == 0
    blocks = _nbytes((tm, d), F32) + _nbytes((d, tn), BF16) + _nbytes((tm, tn), F32)
    return pl.pallas_call(
        _proj_body,
        grid=(m // tm, n // tn),
        in_specs=[
            pl.BlockSpec((tm, d), lambda i, j: (i, 0)),
            pl.BlockSpec((1, d), lambda i, j: (0, 0)),
            pl.BlockSpec((d, tn), lambda i, j: (0, j)),
        ],
        out_specs=pl.BlockSpec((tm, tn), lambda i, j: (i, j)),
        out_shape=jax.ShapeDtypeStruct((m, n), F32),
        scratch_shapes=[pltpu.VMEM((tm, d), BF16)],
        compiler_params=pltpu.CompilerParams(
            dimension_semantics=("parallel", "arbitrary"),
            vmem_limit_bytes=_vmem_limit(blocks, _nbytes((tm, d), BF16), 2 * _nbytes((tm, tn), F32))),
        name="proj",
    )(x, gain.reshape(1, d), w)


def _attn_body(q_ref, k_ref, v_ref, ck_ref, cv_ref, qg_ref, kg_ref, bvec_ref, o_ref, kn_ref,
               qn_ref, kp_ref, vp_ref, bias_ref, *, seq, tq, chunk, left_valid, group):
    keep = kn_ref.shape[1]
    qn_ref[...] = _rms_norm_rows(q_ref[0], qg_ref[...]).astype(BF16)
    kn = _rms_norm_rows(k_ref[0], kg_ref[...])
    kn_ref[0] = kn[seq - keep:, :]
    kp_ref[0:LEFT, :] = ck_ref[0].astype(BF16)
    vp_ref[0:LEFT, :] = cv_ref[0].astype(BF16)
    kp_ref[LEFT:LEFT + seq, :] = kn.astype(BF16)
    vp_ref[LEFT:LEFT + seq, :] = v_ref[0].astype(BF16)

    width = LEFT + tq
    bw = bvec_ref.shape[-1]
    bias = pltpu.roll(jnp.broadcast_to(bvec_ref[0], (tq, bw)), 0, 1, stride=1, stride_axis=0)[:, :width]
    if tq > chunk:
        row = lax.broadcasted_iota(jnp.int32, (tq, width), 0)
        col = lax.broadcasted_iota(jnp.int32, (tq, width), 1)
        chunk_start = (row // chunk) * chunk
        in_band = (col >= chunk_start) & (col < chunk_start + LEFT + chunk)
        bias = jnp.where(in_band, bias, NEG_INF)
    bias_ref[...] = bias
    scale = ATT_HEAD_DIM ** -0.5

    def blocks(t, mask_left):
        bases = [pl.multiple_of((t * group + u) * tq, tq) for u in range(group)]
        scores = [_mm_nt(qn_ref[pl.ds(base, tq), :], kp_ref[pl.ds(base, width), :]) for base in bases]
        probs = []
        for s, base in zip(scores, bases):
            band = bias_ref[...]
            ok = band > 0.5 * NEG_INF
            if mask_left:
                col = lax.broadcasted_iota(jnp.int32, (tq, width), 1)
                ok = ok & (col >= LEFT - base)
            s = jnp.where(ok, s * scale + band, NEG_INF)
            e = jnp.exp(s - jnp.max(s, axis=-1, keepdims=True))
            probs.append((e * (1.0 / jnp.sum(e, axis=-1, keepdims=True))).astype(BF16))
        for p, base in zip(probs, bases):
            o_ref[0, pl.ds(base, tq), :] = _mm(p, vp_ref[pl.ds(base, width), :])

    n_iter = seq // (tq * group)
    n_left = 0 if left_valid else min(n_iter, -(-LEFT // (tq * group)))

    def masked(t, carry):
        blocks(t, True)
        return carry

    def plain(t, carry):
        blocks(t, False)
        return carry

    lax.fori_loop(0, n_left, masked, 0)
    lax.fori_loop(n_left, n_iter, plain, 0)


def _attention(proj, ctx_k, ctx_v, q_gain, k_gain, bias_vec, *, heads, tq, chunk, left_valid, group):
    b, seq, _ = proj.shape
    dh = ATT_HEAD_DIM
    bw = bias_vec.shape[-1]
    keep = min(seq, LEFT)
    assert seq % (tq * group) == 0 and tq % chunk == 0 and ctx_k.shape[1] == LEFT
    per_batch_ctx = ctx_k.shape[0] == b
    ctx_map = (lambda bi, h: (bi, 0, h)) if per_batch_ctx else (lambda bi, h: (0, 0, h))
    blocks = (4 * _nbytes((seq, dh), F32) + 2 * _nbytes((LEFT, dh), F32) + _nbytes((8, bw), F32)
              + _nbytes((keep, dh), F32))
    scratch = (_nbytes((seq, dh), BF16) + 2 * _nbytes((LEFT + seq, dh), BF16)
               + _nbytes((tq, LEFT + tq), F32))
    return pl.pallas_call(
        functools.partial(_attn_body, seq=seq, tq=tq, chunk=chunk, left_valid=left_valid, group=group),
        grid=(b, heads),
        in_specs=[
            pl.BlockSpec((1, seq, dh), lambda bi, h: (bi, 0, h)),
            pl.BlockSpec((1, seq, dh), lambda bi, h: (bi, 0, heads + h)),
            pl.BlockSpec((1, seq, dh), lambda bi, h: (bi, 0, 2 * heads + h)),
            pl.BlockSpec((1, LEFT, dh), ctx_map),
            pl.BlockSpec((1, LEFT, dh), ctx_map),
            pl.BlockSpec((1, dh), lambda bi, h: (0, 0)),
            pl.BlockSpec((1, dh), lambda bi, h: (0, 0)),
            pl.BlockSpec((1, 1, bw), lambda bi, h: (h, 0, 0)),
        ],
        out_specs=[
            pl.BlockSpec((1, seq, dh), lambda bi, h: (bi, 0, h)),
            pl.BlockSpec((1, keep, dh), lambda bi, h: (bi, 0, h)),
        ],
        out_shape=[jax.ShapeDtypeStruct((b, seq, heads * dh), F32),
                   jax.ShapeDtypeStruct((b, keep, heads * dh), F32)],
        scratch_shapes=[pltpu.VMEM((seq, dh), BF16), pltpu.VMEM((LEFT + seq, dh), BF16),
                        pltpu.VMEM((LEFT + seq, dh), BF16), pltpu.VMEM((tq, LEFT + tq), F32)],
        compiler_params=pltpu.CompilerParams(
            dimension_semantics=("parallel", "parallel"),
            vmem_limit_bytes=_vmem_limit(blocks, scratch, 8 * group * _nbytes((tq, LEFT + tq), F32)
                                         + 3 * _nbytes((seq, dh), F32))),
        name="attn",
    )(proj, proj, proj, ctx_k, ctx_v, q_gain.reshape(1, dh), k_gain.reshape(1, dh), bias_vec)


def _band_bias_vec(rel_bias, tq):
    width = LEFT + tq
    bw = -(-(width + tq) // LANES) * LANES
    m = jnp.arange(bw)
    d = jnp.where(m < width, m, m - bw)
    idx = jnp.clip(LEFT - d, -REL_CLIP, REL_CLIP) + REL_CLIP
    return rel_bias[idx].T.astype(F32)[:, None, :]


def _split3(x):
    h1 = x.astype(BF16)
    r1 = x - h1.astype(F32)
    h2 = r1.astype(BF16)
    h3 = (r1 - h2.astype(F32)).astype(BF16)
    return h1, h2, h3


def _wkv_body(x_ref, sp_ref, s0_ref, mu_ref, w0_ref, wl_ref, a0_ref, al_ref, gl_ref, kk_ref, ka_ref,
              rk_ref, lnw_ref, lnb_ref, o_ref, so_ref, carry_ref, s_ref, *, clen, heads, col0):
    c = pl.program_id(1)
    n = RWKV_HEAD_DIM
    width = heads * n
    pairs = heads // 2
    pw = 2 * n
    tw = 2 * clen
    bf = lambda t: t.astype(BF16)

    @pl.when(c == 0)
    def _():
        carry_ref[...] = sp_ref[0]
        zeros = jnp.zeros((n, n), F32)
        for q in range(pairs):
            s_ref[q] = jnp.concatenate(
                [jnp.concatenate([s0_ref[0, 2 * q], zeros], axis=1),
                 jnp.concatenate([zeros, s0_ref[0, 2 * q + 1]], axis=1)], axis=0)

    xb = x_ref[0, :, col0:]
    trow = lax.broadcasted_iota(jnp.int32, (clen, 1), 0)
    shifted = jnp.where(trow == 0, carry_ref[...], pltpu.roll(xb, 1, 0))
    carry_ref[...] = xb[clen - 1:clen, :]
    xm = xb + mu_ref[...] * (shifted - xb)

    r = xm[:, 0:width]
    k = xm[:, width:2 * width]
    v = xm[:, 2 * width:3 * width]
    o3 = 3 * width
    w_lo = xm[:, o3:o3 + W_LORA]
    a_lo = xm[:, o3 + W_LORA:o3 + W_LORA + A_LORA]
    g_lo = xm[:, o3 + W_LORA + A_LORA:o3 + W_LORA + A_LORA + G_LORA]

    z = -(w0_ref[...] + _mm(bf(jnp.tanh(w_lo)), wl_ref[...]))
    softplus_z = jnp.maximum(z, 0.0) + jnp.log1p(jnp.exp(-jnp.abs(z)))
    log_decay = -jnp.exp(-softplus_z - 0.5)
    a = jax.nn.sigmoid(a0_ref[...] + _mm(bf(a_lo), al_ref[...]))
    g = _mm(bf(jax.nn.sigmoid(g_lo)), gl_ref[...])
    kk = k * kk_ref[...]
    k = k * (1.0 + (a - 1.0) * ka_ref[...])
    rk = r * k * rk_ref[...]

    ti = lax.broadcasted_iota(jnp.int32, (clen, clen), 0)
    si = lax.broadcasted_iota(jnp.int32, (clen, clen), 1)
    tri = (si <= ti).astype(BF16)
    d1, d2, d3 = _split3(log_decay)
    cum = _mm(tri, d1) + _mm(tri, d2) + _mm(tri, d3)
    g_in = jnp.exp(cum)
    g_inv = jnp.exp(-cum)
    g_ex = jnp.exp(cum - log_decay)
    g_last = g_in[clen - 1:clen, :]

    r_t = r * g_in
    k_h = k * g_inv
    k_l = k_h * g_last
    levels = max(1, (clen - 1).bit_length())

    first = lax.broadcasted_iota(jnp.int32, (1, pw), 1) < n
    first_t = lax.broadcasted_iota(jnp.int32, (1, tw), 1) < clen
    strict = (lax.broadcasted_iota(jnp.int32, (clen, tw), 1) % clen
              < lax.broadcasted_iota(jnp.int32, (clen, tw), 0))
    lower2 = (lax.broadcasted_iota(jnp.int32, (clen, 2 * tw), 1) % clen
              <= lax.broadcasted_iota(jnp.int32, (clen, 2 * tw), 0))
    same_head = ((lax.broadcasted_iota(jnp.int32, (pw, pw), 0) < n)
                 == (lax.broadcasted_iota(jnp.int32, (pw, pw), 1) < n))

    def head_sums(x):
        s0 = jnp.sum(jnp.where(first, x, 0.0), axis=-1, keepdims=True)
        s1 = jnp.sum(jnp.where(first, 0.0, x), axis=-1, keepdims=True)
        return jnp.where(first, s0, s1)

    def block_diag(x, mask):
        zero = jnp.zeros_like(x)
        return jnp.concatenate([jnp.where(mask, x, zero), jnp.where(mask, zero, x)], axis=0)

    qs = range(pairs)
    sls = [slice(q * pw, (q + 1) * pw) for q in qs]

    b_hat, a_t, v_bd, s_prev, s_bf = [], [], [], [], []
    for q, sl in zip(qs, sls):
        kk_q = kk[:, sl]
        kk_q = kk_q * (1.0 / jnp.maximum(jnp.sqrt(head_sums(kk_q * kk_q)), 1e-12))
        b_hat.append(kk_q * a[:, sl] * g_inv[:, sl])
        a_t.append(-kk_q * g_ex[:, sl])
        v_bd.append(block_diag(bf(v[:, sl]), first))
        s_prev.append(s_ref[q])
        s_bf.append(bf(s_prev[q]))

    p = [_mm_nt(bf(jnp.concatenate([a_t[q], r_t[:, sls[q]]], axis=0)),
                jnp.concatenate([block_diag(bf(b_hat[q]), first), block_diag(bf(k_h[:, sls[q]]), first)],
                                axis=0)) for q in qs]
    nk = [bf(jnp.where(strict, p[q][:clen, :tw], 0.0)) for q in qs]
    a_ak = [bf(jnp.where(strict, p[q][:clen, tw:], 0.0)) for q in qs]
    a_r = [bf(jnp.where(lower2, p[q][clen:, :], 0.0)) for q in qs]

    sa = [_mm_nt(bf(a_t[q]), s_bf[q]) + _mm(a_ak[q], v_bd[q]) for q in qs]
    for lvl in range(levels):
        sa = [sa[q] + _mm(nk[q], block_diag(bf(sa[q]), first)) for q in qs]
        if lvl + 1 < levels:
            nk = [bf(_mm(nk[q], block_diag(nk[q], first_t))) for q in qs]

    sa_bf = [bf(sa[q]) for q in qs]
    ys = [_mm_nt(bf(r_t[:, sls[q]]), s_bf[q])
          + _mm(a_r[q], jnp.concatenate([block_diag(sa_bf[q], first), v_bd[q]], axis=0)) for q in qs]
    for q, sl in zip(qs, sls):
        upd = _mm_tn(jnp.concatenate([sa_bf[q], bf(v[:, sl])], axis=0),
                     jnp.concatenate([bf(b_hat[q] * g_last[:, sl]), bf(k_l[:, sl])], axis=0))
        s_ref[q] = s_prev[q] * g_last[:, sl] + jnp.where(same_head, upd, 0.0)

    outs = []
    inv_n = 1.0 / n
    for q, sl in zip(qs, sls):
        y = ys[q]
        mean = head_sums(y) * inv_n
        var = head_sums(jnp.square(y - mean)) * inv_n
        y = (y - mean) * lax.rsqrt(var + GN_EPS)
        y = y * lnw_ref[:, sl] + lnb_ref[:, sl]
        y = y + head_sums(rk[:, sl]) * v[:, sl]
        outs.append(y)

    o_ref[0] = jnp.concatenate(outs, axis=1) * g

    @pl.when(c == pl.num_programs(1) - 1)
    def _():
        for q in range(pairs):
            so_ref[0, 2 * q] = s_ref[q, 0:n, 0:n]
            so_ref[0, 2 * q + 1] = s_ref[q, n:pw, n:pw]


def _wkv(proj, shift_prev, state0, mu, w0, w_lora, a0, a_lora, g_lora, k_k, k_a, r_k, lnx_w, lnx_b, *, clen):
    b, t, full_w = proj.shape
    heads = state0.shape[1]
    n = RWKV_HEAD_DIM
    width = heads * n
    pw = shift_prev.shape[-1]
    col0 = full_w - pw
    assert t % clen == 0 and pw == 3 * width + W_LORA + A_LORA + G_LORA and heads % 2 == 0
    assert col0 % LANES == 0 and 2 * n == LANES
    row = lambda a: a.reshape(1, -1)
    full = lambda shape: pl.BlockSpec(shape, lambda bi, c: (0,) * len(shape))
    blocks = _nbytes((clen, full_w), F32) + _nbytes((clen, width), F32) + 2 * _nbytes((heads, n, LANES), F32)
    return pl.pallas_call(
        functools.partial(_wkv_body, clen=clen, heads=heads, col0=col0),
        grid=(b, t // clen),
        in_specs=[
            pl.BlockSpec((1, clen, full_w), lambda bi, c: (bi, c, 0)),
            pl.BlockSpec((1, 1, pw), lambda bi, c: (bi, 0, 0)),
            pl.BlockSpec((1, heads, n, n), lambda bi, c: (bi, 0, 0, 0)),
            full((1, pw)), full((1, width)), full((W_LORA, width)), full((1, width)), full((A_LORA, width)),
            full((G_LORA, width)), full((1, width)), full((1, width)), full((1, width)), full((1, width)),
            full((1, width)),
        ],
        out_specs=[
            pl.BlockSpec((1, clen, width), lambda bi, c: (bi, c, 0)),
            pl.BlockSpec((1, heads, n, n), lambda bi, c: (bi, 0, 0, 0)),
        ],
        out_shape=[jax.ShapeDtypeStruct((b, t, width), F32), jax.ShapeDtypeStruct((b, heads, n, n), F32)],
        scratch_shapes=[pltpu.VMEM((1, pw), F32), pltpu.VMEM((heads // 2, 2 * n, 2 * n), F32)],
        compiler_params=pltpu.CompilerParams(
            dimension_semantics=("parallel", "arbitrary"),
            vmem_limit_bytes=_vmem_limit(blocks, _nbytes((heads // 2, 2 * n, 2 * n), F32),
                                         40 * _nbytes((clen, width), F32))),
        name="wkv",
    )(proj, shift_prev, state0, row(mu), row(w0), w_lora, row(a0), a_lora, g_lora, row(k_k), row(k_a),
      row(r_k), row(lnx_w), row(lnx_b))


def _outproj_body(h_ref, att_ref, rw_ref, wa_ref, wr_ref, o_ref):
    o_ref[...] = (h_ref[...] + _mm(att_ref[...].astype(BF16), wa_ref[...])
                  + _mm(rw_ref[...].astype(BF16), wr_ref[...]))


def _outproj(h, att, rw, w_att, w_rw, *, tm):
    m, d = h.shape
    ka, kr = att.shape[1], rw.shape[1]
    assert m % tm == 0
    blocks = (2 * _nbytes((tm, d), F32) + _nbytes((tm, ka), F32) + _nbytes((tm, kr), F32)
              + _nbytes((ka, d), BF16) + _nbytes((kr, d), BF16))
    return pl.pallas_call(
        _outproj_body,
        grid=(m // tm,),
        in_specs=[
            pl.BlockSpec((tm, d), lambda i: (i, 0)),
            pl.BlockSpec((tm, ka), lambda i: (i, 0)),
            pl.BlockSpec((tm, kr), lambda i: (i, 0)),
            pl.BlockSpec((ka, d), lambda i: (0, 0)),
            pl.BlockSpec((kr, d), lambda i: (0, 0)),
        ],
        out_specs=pl.BlockSpec((tm, d), lambda i: (i, 0)),
        out_shape=jax.ShapeDtypeStruct((m, d), F32),
        compiler_params=pltpu.CompilerParams(
            dimension_semantics=("parallel",),
            vmem_limit_bytes=_vmem_limit(blocks, 0, 2 * _nbytes((tm, d), F32))),
        name="outproj",
    )(h, att, rw, w_att, w_rw)


def _ple_body(x_ref, g_ref, wg_ref, p_ref, wp_ref, o_ref):
    x = x_ref[...]
    gate = jax.nn.sigmoid(_mm(_rms_norm_rows(x, g_ref[...]).astype(BF16), wg_ref[...]))
    o_ref[...] = x + gate * _mm(p_ref[...].astype(BF16), wp_ref[...])


def _ple(x, gain, w_gate, p, w_proj, *, tm):
    m, d = x.shape
    pd = p.shape[1]
    assert m % tm == 0
    blocks = (2 * _nbytes((tm, d), F32) + _nbytes((d, d), BF16) + _nbytes((tm, pd), F32)
              + _nbytes((pd, d), BF16))
    return pl.pallas_call(
        _ple_body,
        grid=(m // tm,),
        in_specs=[
            pl.BlockSpec((tm, d), lambda i: (i, 0)),
            pl.BlockSpec((1, d), lambda i: (0, 0)),
            pl.BlockSpec((d, d), lambda i: (0, 0)),
            pl.BlockSpec((tm, pd), lambda i: (i, 0)),
            pl.BlockSpec((pd, d), lambda i: (0, 0)),
        ],
        out_specs=pl.BlockSpec((tm, d), lambda i: (i, 0)),
        out_shape=jax.ShapeDtypeStruct((m, d), F32),
        compiler_params=pltpu.CompilerParams(
            dimension_semantics=("parallel",),
            vmem_limit_bytes=_vmem_limit(blocks, 0, 4 * _nbytes((tm, d), F32))),
        name="ple",
    )(x, gain.reshape(1, d), w_gate, p, w_proj)


def _pick(m, candidates):
    for c in candidates:
        if m % c == 0:
            return c
    raise ValueError(f"no tile for {m}")


def _trunk_layer(x, p, lw, ctx_k, ctx_v, left_valid, shift_prev, wkv_prev):
    b, t, d = x.shape
    m = b * t
    att_w = lw["w_att_out"].shape[0]
    heads = att_w // ATT_HEAD_DIM
    tm = _pick(m, (512, 256, 128))
    x2 = x.reshape(m, d)

    h = _ffn(x2, lw["norm_ffn1"], lw["ffn1_gate"], lw["ffn1_up"], lw["ffn1_down"], tm=tm, tf=512)
    proj = _proj(h, lw["norm_mix"], lw["w_in"], tm=tm, tn=lw["w_in"].shape[1] // 5)
    proj3 = proj.reshape(b, t, -1)

    clen = _pick(t, (CHUNK, 16))
    tq = _pick(t, (4 * CHUNK, CHUNK, 16))
    group = 2 if t % (2 * tq) == 0 and LEFT % (2 * tq) == 0 else 1
    att, k_keep = _attention(proj3, ctx_k, ctx_v, lw["q_norm"], lw["k_norm"], _band_bias_vec(lw["rel_bias"], tq),
                             heads=heads, tq=tq, chunk=min(CHUNK, tq), left_valid=left_valid, group=group)
    rw, wkv_new = _wkv(proj3, shift_prev, wkv_prev, lw["rwkv_mu"], lw["rwkv_w0"], lw["rwkv_w_lora"],
                       lw["rwkv_a0"], lw["rwkv_a_lora"], lw["rwkv_g_lora"], lw["rwkv_k_k"], lw["rwkv_k_a"],
                       lw["rwkv_r_k"], lw["rwkv_lnx_w"], lw["rwkv_lnx_b"], clen=clen)

    h = _outproj(h, att.reshape(m, att_w), rw.reshape(m, -1), lw["w_att_out"], lw["w_rw_out"], tm=tm)
    h = _ffn(h, lw["norm_ffn2"], lw["ffn2_gate"], lw["ffn2_up"], lw["ffn2_down"], tm=tm, tf=512)
    h = _ple(h, lw["norm_ple"], lw["ple_gate"], p.reshape(m, -1), lw["ple_proj"], tm=tm)

    keep = k_keep.shape[1]
    k_keep = k_keep.reshape(b, keep, heads, ATT_HEAD_DIM)
    v_keep = proj3[:, t - keep:, 2 * att_w:3 * att_w].reshape(b, keep, heads, ATT_HEAD_DIM)
    return h.reshape(b, t, d), (k_keep, v_keep, wkv_new, proj3[:, -1:, 3 * att_w:])


def kernel(x_prompt, x_sample, cache_att_k, cache_att_v, state_wkv, state_shift, p_prompt, p_sample, norm_ffn1, ffn1_gate, ffn1_up, ffn1_down, norm_mix, w_in, q_norm, k_norm, rel_bias, rwkv_mu, rwkv_w0, rwkv_w_lora, rwkv_a0, rwkv_a_lora, rwkv_g_lora, rwkv_k_k, rwkv_k_a, rwkv_r_k, rwkv_lnx_w, rwkv_lnx_b, w_out, norm_ffn2, ffn2_gate, ffn2_up, ffn2_down, norm_ple, ple_gate, ple_proj):
    depth = norm_ffn1.shape[0]
    hp, hs = x_prompt, x_sample
    bp = hp.shape[0]
    bs = hs.shape[0]
    att_w = w_out.shape[1] // 2
    rwkv_heads = rwkv_r_k.shape[1]
    rwkv_pw = rwkv_mu.shape[1]
    prompt_states, sample_states = [], []
    for i in range(depth):
        lw = {
            "norm_ffn1": norm_ffn1[i], "ffn1_gate": ffn1_gate[i].astype(BF16), "ffn1_up": ffn1_up[i].astype(BF16),
            "ffn1_down": ffn1_down[i].astype(BF16), "norm_mix": norm_mix[i], "w_in": w_in[i].astype(BF16),
            "q_norm": q_norm[i], "k_norm": k_norm[i],
            "rel_bias": rel_bias[i], "rwkv_mu": rwkv_mu[i], "rwkv_w0": rwkv_w0[i],
            "rwkv_w_lora": rwkv_w_lora[i].astype(BF16), "rwkv_a0": rwkv_a0[i],
            "rwkv_a_lora": rwkv_a_lora[i].astype(BF16), "rwkv_g_lora": rwkv_g_lora[i].astype(BF16),
            "rwkv_k_k": rwkv_k_k[i], "rwkv_k_a": rwkv_k_a[i], "rwkv_r_k": rwkv_r_k[i].reshape(-1),
            "rwkv_lnx_w": rwkv_lnx_w[i], "rwkv_lnx_b": rwkv_lnx_b[i],
            "w_att_out": w_out[i][:att_w].astype(BF16), "w_rw_out": w_out[i][att_w:].astype(BF16),
            "norm_ffn2": norm_ffn2[i], "ffn2_gate": ffn2_gate[i].astype(BF16), "ffn2_up": ffn2_up[i].astype(BF16),
            "ffn2_down": ffn2_down[i].astype(BF16), "norm_ple": norm_ple[i],
            "ple_gate": ple_gate[i].astype(BF16), "ple_proj": ple_proj[i].astype(BF16),
        }
        zero_ctx = jnp.zeros((1, LEFT, att_w), F32)
        zero_shift = jnp.zeros((bp, 1, rwkv_pw), hp.dtype)
        zero_wkv = jnp.zeros((bp, rwkv_heads, RWKV_HEAD_DIM, RWKV_HEAD_DIM), hp.dtype)
        hp, sp = _trunk_layer(hp, p_prompt[i], lw, zero_ctx, zero_ctx, False, zero_shift, zero_wkv)
        ctx_k = cache_att_k[i].reshape(bs, -1, att_w)
        ctx_v = cache_att_v[i].reshape(bs, -1, att_w)
        hs, ss = _trunk_layer(hs, p_sample[i], lw, ctx_k, ctx_v, True, state_shift[i], state_wkv[i])
        prompt_states.append(sp)
        sample_states.append(ss)
    stack = lambda states, j: jnp.stack([s[j] for s in states])
    return (hp, hs, stack(prompt_states, 0), stack(prompt_states, 1), stack(prompt_states, 2),
            stack(prompt_states, 3), stack(sample_states, 0), stack(sample_states, 1),
            stack(sample_states, 2), stack(sample_states, 3))
```

```python
import functools
import math

import jax
import jax.numpy as jnp
from jax import lax
from jax.experimental import pallas as pl
from jax.experimental.pallas import tpu as pltpu

F32 = jnp.float32
BF16 = jnp.bfloat16

CHUNK = 64
LEFT = 512
ATT_HEAD_DIM = 128
REL_CLIP = 256
RWKV_HEAD_DIM = 64
W_LORA = 64
A_LORA = 64
G_LORA = 128
RMS_EPS = 1e-6
GN_EPS = 64e-5
NEG_INF = -1e30

LANES = 128
V7X_VMEM_BYTES = 64 * 1024 * 1024
VMEM_LIMIT_CAP = V7X_VMEM_BYTES - 8 * 1024 * 1024


def _vmem_limit(block_bytes, scratch_bytes, temp_bytes):
    need = 2 * block_bytes + scratch_bytes + temp_bytes + (4 << 20)
    return int(min(max(need, 16 << 20), VMEM_LIMIT_CAP))


def _nbytes(shape, dtype):
    return math.prod(shape) * jnp.dtype(dtype).itemsize


def _rms_norm_rows(x, gain):
    ms = jnp.mean(x * x, axis=-1, keepdims=True)
    return x * lax.rsqrt(ms + RMS_EPS) * gain


def _mm(a, b):
    return jnp.dot(a, b, preferred_element_type=F32)


def _mm_nt(a, b):
    return lax.dot_general(a, b, (((1,), (1,)), ((), ())), preferred_element_type=F32)


def _mm_tn(a, b):
    return lax.dot_general(a, b, (((0,), (0,)), ((), ())), preferred_element_type=F32)


def _ffn_body(x_ref, g_ref, wg_ref, wu_ref, wd_ref, o_ref, xn_ref):
    j = pl.program_id(1)
    last = pl.num_programs(1) - 1

    @pl.when(j == 0)
    def _():
        xn_ref[...] = _rms_norm_rows(x_ref[...], g_ref[...]).astype(BF16)
        o_ref[...] = jnp.zeros_like(o_ref)

    xn = xn_ref[...]
    a = _mm(xn, wg_ref[...])
    u = _mm(xn, wu_ref[...])
    h = (a * jax.nn.sigmoid(a) * u).astype(BF16)
    o_ref[...] += _mm(h, wd_ref[...])

    @pl.when(j == last)
    def _():
        o_ref[...] = x_ref[...] + 0.5 * o_ref[...]


def _ffn(x, gain, wg, wu, wd, *, tm, tf):
    m, d = x.shape
    f = wg.shape[1]
    assert m % tm == 0 and f % tf == 0
    blocks = (_nbytes((tm, d), F32) * 2 + _nbytes((d, tf), BF16) * 3)
    temps = _nbytes((tm, tf), F32) * 4
    return pl.pallas_call(
        _ffn_body,
        grid=(m // tm, f // tf),
        in_specs=[
            pl.BlockSpec((tm, d), lambda i, j: (i, 0)),
            pl.BlockSpec((1, d), lambda i, j: (0, 0)),
            pl.BlockSpec((d, tf), lambda i, j: (0, j)),
            pl.BlockSpec((d, tf), lambda i, j: (0, j)),
            pl.BlockSpec((tf, d), lambda i, j: (j, 0)),
        ],
        out_specs=pl.BlockSpec((tm, d), lambda i, j: (i, 0)),
        out_shape=jax.ShapeDtypeStruct((m, d), F32),
        scratch_shapes=[pltpu.VMEM((tm, d), BF16)],
        compiler_params=pltpu.CompilerParams(
            dimension_semantics=("parallel", "arbitrary"),
            vmem_limit_bytes=_vmem_limit(blocks, _nbytes((tm, d), BF16), temps)),
        name="ffn",
    )(x, gain.reshape(1, d), wg, wu, wd)


def _proj_body(x_ref, g_ref, w_ref, o_ref, xn_ref):
    @pl.when(pl.program_id(1) == 0)
    def _():
        xn_ref[...] = _rms_norm_rows(x_ref[...], g_ref[...]).astype(BF16)

    o_ref[...] = _mm(xn_ref[...], w_ref[...])


def _proj(x, gain, w, *, tm, tn):
    m, d = x.shape
    n = w.shape[1]
    assert m % tm == 0 and n % tn == 0
    blocks = _nbytes((tm, d), F32) + _nbytes((d, tn), BF16) + _nbytes((tm, tn), F32)
    return pl.pallas_call(
        _proj_body,
        grid=(m // tm, n // tn),
        in_specs=[
            pl.BlockSpec((tm, d), lambda i, j: (i, 0)),
            pl.BlockSpec((1, d), lambda i, j: (0, 0)),
            pl.BlockSpec((d, tn), lambda i, j: (0, j)),
        ],
        out_specs=pl.BlockSpec((tm, tn), lambda i, j: (i, j)),
        out_shape=jax.ShapeDtypeStruct((m, n), F32),
        scratch_shapes=[pltpu.VMEM((tm, d), BF16)],
        compiler_params=pltpu.CompilerParams(
            dimension_semantics=("parallel", "arbitrary"),
            vmem_limit_bytes=_vmem_limit(blocks, _nbytes((tm, d), BF16), 2 * _nbytes((tm, tn), F32))),
        name="proj",
    )(x, gain.reshape(1, d), w)


def _attn_body(q_ref, k_ref, v_ref, ck_ref, cv_ref, qg_ref, kg_ref, bvec_ref, o_ref, kn_ref,
               qn_ref, kp_ref, vp_ref, bias_ref, *, seq, tq, chunk, left_valid, group):
    keep = kn_ref.shape[1]
    qn_ref[...] = _rms_norm_rows(q_ref[0], qg_ref[...]).astype(BF16)
    kn = _rms_norm_rows(k_ref[0], kg_ref[...])
    kn_ref[0] = kn[seq - keep:, :]
    kp_ref[0:LEFT, :] = ck_ref[0].astype(BF16)
    vp_ref[0:LEFT, :] = cv_ref[0].astype(BF16)
    kp_ref[LEFT:LEFT + seq, :] = kn.astype(BF16)
    vp_ref[LEFT:LEFT + seq, :] = v_ref[0].astype(BF16)

    width = LEFT + tq
    bw = bvec_ref.shape[-1]
    bias = pltpu.roll(jnp.broadcast_to(bvec_ref[0], (tq, bw)), 0, 1, stride=1, stride_axis=0)[:, :width]
    if tq > chunk:
        row = lax.broadcasted_iota(jnp.int32, (tq, width), 0)
        col = lax.broadcasted_iota(jnp.int32, (tq, width), 1)
        chunk_start = (row // chunk) * chunk
        in_band = (col >= chunk_start) & (col < chunk_start + LEFT + chunk)
        bias = jnp.where(in_band, bias, NEG_INF)
    bias_ref[...] = bias
    scale = ATT_HEAD_DIM ** -0.5

    def blocks(t, mask_left):
        bases = [pl.multiple_of((t * group + u) * tq, tq) for u in range(group)]
        scores = [_mm_nt(qn_ref[pl.ds(base, tq), :], kp_ref[pl.ds(base, width), :]) for base in bases]
        probs = []
        for s, base in zip(scores, bases):
            band = bias_ref[...]
            ok = band > 0.5 * NEG_INF
            if mask_left:
                col = lax.broadcasted_iota(jnp.int32, (tq, width), 1)
                ok = ok & (col >= LEFT - base)
            s = jnp.where(ok, s * scale + band, NEG_INF)
            e = jnp.exp(s - jnp.max(s, axis=-1, keepdims=True))
            probs.append((e * (1.0 / jnp.sum(e, axis=-1, keepdims=True))).astype(BF16))
        for p, base in zip(probs, bases):
            o_ref[0, pl.ds(base, tq), :] = _mm(p, vp_ref[pl.ds(base, width), :])

    n_iter = seq // (tq * group)
    n_left = 0 if left_valid else min(n_iter, -(-LEFT // (tq * group)))

    def masked(t, carry):
        blocks(t, True)
        return carry

    def plain(t, carry):
        blocks(t, False)
        return carry

    lax.fori_loop(0, n_left, masked, 0)
    lax.fori_loop(n_left, n_iter, plain, 0)


def _attention(proj, ctx_k, ctx_v, q_gain, k_gain, bias_vec, *, heads, tq, chunk, left_valid, group):
    b, seq, _ = proj.shape
    dh = ATT_HEAD_DIM
    bw = bias_vec.shape[-1]
    keep = min(seq, LEFT)
    assert seq % (tq * group) == 0 and tq % chunk == 0 and ctx_k.shape[1] == LEFT
    per_batch_ctx = ctx_k.shape[0] == b
    ctx_map = (lambda bi, h: (bi, 0, h)) if per_batch_ctx else (lambda bi, h: (0, 0, h))
    blocks = (4 * _nbytes((seq, dh), F32) + 2 * _nbytes((LEFT, dh), F32) + _nbytes((8, bw), F32)
              + _nbytes((keep, dh), F32))
    scratch = (_nbytes((seq, dh), BF16) + 2 * _nbytes((LEFT + seq, dh), BF16)
               + _nbytes((tq, LEFT + tq), F32))
    return pl.pallas_call(
        functools.partial(_attn_body, seq=seq, tq=tq, chunk=chunk, left_valid=left_valid, group=group),
        grid=(b, heads),
        in_specs=[
            pl.BlockSpec((1, seq, dh), lambda bi, h: (bi, 0, h)),
            pl.BlockSpec((1, seq, dh), lambda bi, h: (bi, 0, heads + h)),
            pl.BlockSpec((1, seq, dh), lambda bi, h: (bi, 0, 2 * heads + h)),
            pl.BlockSpec((1, LEFT, dh), ctx_map),
            pl.BlockSpec((1, LEFT, dh), ctx_map),
            pl.BlockSpec((1, dh), lambda bi, h: (0, 0)),
            pl.BlockSpec((1, dh), lambda bi, h: (0, 0)),
            pl.BlockSpec((1, 1, bw), lambda bi, h: (h, 0, 0)),
        ],
        out_specs=[
            pl.BlockSpec((1, seq, dh), lambda bi, h: (bi, 0, h)),
            pl.BlockSpec((1, keep, dh), lambda bi, h: (bi, 0, h)),
        ],
        out_shape=[jax.ShapeDtypeStruct((b, seq, heads * dh), F32),
                   jax.ShapeDtypeStruct((b, keep, heads * dh), F32)],
        scratch_shapes=[pltpu.VMEM((seq, dh), BF16), pltpu.VMEM((LEFT + seq, dh), BF16),
                        pltpu.VMEM((LEFT + seq, dh), BF16), pltpu.VMEM((tq, LEFT + tq), F32)],
        compiler_params=pltpu.CompilerParams(
            dimension_semantics=("parallel", "parallel"),
            vmem_limit_bytes=_vmem_limit(blocks, scratch, 8 * group * _nbytes((tq, LEFT + tq), F32)
                                         + 3 * _nbytes((seq, dh), F32))),
        name="attn",
    )(proj, proj, proj, ctx_k, ctx_v, q_gain.reshape(1, dh), k_gain.reshape(1, dh), bias_vec)


def _band_bias_vec(rel_bias, tq):
    width = LEFT + tq
    bw = -(-(width + tq) // LANES) * LANES
    m = jnp.arange(bw)
    d = jnp.where(m < width, m, m - bw)
    idx = jnp.clip(LEFT - d, -REL_CLIP, REL_CLIP) + REL_CLIP
    return rel_bias[idx].T.astype(F32)[:, None, :]


def _split3(x):
    h1 = x.astype(BF16)
    r1 = x - h1.astype(F32)
    h2 = r1.astype(BF16)
    h3 = (r1 - h2.astype(F32)).astype(BF16)
    return h1, h2, h3


def _wkv_body(x_ref, sp_ref, s0_ref, mu_ref, w0_ref, wl_ref, a0_ref, al_ref, gl_ref, kk_ref, ka_ref,
              rk_ref, lnw_ref, lnb_ref, o_ref, so_ref, carry_ref, s_ref, *, clen, heads, col0, nseq):
    c = pl.program_id(1)
    n = RWKV_HEAD_DIM
    width = heads * n
    pairs = heads // 2
    pw = 2 * n
    tw = 2 * clen
    bf = lambda t: t.astype(BF16)

    @pl.when(c == 0)
    def _():
        zeros = jnp.zeros((n, n), F32)
        for i in range(nseq):
            carry_ref[i] = sp_ref[i]
            for q in range(pairs):
                s_ref[i * pairs + q] = jnp.concatenate(
                    [jnp.concatenate([s0_ref[i, 2 * q], zeros], axis=1),
                     jnp.concatenate([zeros, s0_ref[i, 2 * q + 1]], axis=1)], axis=0)

    trow = lax.broadcasted_iota(jnp.int32, (clen, 1), 0)
    ti = lax.broadcasted_iota(jnp.int32, (clen, clen), 0)
    si = lax.broadcasted_iota(jnp.int32, (clen, clen), 1)
    tri = (si <= ti).astype(BF16)
    levels = max(1, (clen - 1).bit_length())

    def prepare(i):
        xb = x_ref[i, :, col0:]
        shifted = jnp.where(trow == 0, carry_ref[i], pltpu.roll(xb, 1, 0))
        carry_ref[i] = xb[clen - 1:clen, :]
        xm = xb + mu_ref[...] * (shifted - xb)

        r = xm[:, 0:width]
        k = xm[:, width:2 * width]
        v = xm[:, 2 * width:3 * width]
        o3 = 3 * width
        w_lo = xm[:, o3:o3 + W_LORA]
        a_lo = xm[:, o3 + W_LORA:o3 + W_LORA + A_LORA]
        g_lo = xm[:, o3 + W_LORA + A_LORA:o3 + W_LORA + A_LORA + G_LORA]

        z = -(w0_ref[...] + _mm(bf(jnp.tanh(w_lo)), wl_ref[...]))
        softplus_z = jnp.maximum(z, 0.0) + jnp.log1p(jnp.exp(-jnp.abs(z)))
        log_decay = -jnp.exp(-softplus_z - 0.5)
        a = jax.nn.sigmoid(a0_ref[...] + _mm(bf(a_lo), al_ref[...]))
        g = _mm(bf(jax.nn.sigmoid(g_lo)), gl_ref[...])
        kk = k * kk_ref[...]
        k = k * (1.0 + (a - 1.0) * ka_ref[...])

        d1, d2, d3 = _split3(log_decay)
        cum = _mm(tri, d1) + _mm(tri, d2) + _mm(tri, d3)
        g_in = jnp.exp(cum)
        g_inv = jnp.exp(-cum)
        g_last = g_in[clen - 1:clen, :]
        k_h = k * g_inv
        return dict(v=v, a=a, g=g, kk=kk, rk=r * k * rk_ref[...], g_inv=g_inv, g_ex=jnp.exp(cum - log_decay),
                    g_last=g_last, r_t=r * g_in, k_h=k_h, k_l=k_h * g_last)

    pre = [prepare(i) for i in range(nseq)]

    first = lax.broadcasted_iota(jnp.int32, (1, pw), 1) < n
    first_t = lax.broadcasted_iota(jnp.int32, (1, tw), 1) < clen
    strict = (lax.broadcasted_iota(jnp.int32, (clen, tw), 1) % clen
              < lax.broadcasted_iota(jnp.int32, (clen, tw), 0))
    lower2 = (lax.broadcasted_iota(jnp.int32, (clen, 2 * tw), 1) % clen
              <= lax.broadcasted_iota(jnp.int32, (clen, 2 * tw), 0))
    same_head = ((lax.broadcasted_iota(jnp.int32, (pw, pw), 0) < n)
                 == (lax.broadcasted_iota(jnp.int32, (pw, pw), 1) < n))

    def head_sums(x):
        s0 = jnp.sum(jnp.where(first, x, 0.0), axis=-1, keepdims=True)
        s1 = jnp.sum(jnp.where(first, 0.0, x), axis=-1, keepdims=True)
        return jnp.where(first, s0, s1)

    def block_diag(x, mask):
        zero = jnp.zeros_like(x)
        return jnp.concatenate([jnp.where(mask, x, zero), jnp.where(mask, zero, x)], axis=0)

    units = [(i, q) for i in range(nseq) for q in range(pairs)]
    us = range(len(units))
    cols = [slice(q * pw, (q + 1) * pw) for _, q in units]
    val = lambda name, u: pre[units[u][0]][name][:, cols[u]]

    b_hat, a_t, v_bd, s_prev, s_bf = [], [], [], [], []
    for u in us:
        kk_u = val("kk", u)
        kk_u = kk_u * (1.0 / jnp.maximum(jnp.sqrt(head_sums(kk_u * kk_u)), 1e-12))
        b_hat.append(kk_u * val("a", u) * val("g_inv", u))
        a_t.append(-kk_u * val("g_ex", u))
        v_bd.append(block_diag(bf(val("v", u)), first))
        s_prev.append(s_ref[u])
        s_bf.append(bf(s_prev[u]))

    p = [_mm_nt(bf(jnp.concatenate([a_t[u], val("r_t", u)], axis=0)),
                jnp.concatenate([block_diag(bf(b_hat[u]), first), block_diag(bf(val("k_h", u)), first)],
                                axis=0)) for u in us]
    nk = [bf(jnp.where(strict, p[u][:clen, :tw], 0.0)) for u in us]
    a_ak = [bf(jnp.where(strict, p[u][:clen, tw:], 0.0)) for u in us]
    a_r = [bf(jnp.where(lower2, p[u][clen:, :], 0.0)) for u in us]

    sa = [_mm_nt(bf(a_t[u]), s_bf[u]) + _mm(a_ak[u], v_bd[u]) for u in us]
    for lvl in range(levels):
        sa = [sa[u] + _mm(nk[u], block_diag(bf(sa[u]), first)) for u in us]
        if lvl + 1 < levels:
            nk = [bf(_mm(nk[u], block_diag(nk[u], first_t))) for u in us]

    sa_bf = [bf(sa[u]) for u in us]
    ys = [_mm_nt(bf(val("r_t", u)), s_bf[u])
          + _mm(a_r[u], jnp.concatenate([block_diag(sa_bf[u], first), v_bd[u]], axis=0)) for u in us]
    for u in us:
        g_last = val("g_last", u)
        upd = _mm_tn(jnp.concatenate([sa_bf[u], bf(val("v", u))], axis=0),
                     jnp.concatenate([bf(b_hat[u] * g_last), bf(val("k_l", u))], axis=0))
        s_ref[u] = s_prev[u] * g_last + jnp.where(same_head, upd, 0.0)

    outs = []
    inv_n = 1.0 / n
    for u in us:
        y = ys[u]
        mean = head_sums(y) * inv_n
        var = head_sums(jnp.square(y - mean)) * inv_n
        y = (y - mean) * lax.rsqrt(var + GN_EPS)
        y = y * lnw_ref[:, cols[u]] + lnb_ref[:, cols[u]]
        y = y + head_sums(val("rk", u)) * val("v", u)
        outs.append(y)

    for i in range(nseq):
        o_ref[i] = jnp.concatenate(outs[i * pairs:(i + 1) * pairs], axis=1) * pre[i]["g"]

    @pl.when(c == pl.num_programs(1) - 1)
    def _():
        for i in range(nseq):
            for q in range(pairs):
                so_ref[i, 2 * q] = s_ref[i * pairs + q, 0:n, 0:n]
                so_ref[i, 2 * q + 1] = s_ref[i * pairs + q, n:pw, n:pw]


def _wkv(proj, shift_prev, state0, mu, w0, w_lora, a0, a_lora, g_lora, k_k, k_a, r_k, lnx_w, lnx_b, *, clen,
         nseq):
    b, t, full_w = proj.shape
    heads = state0.shape[1]
    n = RWKV_HEAD_DIM
    width = heads * n
    pw = shift_prev.shape[-1]
    col0 = full_w - pw
    assert t % clen == 0 and pw == 3 * width + W_LORA + A_LORA + G_LORA and heads % 2 == 0
    assert col0 % LANES == 0 and 2 * n == LANES and b % nseq == 0
    row = lambda a: a.reshape(1, -1)
    full = lambda shape: pl.BlockSpec(shape, lambda bi, c: (0,) * len(shape))
    blocks = nseq * (_nbytes((clen, full_w), F32) + _nbytes((clen, width), F32)
                     + 2 * _nbytes((heads, n, LANES), F32))
    state_scratch = _nbytes((nseq * heads // 2, 2 * n, 2 * n), F32)
    return pl.pallas_call(
        functools.partial(_wkv_body, clen=clen, heads=heads, col0=col0, nseq=nseq),
        grid=(b // nseq, t // clen),
        in_specs=[
            pl.BlockSpec((nseq, clen, full_w), lambda bi, c: (bi, c, 0)),
            pl.BlockSpec((nseq, 1, pw), lambda bi, c: (bi, 0, 0)),
            pl.BlockSpec((nseq, heads, n, n), lambda bi, c: (bi, 0, 0, 0)),
            full((1, pw)), full((1, width)), full((W_LORA, width)), full((1, width)), full((A_LORA, width)),
            full((G_LORA, width)), full((1, width)), full((1, width)), full((1, width)), full((1, width)),
            full((1, width)),
        ],
        out_specs=[
            pl.BlockSpec((nseq, clen, width), lambda bi, c: (bi, c, 0)),
            pl.BlockSpec((nseq, heads, n, n), lambda bi, c: (bi, 0, 0, 0)),
        ],
        out_shape=[jax.ShapeDtypeStruct((b, t, width), F32), jax.ShapeDtypeStruct((b, heads, n, n), F32)],
        scratch_shapes=[pltpu.VMEM((nseq, 1, pw), F32), pltpu.VMEM((nseq * heads // 2, 2 * n, 2 * n), F32)],
        compiler_params=pltpu.CompilerParams(
            dimension_semantics=("parallel", "arbitrary"),
            vmem_limit_bytes=_vmem_limit(blocks, state_scratch, 40 * nseq * _nbytes((clen, width), F32))),
        name="wkv",
    )(proj, shift_prev, state0, row(mu), row(w0), w_lora, row(a0), a_lora, g_lora, row(k_k), row(k_a),
      row(r_k), row(lnx_w), row(lnx_b))


def _outproj_body(h_ref, att_ref, rw_ref, wa_ref, wr_ref, o_ref):
    o_ref[...] = (h_ref[...] + _mm(att_ref[...].astype(BF16), wa_ref[...])
                  + _mm(rw_ref[...].astype(BF16), wr_ref[...]))


def _outproj(h, att, rw, w_att, w_rw, *, tm):
    m, d = h.shape
    ka, kr = att.shape[1], rw.shape[1]
    assert m % tm == 0
    blocks = (2 * _nbytes((tm, d), F32) + _nbytes((tm, ka), F32) + _nbytes((tm, kr), F32)
              + _nbytes((ka, d), BF16) + _nbytes((kr, d), BF16))
    return pl.pallas_call(
        _outproj_body,
        grid=(m // tm,),
        in_specs=[
            pl.BlockSpec((tm, d), lambda i: (i, 0)),
            pl.BlockSpec((tm, ka), lambda i: (i, 0)),
            pl.BlockSpec((tm, kr), lambda i: (i, 0)),
            pl.BlockSpec((ka, d), lambda i: (0, 0)),
            pl.BlockSpec((kr, d), lambda i: (0, 0)),
        ],
        out_specs=pl.BlockSpec((tm, d), lambda i: (i, 0)),
        out_shape=jax.ShapeDtypeStruct((m, d), F32),
        compiler_params=pltpu.CompilerParams(
            dimension_semantics=("parallel",),
            vmem_limit_bytes=_vmem_limit(blocks, 0, 2 * _nbytes((tm, d), F32))),
        name="outproj",
    )(h, att, rw, w_att, w_rw)


def _ple_body(x_ref, g_ref, wg_ref, p_ref, wp_ref, o_ref):
    x = x_ref[...]
    gate = jax.nn.sigmoid(_mm(_rms_norm_rows(x, g_ref[...]).astype(BF16), wg_ref[...]))
    o_ref[...] = x + gate * _mm(p_ref[...].astype(BF16), wp_ref[...])


def _ple(x, gain, w_gate, p, w_proj, *, tm):
    m, d = x.shape
    pd = p.shape[1]
    assert m % tm == 0
    blocks = (2 * _nbytes((tm, d), F32) + _nbytes((d, d), BF16) + _nbytes((tm, pd), F32)
              + _nbytes((pd, d), BF16))
    return pl.pallas_call(
        _ple_body,
        grid=(m // tm,),
        in_specs=[
            pl.BlockSpec((tm, d), lambda i: (i, 0)),
            pl.BlockSpec((1, d), lambda i: (0, 0)),
            pl.BlockSpec((d, d), lambda i: (0, 0)),
            pl.BlockSpec((tm, pd), lambda i: (i, 0)),
            pl.BlockSpec((pd, d), lambda i: (0, 0)),
        ],
        out_specs=pl.BlockSpec((tm, d), lambda i: (i, 0)),
        out_shape=jax.ShapeDtypeStruct((m, d), F32),
        compiler_params=pltpu.CompilerParams(
            dimension_semantics=("parallel",),
            vmem_limit_bytes=_vmem_limit(blocks, 0, 4 * _nbytes((tm, d), F32))),
        name="ple",
    )(x, gain.reshape(1, d), w_gate, p, w_proj)


def _pick(m, candidates):
    for c in candidates:
        if m % c == 0:
            return c
    raise ValueError(f"no tile for {m}")


def _trunk_layer(x, p, lw, ctx_k, ctx_v, left_valid, shift_prev, wkv_prev):
    b, t, d = x.shape
    m = b * t
    att_w = lw["w_att_out"].shape[0]
    heads = att_w // ATT_HEAD_DIM
    tm = _pick(m, (512, 256, 128))
    x2 = x.reshape(m, d)

    h = _ffn(x2, lw["norm_ffn1"], lw["ffn1_gate"], lw["ffn1_up"], lw["ffn1_down"], tm=tm, tf=512)
    proj = _proj(h, lw["norm_mix"], lw["w_in"], tm=_pick(m, (1024, 512, 256, 128)), tn=lw["w_in"].shape[1] // 5)
    proj3 = proj.reshape(b, t, -1)

    clen = _pick(t, (CHUNK, 16))
    tq = _pick(t, (4 * CHUNK, CHUNK, 16))
    group = 2 if t % (2 * tq) == 0 and LEFT % (2 * tq) == 0 else 1
    att, k_keep = _attention(proj3, ctx_k, ctx_v, lw["q_norm"], lw["k_norm"], _band_bias_vec(lw["rel_bias"], tq),
                             heads=heads, tq=tq, chunk=min(CHUNK, tq), left_valid=left_valid, group=group)
    rw, wkv_new = _wkv(proj3, shift_prev, wkv_prev, lw["rwkv_mu"], lw["rwkv_w0"], lw["rwkv_w_lora"],
                       lw["rwkv_a0"], lw["rwkv_a_lora"], lw["rwkv_g_lora"], lw["rwkv_k_k"], lw["rwkv_k_a"],
                       lw["rwkv_r_k"], lw["rwkv_lnx_w"], lw["rwkv_lnx_b"], clen=clen, nseq=_pick(b, (2, 1)))

    h = _outproj(h, att.reshape(m, att_w), rw.reshape(m, -1), lw["w_att_out"], lw["w_rw_out"], tm=tm)
    h = _ffn(h, lw["norm_ffn2"], lw["ffn2_gate"], lw["ffn2_up"], lw["ffn2_down"], tm=tm, tf=512)
    h = _ple(h, lw["norm_ple"], lw["ple_gate"], p.reshape(m, -1), lw["ple_proj"], tm=tm)

    keep = k_keep.shape[1]
    k_keep = k_keep.reshape(b, keep, heads, ATT_HEAD_DIM)
    v_keep = proj3[:, t - keep:, 2 * att_w:3 * att_w].reshape(b, keep, heads, ATT_HEAD_DIM)
    return h.reshape(b, t, d), (k_keep, v_keep, wkv_new, proj3[:, -1:, 3 * att_w:])


def kernel(x_prompt, x_sample, cache_att_k, cache_att_v, state_wkv, state_shift, p_prompt, p_sample, norm_ffn1, ffn1_gate, ffn1_up, ffn1_down, norm_mix, w_in, q_norm, k_norm, rel_bias, rwkv_mu, rwkv_w0, rwkv_w_lora, rwkv_a0, rwkv_a_lora, rwkv_g_lora, rwkv_k_k, rwkv_k_a, rwkv_r_k, rwkv_lnx_w, rwkv_lnx_b, w_out, norm_ffn2, ffn2_gate, ffn2_up, ffn2_down, norm_ple, ple_gate, ple_proj):
    depth = norm_ffn1.shape[0]
    hp, hs = x_prompt, x_sample
    bp = hp.shape[0]
    bs = hs.shape[0]
    att_w = w_out.shape[1] // 2
    rwkv_heads = rwkv_r_k.shape[1]
    rwkv_pw = rwkv_mu.shape[1]
    prompt_states, sample_states = [], []
    for i in range(depth):
        lw = {
            "norm_ffn1": norm_ffn1[i], "ffn1_gate": ffn1_gate[i].astype(BF16), "ffn1_up": ffn1_up[i].astype(BF16),
            "ffn1_down": ffn1_down[i].astype(BF16), "norm_mix": norm_mix[i], "w_in": w_in[i].astype(BF16),
            "q_norm": q_norm[i], "k_norm": k_norm[i],
            "rel_bias": rel_bias[i], "rwkv_mu": rwkv_mu[i], "rwkv_w0": rwkv_w0[i],
            "rwkv_w_lora": rwkv_w_lora[i].astype(BF16), "rwkv_a0": rwkv_a0[i],
            "rwkv_a_lora": rwkv_a_lora[i].astype(BF16), "rwkv_g_lora": rwkv_g_lora[i].astype(BF16),
            "rwkv_k_k": rwkv_k_k[i], "rwkv_k_a": rwkv_k_a[i], "rwkv_r_k": rwkv_r_k[i].reshape(-1),
            "rwkv_lnx_w": rwkv_lnx_w[i], "rwkv_lnx_b": rwkv_lnx_b[i],
            "w_att_out": w_out[i][:att_w].astype(BF16), "w_rw_out": w_out[i][att_w:].astype(BF16),
            "norm_ffn2": norm_ffn2[i], "ffn2_gate": ffn2_gate[i].astype(BF16), "ffn2_up": ffn2_up[i].astype(BF16),
            "ffn2_down": ffn2_down[i].astype(BF16), "norm_ple": norm_ple[i],
            "ple_gate": ple_gate[i].astype(BF16), "ple_proj": ple_proj[i].astype(BF16),
        }
        zero_ctx = jnp.zeros((1, LEFT, att_w), F32)
        zero_shift = jnp.zeros((bp, 1, rwkv_pw), hp.dtype)
        zero_wkv = jnp.zeros((bp, rwkv_heads, RWKV_HEAD_DIM, RWKV_HEAD_DIM), hp.dtype)
        hp, sp = _trunk_layer(hp, p_prompt[i], lw, zero_ctx, zero_ctx, False, zero_shift, zero_wkv)
        ctx_k = cache_att_k[i].reshape(bs, -1, att_w)
        ctx_v = cache_att_v[i].reshape(bs, -1, att_w)
        hs, ss = _trunk_layer(hs, p_sample[i], lw, ctx_k, ctx_v, True, state_shift[i], state_wkv[i])
        prompt_states.append(sp)
        sample_states.append(ss)
    stack = lambda states, j: jnp.stack([s[j] for s in states])
    return (hp, hs, stack(prompt_states, 0), stack(prompt_states, 1), stack(prompt_states, 2),
            stack(prompt_states, 3), stack(sample_states, 0), stack(sample_states, 1),
            stack(sample_states, 2), stack(sample_states, 3))
```

```python
import functools
import math

import jax
import jax.numpy as jnp
from jax import lax
from jax.experimental import pallas as pl
from jax.experimental.pallas import tpu as pltpu

F32 = jnp.float32
BF16 = jnp.bfloat16

CHUNK = 64
LEFT = 512
ATT_HEAD_DIM = 128
REL_CLIP = 256
RWKV_HEAD_DIM = 64
W_LORA = 64
A_LORA = 64
G_LORA = 128
RMS_EPS = 1e-6
GN_EPS = 64e-5
NEG_INF = -1e30

LANES = 128
SUBLANES = 8
V7X_VMEM_BYTES = 64 * 1024 * 1024
VMEM_LIMIT_CAP = V7X_VMEM_BYTES - 8 * 1024 * 1024


def _vmem_limit(block_bytes, scratch_bytes, temp_bytes):
    need = 2 * block_bytes + scratch_bytes + temp_bytes + (4 << 20)
    return int(min(max(need, 16 << 20), VMEM_LIMIT_CAP))


def _nbytes(shape, dtype):
    return math.prod(shape) * jnp.dtype(dtype).itemsize


def _rms_norm_rows(x, gain):
    ms = jnp.mean(x * x, axis=-1, keepdims=True)
    return x * lax.rsqrt(ms + RMS_EPS) * gain


def _mm(a, b):
    return jnp.dot(a, b, preferred_element_type=F32)


def _mm_nt(a, b):
    return lax.dot_general(a, b, (((1,), (1,)), ((), ())), preferred_element_type=F32)


def _mm_tn(a, b):
    return lax.dot_general(a, b, (((0,), (0,)), ((), ())), preferred_element_type=F32)


def _ffn_body(x_ref, g_ref, wg_ref, wu_ref, wd_ref, o_ref, xn_ref):
    j = pl.program_id(1)
    last = pl.num_programs(1) - 1

    @pl.when(j == 0)
    def _():
        xn_ref[...] = _rms_norm_rows(x_ref[...], g_ref[...]).astype(BF16)
        o_ref[...] = jnp.zeros_like(o_ref)

    xn = xn_ref[...]
    a = _mm(xn, wg_ref[...])
    u = _mm(xn, wu_ref[...])
    h = (a * jax.nn.sigmoid(a) * u).astype(BF16)
    o_ref[...] += _mm(h, wd_ref[...])

    @pl.when(j == last)
    def _():
        o_ref[...] = x_ref[...] + 0.5 * o_ref[...]


def _ffn(x, gain, wg, wu, wd, *, tm, tf):
    m, d = x.shape
    f = wg.shape[1]
    assert m % tm == 0 and f % tf == 0
    blocks = (_nbytes((tm, d), F32) * 2 + _nbytes((d, tf), BF16) * 3)
    temps = _nbytes((tm, tf), F32) * 4
    return pl.pallas_call(
        _ffn_body,
        grid=(m // tm, f // tf),
        in_specs=[
            pl.BlockSpec((tm, d), lambda i, j: (i, 0)),
            pl.BlockSpec((1, d), lambda i, j: (0, 0)),
            pl.BlockSpec((d, tf), lambda i, j: (0, j)),
            pl.BlockSpec((d, tf), lambda i, j: (0, j)),
            pl.BlockSpec((tf, d), lambda i, j: (j, 0)),
        ],
        out_specs=pl.BlockSpec((tm, d), lambda i, j: (i, 0)),
        out_shape=jax.ShapeDtypeStruct((m, d), F32),
        scratch_shapes=[pltpu.VMEM((tm, d), BF16)],
        compiler_params=pltpu.CompilerParams(
            dimension_semantics=("parallel", "arbitrary"),
            vmem_limit_bytes=_vmem_limit(blocks, _nbytes((tm, d), BF16), temps)),
        name="ffn",
    )(x, gain.reshape(1, d), wg, wu, wd)


def _proj_body(x_ref, g_ref, w_ref, o_ref, xn_ref):
    @pl.when(pl.program_id(1) == 0)
    def _():
        xn_ref[...] = _rms_norm_rows(x_ref[...], g_ref[...]).astype(BF16)

    o_ref[...] = _mm(xn_ref[...], w_ref[...])


def _proj(x, gain, w, *, tm, tn):
    m, d = x.shape
    n = w.shape[1]
    assert m % tm == 0 and n % tn == 0
    blocks = _nbytes((tm, d), F32) + _nbytes((d, tn), BF16) + _nbytes((tm, tn), F32)
    return pl.pallas_call(
        _proj_body,
        grid=(m // tm, n // tn),
        in_specs=[
            pl.BlockSpec((tm, d), lambda i, j: (i, 0)),
            pl.BlockSpec((1, d), lambda i, j: (0, 0)),
            pl.BlockSpec((d, tn), lambda i, j: (0, j)),
        ],
        out_specs=pl.BlockSpec((tm, tn), lambda i, j: (i, j)),
        out_shape=jax.ShapeDtypeStruct((m, n), F32),
        scratch_shapes=[pltpu.VMEM((tm, d), BF16)],
        compiler_params=pltpu.CompilerParams(
            dimension_semantics=("parallel", "arbitrary"),
            vmem_limit_bytes=_vmem_limit(blocks, _nbytes((tm, d), BF16), 2 * _nbytes((tm, tn), F32))),
        name="proj",
    )(x, gain.reshape(1, d), w)


def _attn_body(q_ref, k_ref, v_ref, ck_ref, cv_ref, qg_ref, kg_ref, bvec_ref, o_ref, kn_ref,
               qn_ref, kp_ref, vp_ref, bias_ref, *, seq, tq, chunk, left_valid, group):
    keep = kn_ref.shape[1]
    qn_ref[...] = _rms_norm_rows(q_ref[0], qg_ref[...]).astype(BF16)
    kn = _rms_norm_rows(k_ref[0], kg_ref[...])
    kn_ref[0] = kn[seq - keep:, :]
    kp_ref[0:LEFT, :] = ck_ref[0].astype(BF16)
    vp_ref[0:LEFT, :] = cv_ref[0].astype(BF16)
    kp_ref[LEFT:LEFT + seq, :] = kn.astype(BF16)
    vp_ref[LEFT:LEFT + seq, :] = v_ref[0].astype(BF16)

    width = LEFT + tq
    bw = bvec_ref.shape[-1]
    bias = pltpu.roll(jnp.broadcast_to(bvec_ref[0], (tq, bw)), 0, 1, stride=1, stride_axis=0)[:, :width]
    if tq > chunk:
        row = lax.broadcasted_iota(jnp.int32, (tq, width), 0)
        col = lax.broadcasted_iota(jnp.int32, (tq, width), 1)
        chunk_start = (row // chunk) * chunk
        in_band = (col >= chunk_start) & (col < chunk_start + LEFT + chunk)
        bias = jnp.where(in_band, bias, NEG_INF)
    bias_ref[...] = bias
    scale = ATT_HEAD_DIM ** -0.5

    n_chunks = tq // chunk
    spans = []
    for r in range(n_chunks):
        lo = (r * chunk) // LANES * LANES
        hi = min(width, -(-(r * chunk + LEFT + chunk) // LANES) * LANES)
        spans.append((lo, hi))

    def blocks(t, mask_left):
        bases = [pl.multiple_of((t * group + u) * tq, tq) for u in range(group)]
        scores = [_mm_nt(qn_ref[pl.ds(base, tq), :], kp_ref[pl.ds(base, width), :]) for base in bases]
        weights, inv_sums = [], []
        for s, base in zip(scores, bases):
            rows, inv = [], []
            for r, (lo, hi) in enumerate(spans):
                rs = slice(r * chunk, (r + 1) * chunk)
                band = bias_ref[rs, lo:hi]
                ok = band > 0.5 * NEG_INF
                if mask_left:
                    col = lax.broadcasted_iota(jnp.int32, (chunk, hi - lo), 1) + lo
                    ok = ok & (col >= LEFT - base)
                sr = jnp.where(ok, s[rs, lo:hi] * scale + band, NEG_INF)
                e = jnp.exp(sr - jnp.max(sr, axis=-1, keepdims=True))
                inv.append(1.0 / jnp.sum(e, axis=-1, keepdims=True))
                e = e.astype(BF16)
                pieces = ([jnp.zeros((chunk, lo), BF16)] if lo else []) + [e]
                pieces += [jnp.zeros((chunk, width - hi), BF16)] if hi < width else []
                rows.append(jnp.concatenate(pieces, axis=1) if len(pieces) > 1 else e)
            weights.append(jnp.concatenate(rows, axis=0) if n_chunks > 1 else rows[0])
            inv_sums.append(jnp.concatenate(inv, axis=0) if n_chunks > 1 else inv[0])
        for w, inv, base in zip(weights, inv_sums, bases):
            o_ref[0, pl.ds(base, tq), :] = _mm(w, vp_ref[pl.ds(base, width), :]) * inv

    n_iter = seq // (tq * group)
    n_left = 0 if left_valid else min(n_iter, -(-LEFT // (tq * group)))

    def masked(t, carry):
        blocks(t, True)
        return carry

    def plain(t, carry):
        blocks(t, False)
        return carry

    lax.fori_loop(0, n_left, masked, 0)
    lax.fori_loop(n_left, n_iter, plain, 0)


def _attention(proj, ctx_k, ctx_v, q_gain, k_gain, bias_vec, *, heads, tq, chunk, left_valid, group):
    b, seq, _ = proj.shape
    dh = ATT_HEAD_DIM
    bw = bias_vec.shape[-1]
    keep = min(seq, LEFT)
    assert seq % (tq * group) == 0 and tq % chunk == 0 and ctx_k.shape[1] == LEFT
    per_batch_ctx = ctx_k.shape[0] == b
    ctx_map = (lambda bi, h: (bi, 0, h)) if per_batch_ctx else (lambda bi, h: (0, 0, h))
    blocks = (4 * _nbytes((seq, dh), F32) + 2 * _nbytes((LEFT, dh), F32) + _nbytes((8, bw), F32)
              + _nbytes((keep, dh), F32))
    scratch = (_nbytes((seq, dh), BF16) + 2 * _nbytes((LEFT + seq, dh), BF16)
               + _nbytes((tq, LEFT + tq), F32))
    return pl.pallas_call(
        functools.partial(_attn_body, seq=seq, tq=tq, chunk=chunk, left_valid=left_valid, group=group),
        grid=(b, heads),
        in_specs=[
            pl.BlockSpec((1, seq, dh), lambda bi, h: (bi, 0, h)),
            pl.BlockSpec((1, seq, dh), lambda bi, h: (bi, 0, heads + h)),
            pl.BlockSpec((1, seq, dh), lambda bi, h: (bi, 0, 2 * heads + h)),
            pl.BlockSpec((1, LEFT, dh), ctx_map),
            pl.BlockSpec((1, LEFT, dh), ctx_map),
            pl.BlockSpec((1, dh), lambda bi, h: (0, 0)),
            pl.BlockSpec((1, dh), lambda bi, h: (0, 0)),
            pl.BlockSpec((1, 1, bw), lambda bi, h: (h, 0, 0)),
        ],
        out_specs=[
            pl.BlockSpec((1, seq, dh), lambda bi, h: (bi, 0, h)),
            pl.BlockSpec((1, keep, dh), lambda bi, h: (bi, 0, h)),
        ],
        out_shape=[jax.ShapeDtypeStruct((b, seq, heads * dh), F32),
                   jax.ShapeDtypeStruct((b, keep, heads * dh), F32)],
        scratch_shapes=[pltpu.VMEM((seq, dh), BF16), pltpu.VMEM((LEFT + seq, dh), BF16),
                        pltpu.VMEM((LEFT + seq, dh), BF16), pltpu.VMEM((tq, LEFT + tq), F32)],
        compiler_params=pltpu.CompilerParams(
            dimension_semantics=("parallel", "parallel"),
            vmem_limit_bytes=_vmem_limit(blocks, scratch, 8 * group * _nbytes((tq, LEFT + tq), F32)
                                         + 3 * _nbytes((seq, dh), F32))),
        name="attn",
    )(proj, proj, proj, ctx_k, ctx_v, q_gain.reshape(1, dh), k_gain.reshape(1, dh), bias_vec)


def _band_bias_vec(rel_bias, tq):
    width = LEFT + tq
    bw = -(-(width + tq) // LANES) * LANES
    m = jnp.arange(bw)
    d = jnp.where(m < width, m, m - bw)
    idx = jnp.clip(LEFT - d, -REL_CLIP, REL_CLIP) + REL_CLIP
    return rel_bias[idx].T.astype(F32)[:, None, :]


def _split3(x):
    h1 = x.astype(BF16)
    r1 = x - h1.astype(F32)
    h2 = r1.astype(BF16)
    h3 = (r1 - h2.astype(F32)).astype(BF16)
    return h1, h2, h3


def _wkv_body(x_ref, sp_ref, s0_ref, mu_ref, w0_ref, wl_ref, a0_ref, al_ref, gl_ref, kk_ref, ka_ref,
              rk_ref, lnw_ref, lnb_ref, o_ref, so_ref, carry_ref, s_ref, *, clen, heads, col0, nseq):
    c = pl.program_id(1)
    n = RWKV_HEAD_DIM
    width = heads * n
    pairs = heads // 2
    pw = 2 * n
    tw = 2 * clen
    bf = lambda t: t.astype(BF16)

    @pl.when(c == 0)
    def _():
        zeros = jnp.zeros((n, n), F32)
        for i in range(nseq):
            carry_ref[i] = sp_ref[i]
            for q in range(pairs):
                s_ref[i * pairs + q] = jnp.concatenate(
                    [jnp.concatenate([s0_ref[i, 2 * q], zeros], axis=1),
                     jnp.concatenate([zeros, s0_ref[i, 2 * q + 1]], axis=1)], axis=0)

    trow = lax.broadcasted_iota(jnp.int32, (SUBLANES, 1), 0)
    ti = lax.broadcasted_iota(jnp.int32, (clen, clen), 0)
    si = lax.broadcasted_iota(jnp.int32, (clen, clen), 1)
    tri = (si <= ti).astype(BF16)
    levels = max(1, (clen - 1).bit_length())

    def prepare(i):
        xb = x_ref[i, :, col0:]
        rolled = pltpu.roll(xb, 1, 0)
        shifted = jnp.concatenate([jnp.where(trow == 0, carry_ref[i], rolled[:SUBLANES]), rolled[SUBLANES:]],
                                  axis=0)
        carry_ref[i] = xb[clen - 1:clen, :]
        xm = xb + mu_ref[...] * (shifted - xb)

        r = xm[:, 0:width]
        k = xm[:, width:2 * width]
        v = xm[:, 2 * width:3 * width]
        o3 = 3 * width
        w_lo = xm[:, o3:o3 + W_LORA]
        a_lo = xm[:, o3 + W_LORA:o3 + W_LORA + A_LORA]
        g_lo = xm[:, o3 + W_LORA + A_LORA:o3 + W_LORA + A_LORA + G_LORA]

        z = -(w0_ref[...] + _mm(bf(jnp.tanh(w_lo)), wl_ref[...]))
        softplus_z = jnp.maximum(z, 0.0) + jnp.log(1.0 + jnp.exp(-jnp.abs(z)))
        log_decay = -jnp.exp(-softplus_z - 0.5)
        a = jax.nn.sigmoid(a0_ref[...] + _mm(bf(a_lo), al_ref[...]))
        g = _mm(bf(jax.nn.sigmoid(g_lo)), gl_ref[...])
        kk = k * kk_ref[...]
        k = k * (1.0 + (a - 1.0) * ka_ref[...])

        d1, d2, d3 = _split3(log_decay)
        cum = _mm(tri, d1) + _mm(tri, d2) + _mm(tri, d3)
        g_in = jnp.exp(cum)
        g_inv = jnp.exp(-cum)
        g_last = g_in[clen - 1:clen, :]
        k_h = k * g_inv
        return dict(v=v, a=a, g=g, kk=kk, rk=r * k * rk_ref[...], g_inv=g_inv, g_ex=jnp.exp(cum - log_decay),
                    g_last=g_last, r_t=r * g_in, k_h=k_h, k_l=k_h * g_last)

    pre = [prepare(i) for i in range(nseq)]

    first = lax.broadcasted_iota(jnp.int32, (1, pw), 1) < n
    first_t = lax.broadcasted_iota(jnp.int32, (1, tw), 1) < clen
    strict = (lax.broadcasted_iota(jnp.int32, (clen, tw), 1) % clen
              < lax.broadcasted_iota(jnp.int32, (clen, tw), 0))
    lower2 = (lax.broadcasted_iota(jnp.int32, (clen, 2 * tw), 1) % clen
              <= lax.broadcasted_iota(jnp.int32, (clen, 2 * tw), 0))
    same_head = ((lax.broadcasted_iota(jnp.int32, (pw, pw), 0) < n)
                 == (lax.broadcasted_iota(jnp.int32, (pw, pw), 1) < n))

    def head_sums(x):
        s0 = jnp.sum(jnp.where(first, x, 0.0), axis=-1, keepdims=True)
        s1 = jnp.sum(jnp.where(first, 0.0, x), axis=-1, keepdims=True)
        return jnp.where(first, s0, s1)

    def block_diag(x, mask):
        zero = jnp.zeros_like(x)
        return jnp.concatenate([jnp.where(mask, x, zero), jnp.where(mask, zero, x)], axis=0)

    units = [(i, q) for i in range(nseq) for q in range(pairs)]
    us = range(len(units))
    cols = [slice(q * pw, (q + 1) * pw) for _, q in units]
    val = lambda name, u: pre[units[u][0]][name][:, cols[u]]

    b_hat, a_t, v_bd, s_prev, s_bf = [], [], [], [], []
    for u in us:
        kk_u = val("kk", u)
        kk_u = kk_u * jnp.minimum(lax.rsqrt(head_sums(kk_u * kk_u)), 1e12)
        b_hat.append(kk_u * val("a", u) * val("g_inv", u))
        a_t.append(-kk_u * val("g_ex", u))
        v_bd.append(block_diag(bf(val("v", u)), first))
        s_prev.append(s_ref[u])
        s_bf.append(bf(s_prev[u]))

    p = [_mm_nt(bf(jnp.concatenate([a_t[u], val("r_t", u)], axis=0)),
                jnp.concatenate([block_diag(bf(b_hat[u]), first), block_diag(bf(val("k_h", u)), first)],
                                axis=0)) for u in us]
    nk = [bf(jnp.where(strict, p[u][:clen, :tw], 0.0)) for u in us]
    a_ak = [bf(jnp.where(strict, p[u][:clen, tw:], 0.0)) for u in us]
    a_r = [bf(jnp.where(lower2, p[u][clen:, :], 0.0)) for u in us]

    sa = [_mm_nt(bf(a_t[u]), s_bf[u]) + _mm(a_ak[u], v_bd[u]) for u in us]
    for lvl in range(levels):
        sa = [sa[u] + _mm(nk[u], block_diag(bf(sa[u]), first)) for u in us]
        if lvl + 1 < levels:
            nk = [bf(_mm(nk[u], block_diag(nk[u], first_t))) for u in us]

    sa_bf = [bf(sa[u]) for u in us]
    ys = [_mm_nt(bf(val("r_t", u)), s_bf[u])
          + _mm(a_r[u], jnp.concatenate([block_diag(sa_bf[u], first), v_bd[u]], axis=0)) for u in us]
    for u in us:
        g_last = val("g_last", u)
        upd = _mm_tn(jnp.concatenate([sa_bf[u], bf(val("v", u))], axis=0),
                     jnp.concatenate([bf(b_hat[u] * g_last), bf(val("k_l", u))], axis=0))
        s_ref[u] = s_prev[u] * g_last + jnp.where(same_head, upd, 0.0)

    outs = []
    inv_n = 1.0 / n
    for u in us:
        y = ys[u]
        mean = head_sums(y) * inv_n
        var = head_sums(jnp.square(y - mean)) * inv_n
        y = (y - mean) * lax.rsqrt(var + GN_EPS)
        y = y * lnw_ref[:, cols[u]] + lnb_ref[:, cols[u]]
        y = y + head_sums(val("rk", u)) * val("v", u)
        outs.append(y)

    for i in range(nseq):
        o_ref[i] = jnp.concatenate(outs[i * pairs:(i + 1) * pairs], axis=1) * pre[i]["g"]

    @pl.when(c == pl.num_programs(1) - 1)
    def _():
        for i in range(nseq):
            for q in range(pairs):
                so_ref[i, 2 * q] = s_ref[i * pairs + q, 0:n, 0:n]
                so_ref[i, 2 * q + 1] = s_ref[i * pairs + q, n:pw, n:pw]


def _wkv(proj, shift_prev, state0, mu, w0, w_lora, a0, a_lora, g_lora, k_k, k_a, r_k, lnx_w, lnx_b, *, clen,
         nseq):
    b, t, full_w = proj.shape
    heads = state0.shape[1]
    n = RWKV_HEAD_DIM
    width = heads * n
    pw = shift_prev.shape[-1]
    col0 = full_w - pw
    assert t % clen == 0 and pw == 3 * width + W_LORA + A_LORA + G_LORA and heads % 2 == 0
    assert col0 % LANES == 0 and 2 * n == LANES and b % nseq == 0
    row = lambda a: a.reshape(1, -1)
    full = lambda shape: pl.BlockSpec(shape, lambda bi, c: (0,) * len(shape))
    blocks = nseq * (_nbytes((clen, full_w), F32) + _nbytes((clen, width), F32)
                     + 2 * _nbytes((heads, n, LANES), F32))
    state_scratch = _nbytes((nseq * heads // 2, 2 * n, 2 * n), F32)
    return pl.pallas_call(
        functools.partial(_wkv_body, clen=clen, heads=heads, col0=col0, nseq=nseq),
        grid=(b // nseq, t // clen),
        in_specs=[
            pl.BlockSpec((nseq, clen, full_w), lambda bi, c: (bi, c, 0)),
            pl.BlockSpec((nseq, 1, pw), lambda bi, c: (bi, 0, 0)),
            pl.BlockSpec((nseq, heads, n, n), lambda bi, c: (bi, 0, 0, 0)),
            full((1, pw)), full((1, width)), full((W_LORA, width)), full((1, width)), full((A_LORA, width)),
            full((G_LORA, width)), full((1, width)), full((1, width)), full((1, width)), full((1, width)),
            full((1, width)),
        ],
        out_specs=[
            pl.BlockSpec((nseq, clen, width), lambda bi, c: (bi, c, 0)),
            pl.BlockSpec((nseq, heads, n, n), lambda bi, c: (bi, 0, 0, 0)),
        ],
        out_shape=[jax.ShapeDtypeStruct((b, t, width), F32), jax.ShapeDtypeStruct((b, heads, n, n), F32)],
        scratch_shapes=[pltpu.VMEM((nseq, 1, pw), F32), pltpu.VMEM((nseq * heads // 2, 2 * n, 2 * n), F32)],
        compiler_params=pltpu.CompilerParams(
            dimension_semantics=("parallel", "arbitrary"),
            vmem_limit_bytes=_vmem_limit(blocks, state_scratch, 40 * nseq * _nbytes((clen, width), F32))),
        name="wkv",
    )(proj, shift_prev, state0, row(mu), row(w0), w_lora, row(a0), a_lora, g_lora, row(k_k), row(k_a),
      row(r_k), row(lnx_w), row(lnx_b))


def _outproj_body(h_ref, att_ref, rw_ref, wa_ref, wr_ref, o_ref):
    o_ref[...] = (h_ref[...] + _mm(att_ref[...].astype(BF16), wa_ref[...])
                  + _mm(rw_ref[...].astype(BF16), wr_ref[...]))


def _outproj(h, att, rw, w_att, w_rw, *, tm):
    m, d = h.shape
    ka, kr = att.shape[1], rw.shape[1]
    assert m % tm == 0
    blocks = (2 * _nbytes((tm, d), F32) + _nbytes((tm, ka), F32) + _nbytes((tm, kr), F32)
              + _nbytes((ka, d), BF16) + _nbytes((kr, d), BF16))
    return pl.pallas_call(
        _outproj_body,
        grid=(m // tm,),
        in_specs=[
            pl.BlockSpec((tm, d), lambda i: (i, 0)),
            pl.BlockSpec((tm, ka), lambda i: (i, 0)),
            pl.BlockSpec((tm, kr), lambda i: (i, 0)),
            pl.BlockSpec((ka, d), lambda i: (0, 0)),
            pl.BlockSpec((kr, d), lambda i: (0, 0)),
        ],
        out_specs=pl.BlockSpec((tm, d), lambda i: (i, 0)),
        out_shape=jax.ShapeDtypeStruct((m, d), F32),
        compiler_params=pltpu.CompilerParams(
            dimension_semantics=("parallel",),
            vmem_limit_bytes=_vmem_limit(blocks, 0, 2 * _nbytes((tm, d), F32))),
        name="outproj",
    )(h, att, rw, w_att, w_rw)


def _ple_body(x_ref, g_ref, wg_ref, p_ref, wp_ref, o_ref):
    x = x_ref[...]
    gate = jax.nn.sigmoid(_mm(_rms_norm_rows(x, g_ref[...]).astype(BF16), wg_ref[...]))
    o_ref[...] = x + gate * _mm(p_ref[...].astype(BF16), wp_ref[...])


def _ple(x, gain, w_gate, p, w_proj, *, tm):
    m, d = x.shape
    pd = p.shape[1]
    assert m % tm == 0
    blocks = (2 * _nbytes((tm, d), F32) + _nbytes((d, d), BF16) + _nbytes((tm, pd), F32)
              + _nbytes((pd, d), BF16))
    return pl.pallas_call(
        _ple_body,
        grid=(m // tm,),
        in_specs=[
            pl.BlockSpec((tm, d), lambda i: (i, 0)),
            pl.BlockSpec((1, d), lambda i: (0, 0)),
            pl.BlockSpec((d, d), lambda i: (0, 0)),
            pl.BlockSpec((tm, pd), lambda i: (i, 0)),
            pl.BlockSpec((pd, d), lambda i: (0, 0)),
        ],
        out_specs=pl.BlockSpec((tm, d), lambda i: (i, 0)),
        out_shape=jax.ShapeDtypeStruct((m, d), F32),
        compiler_params=pltpu.CompilerParams(
            dimension_semantics=("parallel",),
            vmem_limit_bytes=_vmem_limit(blocks, 0, 4 * _nbytes((tm, d), F32))),
        name="ple",
    )(x, gain.reshape(1, d), w_gate, p, w_proj)


def _pick(m, candidates):
    for c in candidates:
        if m % c == 0:
            return c
    raise ValueError(f"no tile for {m}")


def _trunk_layer(x, p, lw, ctx_k, ctx_v, left_valid, shift_prev, wkv_prev):
    b, t, d = x.shape
    m = b * t
    att_w = lw["w_att_out"].shape[0]
    heads = att_w // ATT_HEAD_DIM
    tm = _pick(m, (512, 256, 128))
    x2 = x.reshape(m, d)

    h = _ffn(x2, lw["norm_ffn1"], lw["ffn1_gate"], lw["ffn1_up"], lw["ffn1_down"], tm=tm, tf=512)
    proj = _proj(h, lw["norm_mix"], lw["w_in"], tm=_pick(m, (1024, 512, 256, 128)), tn=lw["w_in"].shape[1] // 5)
    proj3 = proj.reshape(b, t, -1)

    clen = _pick(t, (CHUNK, 16))
    tq = _pick(t, (2 * CHUNK, CHUNK, 16))
    group = _pick(t // tq, (4, 2, 1))
    att, k_keep = _attention(proj3, ctx_k, ctx_v, lw["q_norm"], lw["k_norm"], _band_bias_vec(lw["rel_bias"], tq),
                             heads=heads, tq=tq, chunk=min(CHUNK, tq), left_valid=left_valid, group=group)
    rw, wkv_new = _wkv(proj3, shift_prev, wkv_prev, lw["rwkv_mu"], lw["rwkv_w0"], lw["rwkv_w_lora"],
                       lw["rwkv_a0"], lw["rwkv_a_lora"], lw["rwkv_g_lora"], lw["rwkv_k_k"], lw["rwkv_k_a"],
                       lw["rwkv_r_k"], lw["rwkv_lnx_w"], lw["rwkv_lnx_b"], clen=clen, nseq=_pick(b, (2, 1)))

    h = _outproj(h, att.reshape(m, att_w), rw.reshape(m, -1), lw["w_att_out"], lw["w_rw_out"], tm=tm)
    h = _ffn(h, lw["norm_ffn2"], lw["ffn2_gate"], lw["ffn2_up"], lw["ffn2_down"], tm=tm, tf=512)
    h = _ple(h, lw["norm_ple"], lw["ple_gate"], p.reshape(m, -1), lw["ple_proj"], tm=tm)

    keep = k_keep.shape[1]
    k_keep = k_keep.reshape(b, keep, heads, ATT_HEAD_DIM)
    v_keep = proj3[:, t - keep:, 2 * att_w:3 * att_w].reshape(b, keep, heads, ATT_HEAD_DIM)
    return h.reshape(b, t, d), (k_keep, v_keep, wkv_new, proj3[:, -1:, 3 * att_w:])


def kernel(x_prompt, x_sample, cache_att_k, cache_att_v, state_wkv, state_shift, p_prompt, p_sample, norm_ffn1, ffn1_gate, ffn1_up, ffn1_down, norm_mix, w_in, q_norm, k_norm, rel_bias, rwkv_mu, rwkv_w0, rwkv_w_lora, rwkv_a0, rwkv_a_lora, rwkv_g_lora, rwkv_k_k, rwkv_k_a, rwkv_r_k, rwkv_lnx_w, rwkv_lnx_b, w_out, norm_ffn2, ffn2_gate, ffn2_up, ffn2_down, norm_ple, ple_gate, ple_proj):
    depth = norm_ffn1.shape[0]
    hp, hs = x_prompt, x_sample
    bp = hp.shape[0]
    bs = hs.shape[0]
    att_w = w_out.shape[1] // 2
    rwkv_heads = rwkv_r_k.shape[1]
    rwkv_pw = rwkv_mu.shape[1]
    prompt_states, sample_states = [], []
    for i in range(depth):
        lw = {
            "norm_ffn1": norm_ffn1[i], "ffn1_gate": ffn1_gate[i].astype(BF16), "ffn1_up": ffn1_up[i].astype(BF16),
            "ffn1_down": ffn1_down[i].astype(BF16), "norm_mix": norm_mix[i], "w_in": w_in[i].astype(BF16),
            "q_norm": q_norm[i], "k_norm": k_norm[i],
            "rel_bias": rel_bias[i], "rwkv_mu": rwkv_mu[i], "rwkv_w0": rwkv_w0[i],
            "rwkv_w_lora": rwkv_w_lora[i].astype(BF16), "rwkv_a0": rwkv_a0[i],
            "rwkv_a_lora": rwkv_a_lora[i].astype(BF16), "rwkv_g_lora": rwkv_g_lora[i].astype(BF16),
            "rwkv_k_k": rwkv_k_k[i], "rwkv_k_a": rwkv_k_a[i], "rwkv_r_k": rwkv_r_k[i].reshape(-1),
            "rwkv_lnx_w": rwkv_lnx_w[i], "rwkv_lnx_b": rwkv_lnx_b[i],
            "w_att_out": w_out[i][:att_w].astype(BF16), "w_rw_out": w_out[i][att_w:].astype(BF16),
            "norm_ffn2": norm_ffn2[i], "ffn2_gate": ffn2_gate[i].astype(BF16), "ffn2_up": ffn2_up[i].astype(BF16),
            "ffn2_down": ffn2_down[i].astype(BF16), "norm_ple": norm_ple[i],
            "ple_gate": ple_gate[i].astype(BF16), "ple_proj": ple_proj[i].astype(BF16),
        }
        zero_ctx = jnp.zeros((1, LEFT, att_w), F32)
        zero_shift = jnp.zeros((bp, 1, rwkv_pw), hp.dtype)
        zero_wkv = jnp.zeros((bp, rwkv_heads, RWKV_HEAD_DIM, RWKV_HEAD_DIM), hp.dtype)
        hp, sp = _trunk_layer(hp, p_prompt[i], lw, zero_ctx, zero_ctx, False, zero_shift, zero_wkv)
        ctx_k = cache_att_k[i].reshape(bs, -1, att_w)
        ctx_v = cache_att_v[i].reshape(bs, -1, att_w)
        hs, ss = _trunk_layer(hs, p_sample[i], lw, ctx_k, ctx_v, True, state_shift[i], state_wkv[i])
        prompt_states.append(sp)
        sample_states.append(ss)
    stack = lambda states, j: jnp.stack([s[j] for s in states])
    return (hp, hs, stack(prompt_states, 0), stack(prompt_states, 1), stack(prompt_states, 2),
            stack(prompt_states, 3), stack(sample_states, 0), stack(sample_states, 1),
            stack(sample_states, 2), stack(sample_states, 3))
```

```python
import functools
import math

import jax
import jax.numpy as jnp
from jax import lax
from jax.experimental import pallas as pl
from jax.experimental.pallas import tpu as pltpu

F32 = jnp.float32
BF16 = jnp.bfloat16

CHUNK = 64
LEFT = 512
ATT_HEAD_DIM = 128
REL_CLIP = 256
RWKV_HEAD_DIM = 64
W_LORA = 64
A_LORA = 64
G_LORA = 128
RMS_EPS = 1e-6
GN_EPS = 64e-5
NEG_INF = -1e30

LANES = 128
SUBLANES = 8
V7X_VMEM_BYTES = 64 * 1024 * 1024
VMEM_LIMIT_CAP = V7X_VMEM_BYTES - 8 * 1024 * 1024


def _vmem_limit(block_bytes, scratch_bytes, temp_bytes):
    need = 2 * block_bytes + scratch_bytes + temp_bytes + (4 << 20)
    return int(min(max(need, 16 << 20), VMEM_LIMIT_CAP))


def _nbytes(shape, dtype):
    return math.prod(shape) * jnp.dtype(dtype).itemsize


def _rms_norm_rows(x, gain):
    ms = jnp.mean(x * x, axis=-1, keepdims=True)
    return x * lax.rsqrt(ms + RMS_EPS) * gain


def _mm(a, b):
    return jnp.dot(a, b, preferred_element_type=F32)


def _mm_nt(a, b):
    return lax.dot_general(a, b, (((1,), (1,)), ((), ())), preferred_element_type=F32)


def _mm_tn(a, b):
    return lax.dot_general(a, b, (((0,), (0,)), ((), ())), preferred_element_type=F32)


def _cast_plan(array, n_i, n_j):
    r, c = array.shape
    bf16_rows = 2 * SUBLANES
    if r % (n_i * n_j) == 0 and r // (n_i * n_j) % bf16_rows == 0:
        return (r // (n_i * n_j), c), (lambda i, j: (i * n_j + j, 0)), False
    if r % n_i == 0 and r // n_i % bf16_rows == 0 and c % n_j == 0 and c // n_j % LANES == 0:
        return (r // n_i, c // n_j), (lambda i, j: (i, j)), False
    if r % n_i == 0 and r // n_i % bf16_rows == 0:
        return (r // n_i, c), (lambda i, j: (i, 0)), True
    return None


def _ffn_body(*refs, cast_once):
    n_cast = len(cast_once)
    x_ref, g_ref, wg_ref, wu_ref, wd_ref = refs[:5]
    cast_in = refs[5:5 + n_cast]
    o_ref = refs[5 + n_cast]
    cast_out = refs[6 + n_cast:6 + 2 * n_cast]
    xn_ref = refs[6 + 2 * n_cast]
    j = pl.program_id(1)
    last = pl.num_programs(1) - 1

    @pl.when(j == 0)
    def _():
        xn_ref[...] = _rms_norm_rows(x_ref[...], g_ref[...]).astype(BF16)
        o_ref[...] = jnp.zeros_like(o_ref)
        for src, dst, once in zip(cast_in, cast_out, cast_once):
            if once:
                dst[...] = src[...].astype(BF16)

    for src, dst, once in zip(cast_in, cast_out, cast_once):
        if not once:
            dst[...] = src[...].astype(BF16)

    xn = xn_ref[...]
    a = _mm(xn, wg_ref[...])
    u = _mm(xn, wu_ref[...])
    h = (a * jax.nn.sigmoid(a) * u).astype(BF16)
    o_ref[...] += _mm(h, wd_ref[...])

    @pl.when(j == last)
    def _():
        o_ref[...] = x_ref[...] + 0.5 * o_ref[...]


def _ffn(x, gain, wg, wu, wd, *, tm, tf, casts=()):
    m, d = x.shape
    f = wg.shape[1]
    assert m % tm == 0 and f % tf == 0
    n_i, n_j = m // tm, f // tf
    plans = [_cast_plan(a, n_i, n_j) for a in casts]
    assert all(p is not None for p in plans)
    cast_bytes = sum(_nbytes(p[0], F32) + _nbytes(p[0], BF16) for p in plans)
    blocks = (_nbytes((tm, d), F32) * 2 + _nbytes((d, tf), BF16) * 3) + cast_bytes
    temps = _nbytes((tm, tf), F32) * 4
    outs = pl.pallas_call(
        functools.partial(_ffn_body, cast_once=tuple(p[2] for p in plans)),
        grid=(n_i, n_j),
        in_specs=[
            pl.BlockSpec((tm, d), lambda i, j: (i, 0)),
            pl.BlockSpec((1, d), lambda i, j: (0, 0)),
            pl.BlockSpec((d, tf), lambda i, j: (0, j)),
            pl.BlockSpec((d, tf), lambda i, j: (0, j)),
            pl.BlockSpec((tf, d), lambda i, j: (j, 0)),
        ] + [pl.BlockSpec(p[0], p[1]) for p in plans],
        out_specs=[pl.BlockSpec((tm, d), lambda i, j: (i, 0))] + [pl.BlockSpec(p[0], p[1]) for p in plans],
        out_shape=[jax.ShapeDtypeStruct((m, d), F32)] + [jax.ShapeDtypeStruct(a.shape, BF16) for a in casts],
        scratch_shapes=[pltpu.VMEM((tm, d), BF16)],
        compiler_params=pltpu.CompilerParams(
            dimension_semantics=("parallel", "arbitrary"),
            vmem_limit_bytes=_vmem_limit(blocks, _nbytes((tm, d), BF16), temps)),
        name="ffn",
    )(x, gain.reshape(1, d), wg, wu, wd, *casts)
    return outs[0], list(outs[1:])


def _proj_body(x_ref, g_ref, w_ref, o_ref, xn_ref):
    @pl.when(pl.program_id(1) == 0)
    def _():
        xn_ref[...] = _rms_norm_rows(x_ref[...], g_ref[...]).astype(BF16)

    o_ref[...] = _mm(xn_ref[...], w_ref[...])


def _proj(x, gain, w, *, tm, tn):
    m, d = x.shape
    n = w.shape[1]
    assert m % tm == 0 and n % tn == 0
    blocks = _nbytes((tm, d), F32) + _nbytes((d, tn), BF16) + _nbytes((tm, tn), F32)
    return pl.pallas_call(
        _proj_body,
        grid=(m // tm, n // tn),
        in_specs=[
            pl.BlockSpec((tm, d), lambda i, j: (i, 0)),
            pl.BlockSpec((1, d), lambda i, j: (0, 0)),
            pl.BlockSpec((d, tn), lambda i, j: (0, j)),
        ],
        out_specs=pl.BlockSpec((tm, tn), lambda i, j: (i, j)),
        out_shape=jax.ShapeDtypeStruct((m, n), F32),
        scratch_shapes=[pltpu.VMEM((tm, d), BF16)],
        compiler_params=pltpu.CompilerParams(
            dimension_semantics=("parallel", "arbitrary"),
            vmem_limit_bytes=_vmem_limit(blocks, _nbytes((tm, d), BF16), 2 * _nbytes((tm, tn), F32))),
        name="proj",
    )(x, gain.reshape(1, d), w)


def _attn_body(q_ref, k_ref, v_ref, ck_ref, cv_ref, qg_ref, kg_ref, bvec_ref, o_ref, kn_ref,
               qn_ref, kp_ref, vp_ref, bias_ref, *, seq, tq, chunk, left_valid, group):
    keep = kn_ref.shape[1]
    qn_ref[...] = _rms_norm_rows(q_ref[0], qg_ref[...]).astype(BF16)
    kn = _rms_norm_rows(k_ref[0], kg_ref[...])
    kn_ref[0] = kn[seq - keep:, :]
    kp_ref[0:LEFT, :] = ck_ref[0].astype(BF16)
    vp_ref[0:LEFT, :] = cv_ref[0].astype(BF16)
    kp_ref[LEFT:LEFT + seq, :] = kn.astype(BF16)
    vp_ref[LEFT:LEFT + seq, :] = v_ref[0].astype(BF16)

    width = LEFT + tq
    bw = bvec_ref.shape[-1]
    bias = pltpu.roll(jnp.broadcast_to(bvec_ref[0], (tq, bw)), 0, 1, stride=1, stride_axis=0)[:, :width]
    if tq > chunk:
        row = lax.broadcasted_iota(jnp.int32, (tq, width), 0)
        col = lax.broadcasted_iota(jnp.int32, (tq, width), 1)
        chunk_start = (row // chunk) * chunk
        in_band = (col >= chunk_start) & (col < chunk_start + LEFT + chunk)
        bias = jnp.where(in_band, bias, NEG_INF)
    bias_ref[...] = bias
    scale = ATT_HEAD_DIM ** -0.5

    n_chunks = tq // chunk
    spans = []
    for r in range(n_chunks):
        lo = (r * chunk) // LANES * LANES
        hi = min(width, -(-(r * chunk + LEFT + chunk) // LANES) * LANES)
        spans.append((lo, hi))

    def block_bases(t):
        return [pl.multiple_of((t * group + u) * tq, tq) for u in range(group)]

    def blocks(t, mask_left):
        bases = block_bases(t)
        scores = [_mm_nt(qn_ref[pl.ds(base, tq), :], kp_ref[pl.ds(base, width), :]) for base in bases]
        pieces = [(u, r) for u in range(group) for r in range(n_chunks)]
        masked_scores = []
        for u, r in pieces:
            lo, hi = spans[r]
            rs = slice(r * chunk, (r + 1) * chunk)
            band = bias_ref[rs, lo:hi]
            ok = band > 0.5 * NEG_INF
            if mask_left:
                col = lax.broadcasted_iota(jnp.int32, (chunk, hi - lo), 1) + lo
                ok = ok & (col >= LEFT - bases[u])
            masked_scores.append(jnp.where(ok, scores[u][rs, lo:hi] * scale + band, NEG_INF))
        maxes = [jnp.max(s, axis=-1, keepdims=True) for s in masked_scores]
        exps = [jnp.exp(s - m) for s, m in zip(masked_scores, maxes)]
        invs = [1.0 / jnp.sum(e, axis=-1, keepdims=True) for e in exps]
        rows = []
        for (u, r), e in zip(pieces, exps):
            lo, hi = spans[r]
            parts = ([jnp.zeros((chunk, lo), BF16)] if lo else []) + [e.astype(BF16)]
            parts += [jnp.zeros((chunk, width - hi), BF16)] if hi < width else []
            rows.append(jnp.concatenate(parts, axis=1) if len(parts) > 1 else parts[0])
        for u, base in enumerate(bases):
            mine = slice(u * n_chunks, (u + 1) * n_chunks)
            w = jnp.concatenate(rows[mine], axis=0) if n_chunks > 1 else rows[mine][0]
            inv = jnp.concatenate(invs[mine], axis=0) if n_chunks > 1 else invs[mine][0]
            o_ref[0, pl.ds(base, tq), :] = _mm(w, vp_ref[pl.ds(base, width), :]) * inv

    n_iter = seq // (tq * group)
    n_left = 0 if left_valid else min(n_iter, -(-LEFT // (tq * group)))

    def masked(t, carry):
        blocks(t, True)
        return carry

    def plain(t, carry):
        blocks(t, False)
        return carry

    lax.fori_loop(0, n_left, masked, 0)
    lax.fori_loop(n_left, n_iter, plain, 0)


def _attention(proj, ctx_k, ctx_v, q_gain, k_gain, bias_vec, *, heads, tq, chunk, left_valid, group):
    b, seq, _ = proj.shape
    dh = ATT_HEAD_DIM
    bw = bias_vec.shape[-1]
    keep = min(seq, LEFT)
    assert seq % (tq * group) == 0 and tq % chunk == 0 and ctx_k.shape[1] == LEFT
    per_batch_ctx = ctx_k.shape[0] == b
    ctx_map = (lambda bi, h: (bi, 0, h)) if per_batch_ctx else (lambda bi, h: (0, 0, h))
    blocks = (4 * _nbytes((seq, dh), F32) + 2 * _nbytes((LEFT, dh), F32) + _nbytes((8, bw), F32)
              + _nbytes((keep, dh), F32))
    scratch = (_nbytes((seq, dh), BF16) + 2 * _nbytes((LEFT + seq, dh), BF16)
               + _nbytes((tq, LEFT + tq), F32))
    return pl.pallas_call(
        functools.partial(_attn_body, seq=seq, tq=tq, chunk=chunk, left_valid=left_valid, group=group),
        grid=(b, heads),
        in_specs=[
            pl.BlockSpec((1, seq, dh), lambda bi, h: (bi, 0, h)),
            pl.BlockSpec((1, seq, dh), lambda bi, h: (bi, 0, heads + h)),
            pl.BlockSpec((1, seq, dh), lambda bi, h: (bi, 0, 2 * heads + h)),
            pl.BlockSpec((1, LEFT, dh), ctx_map),
            pl.BlockSpec((1, LEFT, dh), ctx_map),
            pl.BlockSpec((1, dh), lambda bi, h: (0, 0)),
            pl.BlockSpec((1, dh), lambda bi, h: (0, 0)),
            pl.BlockSpec((1, 1, bw), lambda bi, h: (h, 0, 0)),
        ],
        out_specs=[
            pl.BlockSpec((1, seq, dh), lambda bi, h: (bi, 0, h)),
            pl.BlockSpec((1, keep, dh), lambda bi, h: (bi, 0, h)),
        ],
        out_shape=[jax.ShapeDtypeStruct((b, seq, heads * dh), F32),
                   jax.ShapeDtypeStruct((b, keep, heads * dh), F32)],
        scratch_shapes=[pltpu.VMEM((seq, dh), BF16), pltpu.VMEM((LEFT + seq, dh), BF16),
                        pltpu.VMEM((LEFT + seq, dh), BF16), pltpu.VMEM((tq, LEFT + tq), F32)],
        compiler_params=pltpu.CompilerParams(
            dimension_semantics=("parallel", "parallel"),
            vmem_limit_bytes=_vmem_limit(blocks, scratch, 8 * group * _nbytes((tq, LEFT + tq), F32)
                                         + 3 * _nbytes((seq, dh), F32))),
        name="attn",
    )(proj, proj, proj, ctx_k, ctx_v, q_gain.reshape(1, dh), k_gain.reshape(1, dh), bias_vec)


def _band_bias_vec(rel_bias, tq):
    width = LEFT + tq
    bw = -(-(width + tq) // LANES) * LANES
    m = jnp.arange(bw)
    d = jnp.where(m < width, m, m - bw)
    idx = jnp.clip(LEFT - d, -REL_CLIP, REL_CLIP) + REL_CLIP
    return rel_bias[idx].T.astype(F32)[:, None, :]


def _split3(x):
    h1 = x.astype(BF16)
    r1 = x - h1.astype(F32)
    h2 = r1.astype(BF16)
    h3 = (r1 - h2.astype(F32)).astype(BF16)
    return h1, h2, h3


def _wkv_body(x_ref, sp_ref, s0_ref, mu_ref, w0_ref, wl_ref, a0_ref, al_ref, gl_ref, kk_ref, ka_ref,
              rk_ref, lnw_ref, lnb_ref, o_ref, so_ref, carry_ref, s_ref, *, clen, heads, col0, nseq):
    c = pl.program_id(1)
    n = RWKV_HEAD_DIM
    width = heads * n
    pairs = heads // 2
    pw = 2 * n
    tw = 2 * clen
    bf = lambda t: t.astype(BF16)

    @pl.when(c == 0)
    def _():
        zeros = jnp.zeros((n, n), F32)
        for i in range(nseq):
            carry_ref[i] = sp_ref[i]
            for q in range(pairs):
                s_ref[i * pairs + q] = jnp.concatenate(
                    [jnp.concatenate([s0_ref[i, 2 * q], zeros], axis=1),
                     jnp.concatenate([zeros, s0_ref[i, 2 * q + 1]], axis=1)], axis=0)

    trow = lax.broadcasted_iota(jnp.int32, (SUBLANES, 1), 0)
    ti = lax.broadcasted_iota(jnp.int32, (clen, clen), 0)
    si = lax.broadcasted_iota(jnp.int32, (clen, clen), 1)
    tri = (si <= ti).astype(BF16)
    levels = max(1, (clen - 1).bit_length())

    def prepare(i):
        xb = x_ref[i, :, col0:]
        rolled = pltpu.roll(xb, 1, 0)
        shifted = jnp.concatenate([jnp.where(trow == 0, carry_ref[i], rolled[:SUBLANES]), rolled[SUBLANES:]],
                                  axis=0)
        carry_ref[i] = xb[clen - 1:clen, :]
        xm = xb + mu_ref[...] * (shifted - xb)

        r = xm[:, 0:width]
        k = xm[:, width:2 * width]
        v = xm[:, 2 * width:3 * width]
        o3 = 3 * width
        w_lo = xm[:, o3:o3 + W_LORA]
        a_lo = xm[:, o3 + W_LORA:o3 + W_LORA + A_LORA]
        g_lo = xm[:, o3 + W_LORA + A_LORA:o3 + W_LORA + A_LORA + G_LORA]

        log_decay = -math.exp(-0.5) * jax.nn.sigmoid(w0_ref[...] + _mm(bf(jnp.tanh(w_lo)), wl_ref[...]))
        a = jax.nn.sigmoid(a0_ref[...] + _mm(bf(a_lo), al_ref[...]))
        g = _mm(bf(jax.nn.sigmoid(g_lo)), gl_ref[...])
        kk = k * kk_ref[...]
        k = k * (1.0 + (a - 1.0) * ka_ref[...])

        d1, d2, d3 = _split3(log_decay)
        cum = _mm(tri, d1) + _mm(tri, d2) + _mm(tri, d3)
        g_in = jnp.exp(cum)
        g_inv = jnp.exp(-cum)
        g_last = g_in[clen - 1:clen, :]
        k_h = k * g_inv
        return dict(v=v, a=a, g=g, kk=kk, rk=r * k * rk_ref[...], g_inv=g_inv, g_ex=jnp.exp(cum - log_decay),
                    g_last=g_last, r_t=r * g_in, k_h=k_h, k_l=k_h * g_last)

    pre = [prepare(i) for i in range(nseq)]

    first = lax.broadcasted_iota(jnp.int32, (1, pw), 1) < n
    first_t = lax.broadcasted_iota(jnp.int32, (1, tw), 1) < clen
    row2 = lax.broadcasted_iota(jnp.int32, (2 * clen, 2 * tw), 0)
    spos = lax.broadcasted_iota(jnp.int32, (2 * clen, 2 * tw), 1) % clen
    causal = spos < jnp.where(row2 < clen, row2, row2 - clen + 1)
    same_head = ((lax.broadcasted_iota(jnp.int32, (pw, pw), 0) < n)
                 == (lax.broadcasted_iota(jnp.int32, (pw, pw), 1) < n))

    def head_sums(x):
        s0 = jnp.sum(jnp.where(first, x, 0.0), axis=-1, keepdims=True)
        s1 = jnp.sum(jnp.where(first, 0.0, x), axis=-1, keepdims=True)
        return jnp.where(first, s0, s1)

    def block_diag(x, mask):
        zero = jnp.zeros_like(x)
        return jnp.concatenate([jnp.where(mask, x, zero), jnp.where(mask, zero, x)], axis=0)

    units = [(i, q) for i in range(nseq) for q in range(pairs)]
    us = range(len(units))
    cols = [slice(q * pw, (q + 1) * pw) for _, q in units]
    val = lambda name, u: pre[units[u][0]][name][:, cols[u]]

    b_hat, a_t, v_bd, s_prev, s_bf = [], [], [], [], []
    for u in us:
        kk_u = val("kk", u)
        kk_u = kk_u * jnp.minimum(lax.rsqrt(head_sums(kk_u * kk_u)), 1e12)
        b_hat.append(kk_u * val("a", u) * val("g_inv", u))
        a_t.append(-kk_u * val("g_ex", u))
        v_bd.append(block_diag(bf(val("v", u)), first))
        s_prev.append(s_ref[u])
        s_bf.append(bf(s_prev[u]))

    ar = [bf(jnp.concatenate([a_t[u], val("r_t", u)], axis=0)) for u in us]
    p = [_mm_nt(ar[u], jnp.concatenate([block_diag(bf(b_hat[u]), first),
                                        block_diag(bf(val("k_h", u)), first)], axis=0)) for u in us]
    pm = [bf(jnp.where(causal, p[u], 0.0)) for u in us]
    nk = [pm[u][:clen, :tw] for u in us]
    a_rb = [pm[u][clen:, :tw] for u in us]

    sv = [_mm_nt(ar[u], s_bf[u]) + _mm(pm[u][:, tw:], v_bd[u]) for u in us]
    sa = [sv[u][:clen] for u in us]
    for lvl in range(levels):
        if lvl + 1 < levels:
            both = [_mm(nk[u], jnp.concatenate([block_diag(bf(sa[u]), first), block_diag(nk[u], first_t)],
                                               axis=1)) for u in us]
            sa = [sa[u] + both[u][:, :pw] for u in us]
            nk = [bf(both[u][:, pw:]) for u in us]
        else:
            sa = [sa[u] + _mm(nk[u], block_diag(bf(sa[u]), first)) for u in us]

    sa_bf = [bf(sa[u]) for u in us]
    ys = [sv[u][clen:] + _mm(a_rb[u], block_diag(sa_bf[u], first)) for u in us]
    for u in us:
        g_last = val("g_last", u)
        upd = _mm_tn(jnp.concatenate([sa_bf[u], bf(val("v", u))], axis=0),
                     jnp.concatenate([bf(b_hat[u] * g_last), bf(val("k_l", u))], axis=0))
        s_ref[u] = s_prev[u] * g_last + jnp.where(same_head, upd, 0.0)

    outs = []
    inv_n = 1.0 / n
    for u in us:
        y = ys[u]
        mean = head_sums(y) * inv_n
        var = head_sums(jnp.square(y - mean)) * inv_n
        y = (y - mean) * lax.rsqrt(var + GN_EPS)
        y = y * lnw_ref[:, cols[u]] + lnb_ref[:, cols[u]]
        y = y + head_sums(val("rk", u)) * val("v", u)
        outs.append(y)

    for i in range(nseq):
        o_ref[i] = jnp.concatenate(outs[i * pairs:(i + 1) * pairs], axis=1) * pre[i]["g"]

    @pl.when(c == pl.num_programs(1) - 1)
    def _():
        for i in range(nseq):
            for q in range(pairs):
                so_ref[i, 2 * q] = s_ref[i * pairs + q, 0:n, 0:n]
                so_ref[i, 2 * q + 1] = s_ref[i * pairs + q, n:pw, n:pw]


def _wkv(proj, shift_prev, state0, mu, w0, w_lora, a0, a_lora, g_lora, k_k, k_a, r_k, lnx_w, lnx_b, *, clen,
         nseq):
    b, t, full_w = proj.shape
    heads = state0.shape[1]
    n = RWKV_HEAD_DIM
    width = heads * n
    pw = shift_prev.shape[-1]
    col0 = full_w - pw
    assert t % clen == 0 and pw == 3 * width + W_LORA + A_LORA + G_LORA and heads % 2 == 0
    assert col0 % LANES == 0 and 2 * n == LANES and b % nseq == 0
    row = lambda a: a.reshape(1, -1)
    full = lambda shape: pl.BlockSpec(shape, lambda bi, c: (0,) * len(shape))
    blocks = nseq * (_nbytes((clen, full_w), F32) + _nbytes((clen, width), F32)
                     + 2 * _nbytes((heads, n, LANES), F32))
    state_scratch = _nbytes((nseq * heads // 2, 2 * n, 2 * n), F32)
    return pl.pallas_call(
        functools.partial(_wkv_body, clen=clen, heads=heads, col0=col0, nseq=nseq),
        grid=(b // nseq, t // clen),
        in_specs=[
            pl.BlockSpec((nseq, clen, full_w), lambda bi, c: (bi, c, 0)),
            pl.BlockSpec((nseq, 1, pw), lambda bi, c: (bi, 0, 0)),
            pl.BlockSpec((nseq, heads, n, n), lambda bi, c: (bi, 0, 0, 0)),
            full((1, pw)), full((1, width)), full((W_LORA, width)), full((1, width)), full((A_LORA, width)),
            full((G_LORA, width)), full((1, width)), full((1, width)), full((1, width)), full((1, width)),
            full((1, width)),
        ],
        out_specs=[
            pl.BlockSpec((nseq, clen, width), lambda bi, c: (bi, c, 0)),
            pl.BlockSpec((nseq, heads, n, n), lambda bi, c: (bi, 0, 0, 0)),
        ],
        out_shape=[jax.ShapeDtypeStruct((b, t, width), F32), jax.ShapeDtypeStruct((b, heads, n, n), F32)],
        scratch_shapes=[pltpu.VMEM((nseq, 1, pw), F32), pltpu.VMEM((nseq * heads // 2, 2 * n, 2 * n), F32)],
        compiler_params=pltpu.CompilerParams(
            dimension_semantics=("parallel", "arbitrary"),
            vmem_limit_bytes=_vmem_limit(blocks, state_scratch, 40 * nseq * _nbytes((clen, width), F32))),
        name="wkv",
    )(proj, shift_prev, state0, row(mu), row(w0), w_lora, row(a0), a_lora, g_lora, row(k_k), row(k_a),
      row(r_k), row(lnx_w), row(lnx_b))


def _outproj_body(h_ref, att_ref, rw_ref, wa_ref, wr_ref, o_ref):
    o_ref[...] = (h_ref[...] + _mm(att_ref[...].astype(BF16), wa_ref[...])
                  + _mm(rw_ref[...].astype(BF16), wr_ref[...]))


def _outproj(h, att, rw, w_out, *, tm):
    m, d = h.shape
    ka, kr = att.shape[1], rw.shape[1]
    assert m % tm == 0 and ka == kr and w_out.shape == (ka + kr, d)
    blocks = (2 * _nbytes((tm, d), F32) + _nbytes((tm, ka), F32) + _nbytes((tm, kr), F32)
              + _nbytes((ka, d), BF16) + _nbytes((kr, d), BF16))
    return pl.pallas_call(
        _outproj_body,
        grid=(m // tm,),
        in_specs=[
            pl.BlockSpec((tm, d), lambda i: (i, 0)),
            pl.BlockSpec((tm, ka), lambda i: (i, 0)),
            pl.BlockSpec((tm, kr), lambda i: (i, 0)),
            pl.BlockSpec((ka, d), lambda i: (0, 0)),
            pl.BlockSpec((kr, d), lambda i: (1, 0)),
        ],
        out_specs=pl.BlockSpec((tm, d), lambda i: (i, 0)),
        out_shape=jax.ShapeDtypeStruct((m, d), F32),
        compiler_params=pltpu.CompilerParams(
            dimension_semantics=("parallel",),
            vmem_limit_bytes=_vmem_limit(blocks, 0, 2 * _nbytes((tm, d), F32))),
        name="outproj",
    )(h, att, rw, w_out, w_out)


def _ple_body(x_ref, g_ref, wg_ref, p_ref, wp_ref, o_ref):
    x = x_ref[...]
    gate = jax.nn.sigmoid(_mm(_rms_norm_rows(x, g_ref[...]).astype(BF16), wg_ref[...]))
    o_ref[...] = x + gate * _mm(p_ref[...].astype(BF16), wp_ref[...])


def _ple(x, gain, w_gate, p, w_proj, *, tm):
    m, d = x.shape
    pd = p.shape[1]
    assert m % tm == 0
    blocks = (2 * _nbytes((tm, d), F32) + _nbytes((d, d), BF16) + _nbytes((tm, pd), F32)
              + _nbytes((pd, d), BF16))
    return pl.pallas_call(
        _ple_body,
        grid=(m // tm,),
        in_specs=[
            pl.BlockSpec((tm, d), lambda i: (i, 0)),
            pl.BlockSpec((1, d), lambda i: (0, 0)),
            pl.BlockSpec((d, d), lambda i: (0, 0)),
            pl.BlockSpec((tm, pd), lambda i: (i, 0)),
            pl.BlockSpec((pd, d), lambda i: (0, 0)),
        ],
        out_specs=pl.BlockSpec((tm, d), lambda i: (i, 0)),
        out_shape=jax.ShapeDtypeStruct((m, d), F32),
        compiler_params=pltpu.CompilerParams(
            dimension_semantics=("parallel",),
            vmem_limit_bytes=_vmem_limit(blocks, 0, 4 * _nbytes((tm, d), F32))),
        name="ple",
    )(x, gain.reshape(1, d), w_gate, p, w_proj)


def _pick(m, candidates):
    for c in candidates:
        if m % c == 0:
            return c
    raise ValueError(f"no tile for {m}")


LATE_WEIGHTS = ("w_in", "w_out", "ffn2_gate", "ffn2_up", "ffn2_down", "ple_gate")


def _trunk_layer(x, p, lw, ctx_k, ctx_v, left_valid, shift_prev, wkv_prev):
    b, t, d = x.shape
    m = b * t
    tm = _pick(m, (512, 256, 128))
    tf = 512
    x2 = x.reshape(m, d)

    late = [name for name in LATE_WEIGHTS if lw[name].dtype != BF16]
    if not all(_cast_plan(lw[name], m // tm, lw["ffn1_gate"].shape[1] // tf) for name in late):
        lw = {**lw, **{name: lw[name].astype(BF16) for name in late}}
        late = []
    h, copies = _ffn(x2, lw["norm_ffn1"], lw["ffn1_gate"], lw["ffn1_up"], lw["ffn1_down"], tm=tm, tf=tf,
                     casts=[lw[name] for name in late])
    lw = {**lw, **dict(zip(late, copies))}
    att_w = lw["w_out"].shape[0] // 2
    heads = att_w // ATT_HEAD_DIM
    proj = _proj(h, lw["norm_mix"], lw["w_in"], tm=_pick(m, (1024, 512, 256, 128)), tn=lw["w_in"].shape[1] // 5)
    proj3 = proj.reshape(b, t, -1)

    clen = _pick(t, (CHUNK, 16))
    tq = _pick(t, (2 * CHUNK, CHUNK, 16))
    group = _pick(t // tq, (4, 2, 1))
    att, k_keep = _attention(proj3, ctx_k, ctx_v, lw["q_norm"], lw["k_norm"], _band_bias_vec(lw["rel_bias"], tq),
                             heads=heads, tq=tq, chunk=min(CHUNK, tq), left_valid=left_valid, group=group)
    rw, wkv_new = _wkv(proj3, shift_prev, wkv_prev, lw["rwkv_mu"], lw["rwkv_w0"], lw["rwkv_w_lora"],
                       lw["rwkv_a0"], lw["rwkv_a_lora"], lw["rwkv_g_lora"], lw["rwkv_k_k"], lw["rwkv_k_a"],
                       lw["rwkv_r_k"], lw["rwkv_lnx_w"], lw["rwkv_lnx_b"], clen=clen, nseq=_pick(b, (2, 1)))

    h = _outproj(h, att.reshape(m, att_w), rw.reshape(m, -1), lw["w_out"], tm=tm)
    h, _ = _ffn(h, lw["norm_ffn2"], lw["ffn2_gate"], lw["ffn2_up"], lw["ffn2_down"], tm=tm, tf=tf)
    h = _ple(h, lw["norm_ple"], lw["ple_gate"], p.reshape(m, -1), lw["ple_proj"], tm=tm)

    keep = k_keep.shape[1]
    k_keep = k_keep.reshape(b, keep, heads, ATT_HEAD_DIM)
    v_keep = proj3[:, t - keep:, 2 * att_w:3 * att_w].reshape(b, keep, heads, ATT_HEAD_DIM)
    return h.reshape(b, t, d), (k_keep, v_keep, wkv_new, proj3[:, -1:, 3 * att_w:]), lw


def kernel(x_prompt, x_sample, cache_att_k, cache_att_v, state_wkv, state_shift, p_prompt, p_sample, norm_ffn1, ffn1_gate, ffn1_up, ffn1_down, norm_mix, w_in, q_norm, k_norm, rel_bias, rwkv_mu, rwkv_w0, rwkv_w_lora, rwkv_a0, rwkv_a_lora, rwkv_g_lora, rwkv_k_k, rwkv_k_a, rwkv_r_k, rwkv_lnx_w, rwkv_lnx_b, w_out, norm_ffn2, ffn2_gate, ffn2_up, ffn2_down, norm_ple, ple_gate, ple_proj):
    depth = norm_ffn1.shape[0]
    hp, hs = x_prompt, x_sample
    bp = hp.shape[0]
    bs = hs.shape[0]
    att_w = w_out.shape[1] // 2
    rwkv_heads = rwkv_r_k.shape[1]
    rwkv_pw = rwkv_mu.shape[1]
    prompt_states, sample_states = [], []
    for i in range(depth):
        lw = {
            "norm_ffn1": norm_ffn1[i], "ffn1_gate": ffn1_gate[i].astype(BF16), "ffn1_up": ffn1_up[i].astype(BF16),
            "ffn1_down": ffn1_down[i].astype(BF16), "norm_mix": norm_mix[i], "w_in": w_in[i],
            "q_norm": q_norm[i], "k_norm": k_norm[i],
            "rel_bias": rel_bias[i], "rwkv_mu": rwkv_mu[i], "rwkv_w0": rwkv_w0[i],
            "rwkv_w_lora": rwkv_w_lora[i].astype(BF16), "rwkv_a0": rwkv_a0[i],
            "rwkv_a_lora": rwkv_a_lora[i].astype(BF16), "rwkv_g_lora": rwkv_g_lora[i].astype(BF16),
            "rwkv_k_k": rwkv_k_k[i], "rwkv_k_a": rwkv_k_a[i], "rwkv_r_k": rwkv_r_k[i].reshape(-1),
            "rwkv_lnx_w": rwkv_lnx_w[i], "rwkv_lnx_b": rwkv_lnx_b[i],
            "w_out": w_out[i], "norm_ffn2": norm_ffn2[i], "ffn2_gate": ffn2_gate[i], "ffn2_up": ffn2_up[i],
            "ffn2_down": ffn2_down[i], "norm_ple": norm_ple[i], "ple_gate": ple_gate[i],
            "ple_proj": ple_proj[i].astype(BF16),
        }
        zero_ctx = jnp.zeros((1, LEFT, att_w), F32)
        zero_shift = jnp.zeros((bp, 1, rwkv_pw), hp.dtype)
        zero_wkv = jnp.zeros((bp, rwkv_heads, RWKV_HEAD_DIM, RWKV_HEAD_DIM), hp.dtype)
        hp, sp, lw = _trunk_layer(hp, p_prompt[i], lw, zero_ctx, zero_ctx, False, zero_shift, zero_wkv)
        ctx_k = cache_att_k[i].reshape(bs, -1, att_w)
        ctx_v = cache_att_v[i].reshape(bs, -1, att_w)
        hs, ss, _ = _trunk_layer(hs, p_sample[i], lw, ctx_k, ctx_v, True, state_shift[i], state_wkv[i])
        prompt_states.append(sp)
        sample_states.append(ss)
    stack = lambda states, j: jnp.stack([s[j] for s in states])
    return (hp, hs, stack(prompt_states, 0), stack(prompt_states, 1), stack(prompt_states, 2),
            stack(prompt_states, 3), stack(sample_states, 0), stack(sample_states, 1),
            stack(sample_states, 2), stack(sample_states, 3))
```

```python
import functools
import math

import jax
import jax.numpy as jnp
from jax import lax
from jax.experimental import pallas as pl
from jax.experimental.pallas import tpu as pltpu

F32 = jnp.float32
BF16 = jnp.bfloat16

CHUNK = 64
LEFT = 512
ATT_HEAD_DIM = 128
REL_CLIP = 256
RWKV_HEAD_DIM = 64
W_LORA = 64
A_LORA = 64
G_LORA = 128
RMS_EPS = 1e-6
GN_EPS = 64e-5
NEG_INF = -1e30

LANES = 128
SUBLANES = 8
V7X_VMEM_BYTES = 64 * 1024 * 1024
VMEM_LIMIT_CAP = V7X_VMEM_BYTES - 8 * 1024 * 1024


def _vmem_limit(block_bytes, scratch_bytes, temp_bytes):
    need = 2 * block_bytes + scratch_bytes + temp_bytes + (4 << 20)
    return int(min(max(need, 16 << 20), VMEM_LIMIT_CAP))


def _nbytes(shape, dtype):
    return math.prod(shape) * jnp.dtype(dtype).itemsize


def _rms_norm_rows(x, gain):
    ms = jnp.mean(x * x, axis=-1, keepdims=True)
    return x * lax.rsqrt(ms + RMS_EPS) * gain


def _mm(a, b):
    return jnp.dot(a, b, preferred_element_type=F32)


def _mm_nt(a, b):
    return lax.dot_general(a, b, (((1,), (1,)), ((), ())), preferred_element_type=F32)


def _mm_tn(a, b):
    return lax.dot_general(a, b, (((0,), (0,)), ((), ())), preferred_element_type=F32)


def _ffn_body(x_ref, g_ref, wg_ref, wu_ref, wd_ref, o_ref, xn_ref):
    j = pl.program_id(1)
    last = pl.num_programs(1) - 1

    @pl.when(j == 0)
    def _():
        xn_ref[...] = _rms_norm_rows(x_ref[...], g_ref[...]).astype(BF16)
        o_ref[...] = jnp.zeros_like(o_ref)

    xn = xn_ref[...]
    a = _mm(xn, wg_ref[...])
    u = _mm(xn, wu_ref[...])
    h = (a * jax.nn.sigmoid(a) * u).astype(BF16)
    o_ref[...] += _mm(h, wd_ref[...])

    @pl.when(j == last)
    def _():
        o_ref[...] = x_ref[...] + 0.5 * o_ref[...]


def _ffn(x, gain, wg, wu, wd, *, tm, tf):
    m, d = x.shape
    f = wg.shape[1]
    assert m % tm == 0 and f % tf == 0
    blocks = (_nbytes((tm, d), F32) * 2 + _nbytes((d, tf), BF16) * 3)
    temps = _nbytes((tm, tf), F32) * 4
    return pl.pallas_call(
        _ffn_body,
        grid=(m // tm, f // tf),
        in_specs=[
            pl.BlockSpec((tm, d), lambda i, j: (i, 0)),
            pl.BlockSpec((1, d), lambda i, j: (0, 0)),
            pl.BlockSpec((d, tf), lambda i, j: (0, j)),
            pl.BlockSpec((d, tf), lambda i, j: (0, j)),
            pl.BlockSpec((tf, d), lambda i, j: (j, 0)),
        ],
        out_specs=pl.BlockSpec((tm, d), lambda i, j: (i, 0)),
        out_shape=jax.ShapeDtypeStruct((m, d), F32),
        scratch_shapes=[pltpu.VMEM((tm, d), BF16)],
        compiler_params=pltpu.CompilerParams(
            dimension_semantics=("parallel", "arbitrary"),
            vmem_limit_bytes=_vmem_limit(blocks, _nbytes((tm, d), BF16), temps)),
        name="ffn",
    )(x, gain.reshape(1, d), wg, wu, wd)


def _proj_body(x_ref, g_ref, w_ref, o_ref, xn_ref):
    @pl.when(pl.program_id(1) == 0)
    def _():
        xn_ref[...] = _rms_norm_rows(x_ref[...], g_ref[...]).astype(BF16)

    o_ref[...] = _mm(xn_ref[...], w_ref[...])


def _proj(x, gain, w, *, tm, tn):
    m, d = x.shape
    n = w.shape[1]
    assert m % tm == 0 and n % tn == 0
    blocks = _nbytes((tm, d), F32) + _nbytes((d, tn), BF16) + _nbytes((tm, tn), F32)
    return pl.pallas_call(
        _proj_body,
        grid=(m // tm, n // tn),
        in_specs=[
            pl.BlockSpec((tm, d), lambda i, j: (i, 0)),
            pl.BlockSpec((1, d), lambda i, j: (0, 0)),
            pl.BlockSpec((d, tn), lambda i, j: (0, j)),
        ],
        out_specs=pl.BlockSpec((tm, tn), lambda i, j: (i, j)),
        out_shape=jax.ShapeDtypeStruct((m, n), F32),
        scratch_shapes=[pltpu.VMEM((tm, d), BF16)],
        compiler_params=pltpu.CompilerParams(
            dimension_semantics=("parallel", "arbitrary"),
            vmem_limit_bytes=_vmem_limit(blocks, _nbytes((tm, d), BF16), 2 * _nbytes((tm, tn), F32))),
        name="proj",
    )(x, gain.reshape(1, d), w)


def _attn_body(q_ref, k_ref, v_ref, ck_ref, cv_ref, qg_ref, kg_ref, bvec_ref, o_ref, kn_ref,
               qn_ref, kp_ref, vp_ref, bias_ref, *, seq, tq, chunk, left_valid, group):
    keep = kn_ref.shape[1]
    qn_ref[...] = _rms_norm_rows(q_ref[0], qg_ref[...]).astype(BF16)
    kn = _rms_norm_rows(k_ref[0], kg_ref[...])
    kn_ref[0] = kn[seq - keep:, :]
    kp_ref[0:LEFT, :] = ck_ref[0].astype(BF16)
    vp_ref[0:LEFT, :] = cv_ref[0].astype(BF16)
    kp_ref[LEFT:LEFT + seq, :] = kn.astype(BF16)
    vp_ref[LEFT:LEFT + seq, :] = v_ref[0].astype(BF16)

    width = LEFT + tq
    bw = bvec_ref.shape[-1]
    bias = pltpu.roll(jnp.broadcast_to(bvec_ref[0], (tq, bw)), 0, 1, stride=1, stride_axis=0)[:, :width]
    if tq > chunk:
        row = lax.broadcasted_iota(jnp.int32, (tq, width), 0)
        col = lax.broadcasted_iota(jnp.int32, (tq, width), 1)
        chunk_start = (row // chunk) * chunk
        in_band = (col >= chunk_start) & (col < chunk_start + LEFT + chunk)
        bias = jnp.where(in_band, bias, NEG_INF)
    bias_ref[...] = bias
    scale = ATT_HEAD_DIM ** -0.5

    n_chunks = tq // chunk
    spans = []
    for r in range(n_chunks):
        lo = (r * chunk) // LANES * LANES
        hi = min(width, -(-(r * chunk + LEFT + chunk) // LANES) * LANES)
        spans.append((lo, hi))

    def block_bases(t):
        return [pl.multiple_of((t * group + u) * tq, tq) for u in range(group)]

    def blocks(t, mask_left):
        bases = block_bases(t)
        scores = [_mm_nt(qn_ref[pl.ds(base, tq), :], kp_ref[pl.ds(base, width), :]) for base in bases]
        pieces = [(u, r) for u in range(group) for r in range(n_chunks)]
        masked_scores = []
        for u, r in pieces:
            lo, hi = spans[r]
            rs = slice(r * chunk, (r + 1) * chunk)
            band = bias_ref[rs, lo:hi]
            ok = band > 0.5 * NEG_INF
            if mask_left:
                col = lax.broadcasted_iota(jnp.int32, (chunk, hi - lo), 1) + lo
                ok = ok & (col >= LEFT - bases[u])
            masked_scores.append(jnp.where(ok, scores[u][rs, lo:hi] * scale + band, NEG_INF))
        maxes = [jnp.max(s, axis=-1, keepdims=True) for s in masked_scores]
        exps = [jnp.exp(s - m) for s, m in zip(masked_scores, maxes)]
        invs = [1.0 / jnp.sum(e, axis=-1, keepdims=True) for e in exps]
        rows = []
        for (u, r), e in zip(pieces, exps):
            lo, hi = spans[r]
            parts = ([jnp.zeros((chunk, lo), BF16)] if lo else []) + [e.astype(BF16)]
            parts += [jnp.zeros((chunk, width - hi), BF16)] if hi < width else []
            rows.append(jnp.concatenate(parts, axis=1) if len(parts) > 1 else parts[0])
        for u, base in enumerate(bases):
            mine = slice(u * n_chunks, (u + 1) * n_chunks)
            w = jnp.concatenate(rows[mine], axis=0) if n_chunks > 1 else rows[mine][0]
            inv = jnp.concatenate(invs[mine], axis=0) if n_chunks > 1 else invs[mine][0]
            o_ref[0, pl.ds(base, tq), :] = _mm(w, vp_ref[pl.ds(base, width), :]) * inv

    n_iter = seq // (tq * group)
    n_left = 0 if left_valid else min(n_iter, -(-LEFT // (tq * group)))

    def masked(t, carry):
        blocks(t, True)
        return carry

    def plain(t, carry):
        blocks(t, False)
        return carry

    lax.fori_loop(0, n_left, masked, 0)
    lax.fori_loop(n_left, n_iter, plain, 0)


def _attention(proj, ctx_k, ctx_v, q_gain, k_gain, bias_vec, *, heads, tq, chunk, left_valid, group):
    b, seq, _ = proj.shape
    dh = ATT_HEAD_DIM
    bw = bias_vec.shape[-1]
    keep = min(seq, LEFT)
    assert seq % (tq * group) == 0 and tq % chunk == 0 and ctx_k.shape[1] == LEFT
    per_batch_ctx = ctx_k.shape[0] == b
    ctx_map = (lambda bi, h: (bi, 0, h)) if per_batch_ctx else (lambda bi, h: (0, 0, h))
    blocks = (4 * _nbytes((seq, dh), F32) + 2 * _nbytes((LEFT, dh), F32) + _nbytes((8, bw), F32)
              + _nbytes((keep, dh), F32))
    scratch = (_nbytes((seq, dh), BF16) + 2 * _nbytes((LEFT + seq, dh), BF16)
               + _nbytes((tq, LEFT + tq), F32))
    return pl.pallas_call(
        functools.partial(_attn_body, seq=seq, tq=tq, chunk=chunk, left_valid=left_valid, group=group),
        grid=(b, heads),
        in_specs=[
            pl.BlockSpec((1, seq, dh), lambda bi, h: (bi, 0, h)),
            pl.BlockSpec((1, seq, dh), lambda bi, h: (bi, 0, heads + h)),
            pl.BlockSpec((1, seq, dh), lambda bi, h: (bi, 0, 2 * heads + h)),
            pl.BlockSpec((1, LEFT, dh), ctx_map),
            pl.BlockSpec((1, LEFT, dh), ctx_map),
            pl.BlockSpec((1, dh), lambda bi, h: (0, 0)),
            pl.BlockSpec((1, dh), lambda bi, h: (0, 0)),
            pl.BlockSpec((1, 1, bw), lambda bi, h: (h, 0, 0)),
        ],
        out_specs=[
            pl.BlockSpec((1, seq, dh), lambda bi, h: (bi, 0, h)),
            pl.BlockSpec((1, keep, dh), lambda bi, h: (bi, 0, h)),
        ],
        out_shape=[jax.ShapeDtypeStruct((b, seq, heads * dh), F32),
                   jax.ShapeDtypeStruct((b, keep, heads * dh), F32)],
        scratch_shapes=[pltpu.VMEM((seq, dh), BF16), pltpu.VMEM((LEFT + seq, dh), BF16),
                        pltpu.VMEM((LEFT + seq, dh), BF16), pltpu.VMEM((tq, LEFT + tq), F32)],
        compiler_params=pltpu.CompilerParams(
            dimension_semantics=("parallel", "parallel"),
            vmem_limit_bytes=_vmem_limit(blocks, scratch, 8 * group * _nbytes((tq, LEFT + tq), F32)
                                         + 3 * _nbytes((seq, dh), F32))),
        name="attn",
    )(proj, proj, proj, ctx_k, ctx_v, q_gain.reshape(1, dh), k_gain.reshape(1, dh), bias_vec)


def _band_bias_vec(rel_bias, tq):
    width = LEFT + tq
    bw = -(-(width + tq) // LANES) * LANES
    m = jnp.arange(bw)
    d = jnp.where(m < width, m, m - bw)
    idx = jnp.clip(LEFT - d, -REL_CLIP, REL_CLIP) + REL_CLIP
    return rel_bias[idx].T.astype(F32)[:, None, :]


def _split3(x):
    h1 = x.astype(BF16)
    r1 = x - h1.astype(F32)
    h2 = r1.astype(BF16)
    h3 = (r1 - h2.astype(F32)).astype(BF16)
    return h1, h2, h3


def _wkv_body(x_ref, sp_ref, s0_ref, mu_ref, w0_ref, wl_ref, a0_ref, al_ref, gl_ref, kk_ref, ka_ref,
              rk_ref, lnw_ref, lnb_ref, o_ref, so_ref, carry_ref, s_ref, *, clen, heads, col0, nseq):
    c = pl.program_id(1)
    n = RWKV_HEAD_DIM
    width = heads * n
    pairs = heads // 2
    pw = 2 * n
    tw = 2 * clen
    bf = lambda t: t.astype(BF16)

    @pl.when(c == 0)
    def _():
        zeros = jnp.zeros((n, n), F32)
        for i in range(nseq):
            carry_ref[i] = sp_ref[i]
            for q in range(pairs):
                s_ref[i * pairs + q] = jnp.concatenate(
                    [jnp.concatenate([s0_ref[i, 2 * q], zeros], axis=1),
                     jnp.concatenate([zeros, s0_ref[i, 2 * q + 1]], axis=1)], axis=0)

    trow = lax.broadcasted_iota(jnp.int32, (SUBLANES, 1), 0)
    ti = lax.broadcasted_iota(jnp.int32, (clen, clen), 0)
    si = lax.broadcasted_iota(jnp.int32, (clen, clen), 1)
    tri = (si <= ti).astype(BF16)
    levels = max(1, (clen - 1).bit_length())

    def prepare(i):
        xb = x_ref[i, :, col0:]
        rolled = pltpu.roll(xb, 1, 0)
        shifted = jnp.concatenate([jnp.where(trow == 0, carry_ref[i], rolled[:SUBLANES]), rolled[SUBLANES:]],
                                  axis=0)
        carry_ref[i] = xb[clen - 1:clen, :]
        xm = xb + mu_ref[...] * (shifted - xb)

        r = xm[:, 0:width]
        k = xm[:, width:2 * width]
        v = xm[:, 2 * width:3 * width]
        o3 = 3 * width
        w_lo = xm[:, o3:o3 + W_LORA]
        a_lo = xm[:, o3 + W_LORA:o3 + W_LORA + A_LORA]
        g_lo = xm[:, o3 + W_LORA + A_LORA:o3 + W_LORA + A_LORA + G_LORA]

        log_decay = -math.exp(-0.5) * jax.nn.sigmoid(w0_ref[...] + _mm(bf(jnp.tanh(w_lo)), wl_ref[...]))
        a = jax.nn.sigmoid(a0_ref[...] + _mm(bf(a_lo), al_ref[...]))
        g = _mm(bf(jax.nn.sigmoid(g_lo)), gl_ref[...])
        kk = k * kk_ref[...]
        k = k * (1.0 + (a - 1.0) * ka_ref[...])

        d1, d2, d3 = _split3(log_decay)
        cum = _mm(tri, d1) + _mm(tri, d2) + _mm(tri, d3)
        g_in = jnp.exp(cum)
        g_inv = jnp.exp(-cum)
        g_last = g_in[clen - 1:clen, :]
        k_h = k * g_inv
        return dict(v=v, a=a, g=g, kk=kk, rk=r * k * rk_ref[...], g_inv=g_inv, g_ex=jnp.exp(cum - log_decay),
                    g_last=g_last, r_t=r * g_in, k_h=k_h, k_l=k_h * g_last)

    pre = [prepare(i) for i in range(nseq)]

    first = lax.broadcasted_iota(jnp.int32, (1, pw), 1) < n
    first_t = lax.broadcasted_iota(jnp.int32, (1, tw), 1) < clen
    row2 = lax.broadcasted_iota(jnp.int32, (2 * clen, 2 * tw), 0)
    spos = lax.broadcasted_iota(jnp.int32, (2 * clen, 2 * tw), 1) % clen
    causal = spos < jnp.where(row2 < clen, row2, row2 - clen + 1)
    same_head = ((lax.broadcasted_iota(jnp.int32, (pw, pw), 0) < n)
                 == (lax.broadcasted_iota(jnp.int32, (pw, pw), 1) < n))

    def head_sums(x):
        s0 = jnp.sum(jnp.where(first, x, 0.0), axis=-1, keepdims=True)
        s1 = jnp.sum(jnp.where(first, 0.0, x), axis=-1, keepdims=True)
        return jnp.where(first, s0, s1)

    def block_diag(x, mask):
        zero = jnp.zeros_like(x)
        return jnp.concatenate([jnp.where(mask, x, zero), jnp.where(mask, zero, x)], axis=0)

    units = [(i, q) for i in range(nseq) for q in range(pairs)]
    us = range(len(units))
    cols = [slice(q * pw, (q + 1) * pw) for _, q in units]
    val = lambda name, u: pre[units[u][0]][name][:, cols[u]]

    b_hat, a_t, v_bd, s_prev, s_bf = [], [], [], [], []
    for u in us:
        kk_u = val("kk", u)
        kk_u = kk_u * jnp.minimum(lax.rsqrt(head_sums(kk_u * kk_u)), 1e12)
        b_hat.append(kk_u * val("a", u) * val("g_inv", u))
        a_t.append(-kk_u * val("g_ex", u))
        v_bd.append(block_diag(bf(val("v", u)), first))
        s_prev.append(s_ref[u])
        s_bf.append(bf(s_prev[u]))

    ar = [bf(jnp.concatenate([a_t[u], val("r_t", u)], axis=0)) for u in us]
    p = [_mm_nt(ar[u], jnp.concatenate([block_diag(bf(b_hat[u]), first),
                                        block_diag(bf(val("k_h", u)), first)], axis=0)) for u in us]
    pm = [bf(jnp.where(causal, p[u], 0.0)) for u in us]
    nk = [pm[u][:clen, :tw] for u in us]
    a_rb = [pm[u][clen:, :tw] for u in us]

    sv = [_mm_nt(ar[u], s_bf[u]) + _mm(pm[u][:, tw:], v_bd[u]) for u in us]
    sa = [sv[u][:clen] for u in us]
    for lvl in range(levels):
        if lvl + 1 < levels:
            both = [_mm(nk[u], jnp.concatenate([block_diag(bf(sa[u]), first), block_diag(nk[u], first_t)],
                                               axis=1)) for u in us]
            sa = [sa[u] + both[u][:, :pw] for u in us]
            nk = [bf(both[u][:, pw:]) for u in us]
        else:
            sa = [sa[u] + _mm(nk[u], block_diag(bf(sa[u]), first)) for u in us]

    sa_bf = [bf(sa[u]) for u in us]
    ys = [sv[u][clen:] + _mm(a_rb[u], block_diag(sa_bf[u], first)) for u in us]
    for u in us:
        g_last = val("g_last", u)
        upd = _mm_tn(jnp.concatenate([sa_bf[u], bf(val("v", u))], axis=0),
                     jnp.concatenate([bf(b_hat[u] * g_last), bf(val("k_l", u))], axis=0))
        s_ref[u] = s_prev[u] * g_last + jnp.where(same_head, upd, 0.0)

    outs = []
    inv_n = 1.0 / n
    for u in us:
        y = ys[u]
        mean = head_sums(y) * inv_n
        var = head_sums(jnp.square(y - mean)) * inv_n
        y = (y - mean) * lax.rsqrt(var + GN_EPS)
        y = y * lnw_ref[:, cols[u]] + lnb_ref[:, cols[u]]
        y = y + head_sums(val("rk", u)) * val("v", u)
        outs.append(y)

    for i in range(nseq):
        o_ref[i] = jnp.concatenate(outs[i * pairs:(i + 1) * pairs], axis=1) * pre[i]["g"]

    @pl.when(c == pl.num_programs(1) - 1)
    def _():
        for i in range(nseq):
            for q in range(pairs):
                so_ref[i, 2 * q] = s_ref[i * pairs + q, 0:n, 0:n]
                so_ref[i, 2 * q + 1] = s_ref[i * pairs + q, n:pw, n:pw]


def _wkv(proj, shift_prev, state0, mu, w0, w_lora, a0, a_lora, g_lora, k_k, k_a, r_k, lnx_w, lnx_b, *, clen,
         nseq):
    b, t, full_w = proj.shape
    heads = state0.shape[1]
    n = RWKV_HEAD_DIM
    width = heads * n
    pw = shift_prev.shape[-1]
    col0 = full_w - pw
    assert t % clen == 0 and pw == 3 * width + W_LORA + A_LORA + G_LORA and heads % 2 == 0
    assert col0 % LANES == 0 and 2 * n == LANES and b % nseq == 0
    row = lambda a: a.reshape(1, -1)
    full = lambda shape: pl.BlockSpec(shape, lambda bi, c: (0,) * len(shape))
    blocks = nseq * (_nbytes((clen, full_w), F32) + _nbytes((clen, width), F32)
                     + 2 * _nbytes((heads, n, LANES), F32))
    state_scratch = _nbytes((nseq * heads // 2, 2 * n, 2 * n), F32)
    return pl.pallas_call(
        functools.partial(_wkv_body, clen=clen, heads=heads, col0=col0, nseq=nseq),
        grid=(b // nseq, t // clen),
        in_specs=[
            pl.BlockSpec((nseq, clen, full_w), lambda bi, c: (bi, c, 0)),
            pl.BlockSpec((nseq, 1, pw), lambda bi, c: (bi, 0, 0)),
            pl.BlockSpec((nseq, heads, n, n), lambda bi, c: (bi, 0, 0, 0)),
            full((1, pw)), full((1, width)), full((W_LORA, width)), full((1, width)), full((A_LORA, width)),
            full((G_LORA, width)), full((1, width)), full((1, width)), full((1, width)), full((1, width)),
            full((1, width)),
        ],
        out_specs=[
            pl.BlockSpec((nseq, clen, width), lambda bi, c: (bi, c, 0)),
            pl.BlockSpec((nseq, heads, n, n), lambda bi, c: (bi, 0, 0, 0)),
        ],
        out_shape=[jax.ShapeDtypeStruct((b, t, width), F32), jax.ShapeDtypeStruct((b, heads, n, n), F32)],
        scratch_shapes=[pltpu.VMEM((nseq, 1, pw), F32), pltpu.VMEM((nseq * heads // 2, 2 * n, 2 * n), F32)],
        compiler_params=pltpu.CompilerParams(
            dimension_semantics=("parallel", "arbitrary"),
            vmem_limit_bytes=_vmem_limit(blocks, state_scratch, 40 * nseq * _nbytes((clen, width), F32))),
        name="wkv",
    )(proj, shift_prev, state0, row(mu), row(w0), w_lora, row(a0), a_lora, g_lora, row(k_k), row(k_a),
      row(r_k), row(lnx_w), row(lnx_b))


def _outproj_body(h_ref, att_ref, rw_ref, wa_ref, wr_ref, o_ref):
    o_ref[...] = (h_ref[...] + _mm(att_ref[...].astype(BF16), wa_ref[...])
                  + _mm(rw_ref[...].astype(BF16), wr_ref[...]))


def _outproj(h, att, rw, w_out, *, tm):
    m, d = h.shape
    ka, kr = att.shape[1], rw.shape[1]
    assert m % tm == 0 and ka == kr and w_out.shape == (ka + kr, d)
    blocks = (2 * _nbytes((tm, d), F32) + _nbytes((tm, ka), F32) + _nbytes((tm, kr), F32)
              + _nbytes((ka, d), BF16) + _nbytes((kr, d), BF16))
    return pl.pallas_call(
        _outproj_body,
        grid=(m // tm,),
        in_specs=[
            pl.BlockSpec((tm, d), lambda i: (i, 0)),
            pl.BlockSpec((tm, ka), lambda i: (i, 0)),
            pl.BlockSpec((tm, kr), lambda i: (i, 0)),
            pl.BlockSpec((ka, d), lambda i: (0, 0)),
            pl.BlockSpec((kr, d), lambda i: (1, 0)),
        ],
        out_specs=pl.BlockSpec((tm, d), lambda i: (i, 0)),
        out_shape=jax.ShapeDtypeStruct((m, d), F32),
        compiler_params=pltpu.CompilerParams(
            dimension_semantics=("parallel",),
            vmem_limit_bytes=_vmem_limit(blocks, 0, 2 * _nbytes((tm, d), F32))),
        name="outproj",
    )(h, att, rw, w_out, w_out)


def _ple_body(x_ref, g_ref, wg_ref, p_ref, wp_ref, o_ref):
    x = x_ref[...]
    gate = jax.nn.sigmoid(_mm(_rms_norm_rows(x, g_ref[...]).astype(BF16), wg_ref[...]))
    o_ref[...] = x + gate * _mm(p_ref[...].astype(BF16), wp_ref[...])


def _ple(x, gain, w_gate, p, w_proj, *, tm):
    m, d = x.shape
    pd = p.shape[1]
    assert m % tm == 0
    blocks = (2 * _nbytes((tm, d), F32) + _nbytes((d, d), BF16) + _nbytes((tm, pd), F32)
              + _nbytes((pd, d), BF16))
    return pl.pallas_call(
        _ple_body,
        grid=(m // tm,),
        in_specs=[
            pl.BlockSpec((tm, d), lambda i: (i, 0)),
            pl.BlockSpec((1, d), lambda i: (0, 0)),
            pl.BlockSpec((d, d), lambda i: (0, 0)),
            pl.BlockSpec((tm, pd), lambda i: (i, 0)),
            pl.BlockSpec((pd, d), lambda i: (0, 0)),
        ],
        out_specs=pl.BlockSpec((tm, d), lambda i: (i, 0)),
        out_shape=jax.ShapeDtypeStruct((m, d), F32),
        compiler_params=pltpu.CompilerParams(
            dimension_semantics=("parallel",),
            vmem_limit_bytes=_vmem_limit(blocks, 0, 4 * _nbytes((tm, d), F32))),
        name="ple",
    )(x, gain.reshape(1, d), w_gate, p, w_proj)


def _pick(m, candidates):
    for c in candidates:
        if m % c == 0:
            return c
    raise ValueError(f"no tile for {m}")


def _trunk_layer(x, p, lw, ctx_k, ctx_v, left_valid, shift_prev, wkv_prev):
    b, t, d = x.shape
    m = b * t
    tm = _pick(m, (512, 256, 128))
    tm_big = _pick(m, (1024, 512, 256, 128))
    tf = 512
    x2 = x.reshape(m, d)
    att_w = lw["w_out"].shape[0] // 2
    heads = att_w // ATT_HEAD_DIM

    h = _ffn(x2, lw["norm_ffn1"], lw["ffn1_gate"], lw["ffn1_up"], lw["ffn1_down"], tm=tm_big, tf=tf)
    proj = _proj(h, lw["norm_mix"], lw["w_in"], tm=tm_big, tn=lw["w_in"].shape[1] // 5)
    proj3 = proj.reshape(b, t, -1)

    clen = _pick(t, (CHUNK, 16))
    tq = _pick(t, (2 * CHUNK, CHUNK, 16))
    group = _pick(t // tq, (4, 2, 1))
    att, k_keep = _attention(proj3, ctx_k, ctx_v, lw["q_norm"], lw["k_norm"], _band_bias_vec(lw["rel_bias"], tq),
                             heads=heads, tq=tq, chunk=min(CHUNK, tq), left_valid=left_valid, group=group)
    rw, wkv_new = _wkv(proj3, shift_prev, wkv_prev, lw["rwkv_mu"], lw["rwkv_w0"], lw["rwkv_w_lora"],
                       lw["rwkv_a0"], lw["rwkv_a_lora"], lw["rwkv_g_lora"], lw["rwkv_k_k"], lw["rwkv_k_a"],
                       lw["rwkv_r_k"], lw["rwkv_lnx_w"], lw["rwkv_lnx_b"], clen=clen, nseq=_pick(b, (2, 1)))

    h = _outproj(h, att.reshape(m, att_w), rw.reshape(m, -1), lw["w_out"], tm=tm)
    h = _ffn(h, lw["norm_ffn2"], lw["ffn2_gate"], lw["ffn2_up"], lw["ffn2_down"], tm=tm_big, tf=tf)
    h = _ple(h, lw["norm_ple"], lw["ple_gate"], p.reshape(m, -1), lw["ple_proj"], tm=tm)

    keep = k_keep.shape[1]
    k_keep = k_keep.reshape(b, keep, heads, ATT_HEAD_DIM)
    v_keep = proj3[:, t - keep:, 2 * att_w:3 * att_w].reshape(b, keep, heads, ATT_HEAD_DIM)
    return h.reshape(b, t, d), (k_keep, v_keep, wkv_new, proj3[:, -1:, 3 * att_w:])


def kernel(x_prompt, x_sample, cache_att_k, cache_att_v, state_wkv, state_shift, p_prompt, p_sample, norm_ffn1, ffn1_gate, ffn1_up, ffn1_down, norm_mix, w_in, q_norm, k_norm, rel_bias, rwkv_mu, rwkv_w0, rwkv_w_lora, rwkv_a0, rwkv_a_lora, rwkv_g_lora, rwkv_k_k, rwkv_k_a, rwkv_r_k, rwkv_lnx_w, rwkv_lnx_b, w_out, norm_ffn2, ffn2_gate, ffn2_up, ffn2_down, norm_ple, ple_gate, ple_proj):
    depth = norm_ffn1.shape[0]
    hp, hs = x_prompt, x_sample
    bp = hp.shape[0]
    bs = hs.shape[0]
    att_w = w_out.shape[1] // 2
    rwkv_heads = rwkv_r_k.shape[1]
    rwkv_pw = rwkv_mu.shape[1]
    prompt_states, sample_states = [], []
    for i in range(depth):
        lw = {
            "norm_ffn1": norm_ffn1[i], "ffn1_gate": ffn1_gate[i].astype(BF16), "ffn1_up": ffn1_up[i].astype(BF16),
            "ffn1_down": ffn1_down[i].astype(BF16), "norm_mix": norm_mix[i], "w_in": w_in[i].astype(BF16),
            "q_norm": q_norm[i], "k_norm": k_norm[i],
            "rel_bias": rel_bias[i], "rwkv_mu": rwkv_mu[i], "rwkv_w0": rwkv_w0[i],
            "rwkv_w_lora": rwkv_w_lora[i].astype(BF16), "rwkv_a0": rwkv_a0[i],
            "rwkv_a_lora": rwkv_a_lora[i].astype(BF16), "rwkv_g_lora": rwkv_g_lora[i].astype(BF16),
            "rwkv_k_k": rwkv_k_k[i], "rwkv_k_a": rwkv_k_a[i], "rwkv_r_k": rwkv_r_k[i].reshape(-1),
            "rwkv_lnx_w": rwkv_lnx_w[i], "rwkv_lnx_b": rwkv_lnx_b[i],
            "w_out": w_out[i].astype(BF16), "norm_ffn2": norm_ffn2[i], "ffn2_gate": ffn2_gate[i].astype(BF16),
            "ffn2_up": ffn2_up[i].astype(BF16), "ffn2_down": ffn2_down[i].astype(BF16),
            "norm_ple": norm_ple[i], "ple_gate": ple_gate[i].astype(BF16), "ple_proj": ple_proj[i].astype(BF16),
        }
        zero_ctx = jnp.zeros((1, LEFT, att_w), F32)
        zero_shift = jnp.zeros((bp, 1, rwkv_pw), hp.dtype)
        zero_wkv = jnp.zeros((bp, rwkv_heads, RWKV_HEAD_DIM, RWKV_HEAD_DIM), hp.dtype)
        hp, sp = _trunk_layer(hp, p_prompt[i], lw, zero_ctx, zero_ctx, False, zero_shift, zero_wkv)
        ctx_k = cache_att_k[i].reshape(bs, -1, att_w)
        ctx_v = cache_att_v[i].reshape(bs, -1, att_w)
        hs, ss = _trunk_layer(hs, p_sample[i], lw, ctx_k, ctx_v, True, state_shift[i], state_wkv[i])
        prompt_states.append(sp)
        sample_states.append(ss)
    stack = lambda states, j: jnp.stack([s[j] for s in states])
    return (hp, hs, stack(prompt_states, 0), stack(prompt_states, 1), stack(prompt_states, 2),
            stack(prompt_states, 3), stack(sample_states, 0), stack(sample_states, 1),
            stack(sample_states, 2), stack(sample_states, 3))
```

```python
import functools
import math

import jax
import jax.numpy as jnp
from jax import lax
from jax.experimental import pallas as pl
from jax.experimental.pallas import tpu as pltpu

F32 = jnp.float32
BF16 = jnp.bfloat16

CHUNK = 64
LEFT = 512
ATT_HEAD_DIM = 128
REL_CLIP = 256
RWKV_HEAD_DIM = 64
W_LORA = 64
A_LORA = 64
G_LORA = 128
RMS_EPS = 1e-6
GN_EPS = 64e-5
NEG_INF = -1e30

LANES = 128
SUBLANES = 8
V7X_VMEM_BYTES = 64 * 1024 * 1024
VMEM_LIMIT_CAP = V7X_VMEM_BYTES - 8 * 1024 * 1024


def _vmem_limit(block_bytes, scratch_bytes, temp_bytes):
    need = 2 * block_bytes + scratch_bytes + temp_bytes + (4 << 20)
    return int(min(max(need, 16 << 20), VMEM_LIMIT_CAP))


def _nbytes(shape, dtype):
    return math.prod(shape) * jnp.dtype(dtype).itemsize


def _rms_norm_rows(x, gain):
    ms = jnp.mean(x * x, axis=-1, keepdims=True)
    return x * lax.rsqrt(ms + RMS_EPS) * gain


def _mm(a, b):
    return jnp.dot(a, b, preferred_element_type=F32)


def _mm_nt(a, b):
    return lax.dot_general(a, b, (((1,), (1,)), ((), ())), preferred_element_type=F32)


def _mm_tn(a, b):
    return lax.dot_general(a, b, (((0,), (0,)), ((), ())), preferred_element_type=F32)


def _ffn_body(x_ref, g_ref, wg_ref, wu_ref, wd_ref, o_ref, xn_ref):
    j = pl.program_id(1)
    last = pl.num_programs(1) - 1

    @pl.when(j == 0)
    def _():
        xn_ref[...] = _rms_norm_rows(x_ref[...], g_ref[...]).astype(BF16)
        o_ref[...] = jnp.zeros_like(o_ref)

    xn = xn_ref[...]
    a = _mm(xn, wg_ref[...])
    u = _mm(xn, wu_ref[...])
    h = (a * jax.nn.sigmoid(a) * u).astype(BF16)
    o_ref[...] += _mm(h, wd_ref[...])

    @pl.when(j == last)
    def _():
        o_ref[...] = x_ref[...] + 0.5 * o_ref[...]


def _ffn(x, gain, wg, wu, wd, *, tm, tf):
    m, d = x.shape
    f = wg.shape[1]
    assert m % tm == 0 and f % tf == 0
    blocks = (_nbytes((tm, d), F32) * 2 + _nbytes((d, tf), BF16) * 3)
    temps = _nbytes((tm, tf), F32) * 4
    return pl.pallas_call(
        _ffn_body,
        grid=(m // tm, f // tf),
        in_specs=[
            pl.BlockSpec((tm, d), lambda i, j: (i, 0)),
            pl.BlockSpec((1, d), lambda i, j: (0, 0)),
            pl.BlockSpec((d, tf), lambda i, j: (0, j)),
            pl.BlockSpec((d, tf), lambda i, j: (0, j)),
            pl.BlockSpec((tf, d), lambda i, j: (j, 0)),
        ],
        out_specs=pl.BlockSpec((tm, d), lambda i, j: (i, 0)),
        out_shape=jax.ShapeDtypeStruct((m, d), F32),
        scratch_shapes=[pltpu.VMEM((tm, d), BF16)],
        compiler_params=pltpu.CompilerParams(
            dimension_semantics=("parallel", "arbitrary"),
            vmem_limit_bytes=_vmem_limit(blocks, _nbytes((tm, d), BF16), temps)),
        name="ffn",
    )(x, gain.reshape(1, d), wg, wu, wd)


def _proj_body(x_ref, g_ref, w_ref, o_ref, xn_ref):
    @pl.when(pl.program_id(1) == 0)
    def _():
        xn_ref[...] = _rms_norm_rows(x_ref[...], g_ref[...]).astype(BF16)

    o_ref[...] = _mm(xn_ref[...], w_ref[...])


def _proj(x, gain, w, *, tm, tn):
    m, d = x.shape
    n = w.shape[1]
    assert m % tm == 0 and n % tn == 0
    blocks = _nbytes((tm, d), F32) + _nbytes((d, tn), BF16) + _nbytes((tm, tn), F32)
    return pl.pallas_call(
        _proj_body,
        grid=(m // tm, n // tn),
        in_specs=[
            pl.BlockSpec((tm, d), lambda i, j: (i, 0)),
            pl.BlockSpec((1, d), lambda i, j: (0, 0)),
            pl.BlockSpec((d, tn), lambda i, j: (0, j)),
        ],
        out_specs=pl.BlockSpec((tm, tn), lambda i, j: (i, j)),
        out_shape=jax.ShapeDtypeStruct((m, n), F32),
        scratch_shapes=[pltpu.VMEM((tm, d), BF16)],
        compiler_params=pltpu.CompilerParams(
            dimension_semantics=("parallel", "arbitrary"),
            vmem_limit_bytes=_vmem_limit(blocks, _nbytes((tm, d), BF16), 2 * _nbytes((tm, tn), F32))),
        name="proj",
    )(x, gain.reshape(1, d), w)


def _attn_body(q_ref, k_ref, v_ref, ck_ref, cv_ref, qg_ref, kg_ref, bvec_ref, o_ref, kn_ref,
               qn_ref, kp_ref, vp_ref, bias_ref, *, seq, tq, chunk, left_valid, group):
    keep = kn_ref.shape[1]
    qn_ref[...] = _rms_norm_rows(q_ref[0], qg_ref[...]).astype(BF16)
    kn = _rms_norm_rows(k_ref[0], kg_ref[...])
    kn_ref[0] = kn[seq - keep:, :]
    if left_valid:
        head = pl.program_id(1)
        kp_ref[0:LEFT, :] = ck_ref[0, :, head, :].astype(BF16)
        vp_ref[0:LEFT, :] = cv_ref[0, :, head, :].astype(BF16)
    else:
        kp_ref[0:LEFT, :] = jnp.zeros((LEFT, ATT_HEAD_DIM), BF16)
        vp_ref[0:LEFT, :] = jnp.zeros((LEFT, ATT_HEAD_DIM), BF16)
    kp_ref[LEFT:LEFT + seq, :] = kn.astype(BF16)
    vp_ref[LEFT:LEFT + seq, :] = v_ref[0].astype(BF16)

    width = LEFT + tq
    bw = bvec_ref.shape[-1]
    bias = pltpu.roll(jnp.broadcast_to(bvec_ref[0], (tq, bw)), 0, 1, stride=1, stride_axis=0)[:, :width]
    if tq > chunk:
        row = lax.broadcasted_iota(jnp.int32, (tq, width), 0)
        col = lax.broadcasted_iota(jnp.int32, (tq, width), 1)
        chunk_start = (row // chunk) * chunk
        in_band = (col >= chunk_start) & (col < chunk_start + LEFT + chunk)
        bias = jnp.where(in_band, bias, NEG_INF)
    bias_ref[...] = bias
    scale = ATT_HEAD_DIM ** -0.5

    n_chunks = tq // chunk
    spans = []
    for r in range(n_chunks):
        lo = (r * chunk) // LANES * LANES
        hi = min(width, -(-(r * chunk + LEFT + chunk) // LANES) * LANES)
        spans.append((lo, hi))

    def block_bases(t):
        return [pl.multiple_of((t * group + u) * tq, tq) for u in range(group)]

    def blocks(t, mask_left):
        bases = block_bases(t)
        scores = [_mm_nt(qn_ref[pl.ds(base, tq), :], kp_ref[pl.ds(base, width), :]) for base in bases]
        pieces = [(u, r) for u in range(group) for r in range(n_chunks)]
        masked_scores = []
        for u, r in pieces:
            lo, hi = spans[r]
            rs = slice(r * chunk, (r + 1) * chunk)
            band = bias_ref[rs, lo:hi]
            ok = band > 0.5 * NEG_INF
            if mask_left:
                col = lax.broadcasted_iota(jnp.int32, (chunk, hi - lo), 1) + lo
                ok = ok & (col >= LEFT - bases[u])
            masked_scores.append(jnp.where(ok, scores[u][rs, lo:hi] * scale + band, NEG_INF))
        maxes = [jnp.max(s, axis=-1, keepdims=True) for s in masked_scores]
        exps = [jnp.exp(s - m) for s, m in zip(masked_scores, maxes)]
        invs = [1.0 / jnp.sum(e, axis=-1, keepdims=True) for e in exps]
        rows = []
        for (u, r), e in zip(pieces, exps):
            lo, hi = spans[r]
            parts = ([jnp.zeros((chunk, lo), BF16)] if lo else []) + [e.astype(BF16)]
            parts += [jnp.zeros((chunk, width - hi), BF16)] if hi < width else []
            rows.append(jnp.concatenate(parts, axis=1) if len(parts) > 1 else parts[0])
        for u, base in enumerate(bases):
            mine = slice(u * n_chunks, (u + 1) * n_chunks)
            w = jnp.concatenate(rows[mine], axis=0) if n_chunks > 1 else rows[mine][0]
            inv = jnp.concatenate(invs[mine], axis=0) if n_chunks > 1 else invs[mine][0]
            o_ref[0, pl.ds(base, tq), :] = (_mm(w, vp_ref[pl.ds(base, width), :]) * inv).astype(o_ref.dtype)

    n_iter = seq // (tq * group)
    n_left = 0 if left_valid else min(n_iter, -(-LEFT // (tq * group)))

    def masked(t, carry):
        blocks(t, True)
        return carry

    def plain(t, carry):
        blocks(t, False)
        return carry

    lax.fori_loop(0, n_left, masked, 0)
    lax.fori_loop(n_left, n_iter, plain, 0)


def _attention(proj, ctx_k, ctx_v, q_gain, k_gain, bias_vec, *, heads, tq, chunk, left_valid, group):
    b, seq, _ = proj.shape
    dh = ATT_HEAD_DIM
    bw = bias_vec.shape[-1]
    keep = min(seq, LEFT)
    assert seq % (tq * group) == 0 and tq % chunk == 0 and ctx_k.shape[1:] == (LEFT, heads, dh)
    per_batch_ctx = ctx_k.shape[0] == b
    ctx_map = (lambda bi, h: (bi, 0, 0, 0)) if per_batch_ctx else (lambda bi, h: (0, 0, 0, 0))
    blocks = (4 * _nbytes((seq, dh), F32) + 2 * _nbytes((LEFT, heads, dh), F32) + _nbytes((8, bw), F32)
              + _nbytes((keep, dh), F32))
    scratch = (_nbytes((seq, dh), BF16) + 2 * _nbytes((LEFT + seq, dh), BF16)
               + _nbytes((tq, LEFT + tq), F32))
    return pl.pallas_call(
        functools.partial(_attn_body, seq=seq, tq=tq, chunk=chunk, left_valid=left_valid, group=group),
        grid=(b, heads),
        in_specs=[
            pl.BlockSpec((1, seq, dh), lambda bi, h: (bi, 0, h)),
            pl.BlockSpec((1, seq, dh), lambda bi, h: (bi, 0, heads + h)),
            pl.BlockSpec((1, seq, dh), lambda bi, h: (bi, 0, 2 * heads + h)),
            pl.BlockSpec((1, LEFT, heads, dh), ctx_map),
            pl.BlockSpec((1, LEFT, heads, dh), ctx_map),
            pl.BlockSpec((1, dh), lambda bi, h: (0, 0)),
            pl.BlockSpec((1, dh), lambda bi, h: (0, 0)),
            pl.BlockSpec((1, 1, bw), lambda bi, h: (h, 0, 0)),
        ],
        out_specs=[
            pl.BlockSpec((1, seq, dh), lambda bi, h: (bi, 0, h)),
            pl.BlockSpec((1, keep, dh), lambda bi, h: (bi, 0, h)),
        ],
        out_shape=[jax.ShapeDtypeStruct((b, seq, heads * dh), BF16),
                   jax.ShapeDtypeStruct((b, keep, heads * dh), F32)],
        scratch_shapes=[pltpu.VMEM((seq, dh), BF16), pltpu.VMEM((LEFT + seq, dh), BF16),
                        pltpu.VMEM((LEFT + seq, dh), BF16), pltpu.VMEM((tq, LEFT + tq), F32)],
        compiler_params=pltpu.CompilerParams(
            dimension_semantics=("parallel", "parallel"),
            vmem_limit_bytes=_vmem_limit(blocks, scratch, 8 * group * _nbytes((tq, LEFT + tq), F32)
                                         + 3 * _nbytes((seq, dh), F32))),
        name="attn",
    )(proj, proj, proj, ctx_k, ctx_v, q_gain.reshape(1, dh), k_gain.reshape(1, dh), bias_vec)


def _band_bias_vec(rel_bias, tq):
    width = LEFT + tq
    bw = -(-(width + tq) // LANES) * LANES
    m = jnp.arange(bw)
    d = jnp.where(m < width, m, m - bw)
    idx = jnp.clip(LEFT - d, -REL_CLIP, REL_CLIP) + REL_CLIP
    return rel_bias[idx].T.astype(F32)[:, None, :]


def _split3(x):
    h1 = x.astype(BF16)
    r1 = x - h1.astype(F32)
    h2 = r1.astype(BF16)
    h3 = (r1 - h2.astype(F32)).astype(BF16)
    return h1, h2, h3


def _wkv_body(x_ref, sp_ref, s0_ref, mu_ref, w0_ref, wl_ref, a0_ref, al_ref, gl_ref, kk_ref, ka_ref,
              rk_ref, lnw_ref, lnb_ref, o_ref, so_ref, carry_ref, s_ref, *, clen, heads, col0, nseq):
    c = pl.program_id(1)
    n = RWKV_HEAD_DIM
    width = heads * n
    pairs = heads // 2
    pw = 2 * n
    tw = 2 * clen
    bf = lambda t: t.astype(BF16)

    @pl.when(c == 0)
    def _():
        zeros = jnp.zeros((n, n), F32)
        for i in range(nseq):
            carry_ref[i] = sp_ref[i]
            for q in range(pairs):
                s_ref[i * pairs + q] = jnp.concatenate(
                    [jnp.concatenate([s0_ref[i, 2 * q], zeros], axis=1),
                     jnp.concatenate([zeros, s0_ref[i, 2 * q + 1]], axis=1)], axis=0)

    trow = lax.broadcasted_iota(jnp.int32, (SUBLANES, 1), 0)
    ti = lax.broadcasted_iota(jnp.int32, (clen, clen), 0)
    si = lax.broadcasted_iota(jnp.int32, (clen, clen), 1)
    tri = (si <= ti).astype(BF16)
    levels = max(1, (clen - 1).bit_length())

    def prepare(i):
        xb = x_ref[i, :, col0:]
        rolled = pltpu.roll(xb, 1, 0)
        shifted = jnp.concatenate([jnp.where(trow == 0, carry_ref[i], rolled[:SUBLANES]), rolled[SUBLANES:]],
                                  axis=0)
        carry_ref[i] = xb[clen - 1:clen, :]
        xm = xb + mu_ref[...] * (shifted - xb)

        r = xm[:, 0:width]
        k = xm[:, width:2 * width]
        v = xm[:, 2 * width:3 * width]
        o3 = 3 * width
        w_lo = xm[:, o3:o3 + W_LORA]
        a_lo = xm[:, o3 + W_LORA:o3 + W_LORA + A_LORA]
        g_lo = xm[:, o3 + W_LORA + A_LORA:o3 + W_LORA + A_LORA + G_LORA]

        log_decay = -math.exp(-0.5) * jax.nn.sigmoid(w0_ref[...] + _mm(bf(jnp.tanh(w_lo)), wl_ref[...]))
        a = jax.nn.sigmoid(a0_ref[...] + _mm(bf(a_lo), al_ref[...]))
        g = _mm(bf(jax.nn.sigmoid(g_lo)), gl_ref[...])
        kk = k * kk_ref[...]
        k = k * (1.0 + (a - 1.0) * ka_ref[...])

        d1, d2, d3 = _split3(log_decay)
        cum = _mm(tri, d1) + _mm(tri, d2) + _mm(tri, d3)
        g_in = jnp.exp(cum)
        g_inv = jnp.exp(-cum)
        g_last = g_in[clen - 1:clen, :]
        k_h = k * g_inv
        return dict(v=v, a=a, g=g, kk=kk, rk=r * k * rk_ref[...], g_inv=g_inv, g_ex=jnp.exp(cum - log_decay),
                    g_last=g_last, r_t=r * g_in, k_h=k_h, k_l=k_h * g_last)

    pre = [prepare(i) for i in range(nseq)]

    first = lax.broadcasted_iota(jnp.int32, (1, pw), 1) < n
    first_t = lax.broadcasted_iota(jnp.int32, (1, tw), 1) < clen
    row2 = lax.broadcasted_iota(jnp.int32, (2 * clen, 2 * tw), 0)
    spos = lax.broadcasted_iota(jnp.int32, (2 * clen, 2 * tw), 1) % clen
    causal = spos < jnp.where(row2 < clen, row2, row2 - clen + 1)
    same_head = ((lax.broadcasted_iota(jnp.int32, (pw, pw), 0) < n)
                 == (lax.broadcasted_iota(jnp.int32, (pw, pw), 1) < n))

    def head_sums(x):
        s0 = jnp.sum(jnp.where(first, x, 0.0), axis=-1, keepdims=True)
        s1 = jnp.sum(jnp.where(first, 0.0, x), axis=-1, keepdims=True)
        return jnp.where(first, s0, s1)

    def block_diag(x, mask):
        zero = jnp.zeros_like(x)
        return jnp.concatenate([jnp.where(mask, x, zero), jnp.where(mask, zero, x)], axis=0)

    units = [(i, q) for i in range(nseq) for q in range(pairs)]
    us = range(len(units))
    cols = [slice(q * pw, (q + 1) * pw) for _, q in units]
    val = lambda name, u: pre[units[u][0]][name][:, cols[u]]

    b_hat, a_t, v_bd, s_prev, s_bf = [], [], [], [], []
    for u in us:
        kk_u = val("kk", u)
        kk_u = kk_u * jnp.minimum(lax.rsqrt(head_sums(kk_u * kk_u)), 1e12)
        b_hat.append(kk_u * val("a", u) * val("g_inv", u))
        a_t.append(-kk_u * val("g_ex", u))
        v_bd.append(block_diag(bf(val("v", u)), first))
        s_prev.append(s_ref[u])
        s_bf.append(bf(s_prev[u]))

    ar = [bf(jnp.concatenate([a_t[u], val("r_t", u)], axis=0)) for u in us]
    p = [_mm_nt(ar[u], jnp.concatenate([block_diag(bf(b_hat[u]), first),
                                        block_diag(bf(val("k_h", u)), first)], axis=0)) for u in us]
    pm = [bf(jnp.where(causal, p[u], 0.0)) for u in us]
    nk = [pm[u][:clen, :tw] for u in us]
    a_rb = [pm[u][clen:, :tw] for u in us]

    sv = [_mm_nt(ar[u], s_bf[u]) + _mm(pm[u][:, tw:], v_bd[u]) for u in us]
    sa = [sv[u][:clen] for u in us]
    for lvl in range(levels):
        if lvl + 1 < levels:
            both = [_mm(nk[u], jnp.concatenate([block_diag(bf(sa[u]), first), block_diag(nk[u], first_t)],
                                               axis=1)) for u in us]
            sa = [sa[u] + both[u][:, :pw] for u in us]
            nk = [bf(both[u][:, pw:]) for u in us]
        else:
            sa = [sa[u] + _mm(nk[u], block_diag(bf(sa[u]), first)) for u in us]

    sa_bf = [bf(sa[u]) for u in us]
    ys = [sv[u][clen:] + _mm(a_rb[u], block_diag(sa_bf[u], first)) for u in us]
    for u in us:
        g_last = val("g_last", u)
        upd = _mm_tn(jnp.concatenate([sa_bf[u], bf(val("v", u))], axis=0),
                     jnp.concatenate([bf(b_hat[u] * g_last), bf(val("k_l", u))], axis=0))
        s_ref[u] = s_prev[u] * g_last + jnp.where(same_head, upd, 0.0)

    outs = []
    inv_n = 1.0 / n
    for u in us:
        y = ys[u]
        mean = head_sums(y) * inv_n
        var = head_sums(jnp.square(y - mean)) * inv_n
        y = (y - mean) * lax.rsqrt(var + GN_EPS)
        y = y * lnw_ref[:, cols[u]] + lnb_ref[:, cols[u]]
        y = y + head_sums(val("rk", u)) * val("v", u)
        outs.append(y)

    for i in range(nseq):
        o_ref[i] = (jnp.concatenate(outs[i * pairs:(i + 1) * pairs], axis=1) * pre[i]["g"]).astype(o_ref.dtype)

    @pl.when(c == pl.num_programs(1) - 1)
    def _():
        for i in range(nseq):
            for q in range(pairs):
                so_ref[i, 2 * q] = s_ref[i * pairs + q, 0:n, 0:n]
                so_ref[i, 2 * q + 1] = s_ref[i * pairs + q, n:pw, n:pw]


def _wkv(proj, shift_prev, state0, mu, w0, w_lora, a0, a_lora, g_lora, k_k, k_a, r_k, lnx_w, lnx_b, *, clen,
         nseq):
    b, t, full_w = proj.shape
    heads = state0.shape[1]
    n = RWKV_HEAD_DIM
    width = heads * n
    pw = shift_prev.shape[-1]
    col0 = full_w - pw
    assert t % clen == 0 and pw == 3 * width + W_LORA + A_LORA + G_LORA and heads % 2 == 0
    assert col0 % LANES == 0 and 2 * n == LANES and b % nseq == 0
    row = lambda a: a.reshape(1, -1)
    full = lambda shape: pl.BlockSpec(shape, lambda bi, c: (0,) * len(shape))
    blocks = nseq * (_nbytes((clen, full_w), F32) + _nbytes((clen, width), F32)
                     + 2 * _nbytes((heads, n, LANES), F32))
    state_scratch = _nbytes((nseq * heads // 2, 2 * n, 2 * n), F32)
    return pl.pallas_call(
        functools.partial(_wkv_body, clen=clen, heads=heads, col0=col0, nseq=nseq),
        grid=(b // nseq, t // clen),
        in_specs=[
            pl.BlockSpec((nseq, clen, full_w), lambda bi, c: (bi, c, 0)),
            pl.BlockSpec((nseq, 1, pw), lambda bi, c: (bi, 0, 0)),
            pl.BlockSpec((nseq, heads, n, n), lambda bi, c: (bi, 0, 0, 0)),
            full((1, pw)), full((1, width)), full((W_LORA, width)), full((1, width)), full((A_LORA, width)),
            full((G_LORA, width)), full((1, width)), full((1, width)), full((1, width)), full((1, width)),
            full((1, width)),
        ],
        out_specs=[
            pl.BlockSpec((nseq, clen, width), lambda bi, c: (bi, c, 0)),
            pl.BlockSpec((nseq, heads, n, n), lambda bi, c: (bi, 0, 0, 0)),
        ],
        out_shape=[jax.ShapeDtypeStruct((b, t, width), BF16),
                   jax.ShapeDtypeStruct((b, heads, n, n), F32)],
        scratch_shapes=[pltpu.VMEM((nseq, 1, pw), F32), pltpu.VMEM((nseq * heads // 2, 2 * n, 2 * n), F32)],
        compiler_params=pltpu.CompilerParams(
            dimension_semantics=("parallel", "arbitrary"),
            vmem_limit_bytes=_vmem_limit(blocks, state_scratch, 40 * nseq * _nbytes((clen, width), F32))),
        name="wkv",
    )(proj, shift_prev, state0, row(mu), row(w0), w_lora, row(a0), a_lora, g_lora, row(k_k), row(k_a),
      row(r_k), row(lnx_w), row(lnx_b))


def _outproj_body(h_ref, att_ref, rw_ref, wa_ref, wr_ref, o_ref):
    o_ref[...] = h_ref[...] + _mm(att_ref[...], wa_ref[...]) + _mm(rw_ref[...], wr_ref[...])


def _outproj(h, att, rw, w_out, *, tm):
    m, d = h.shape
    ka, kr = att.shape[1], rw.shape[1]
    assert m % tm == 0 and ka == kr and w_out.shape == (ka + kr, d)
    blocks = (2 * _nbytes((tm, d), F32) + _nbytes((tm, ka), BF16) + _nbytes((tm, kr), BF16)
              + _nbytes((ka, d), BF16) + _nbytes((kr, d), BF16))
    return pl.pallas_call(
        _outproj_body,
        grid=(m // tm,),
        in_specs=[
            pl.BlockSpec((tm, d), lambda i: (i, 0)),
            pl.BlockSpec((tm, ka), lambda i: (i, 0)),
            pl.BlockSpec((tm, kr), lambda i: (i, 0)),
            pl.BlockSpec((ka, d), lambda i: (0, 0)),
            pl.BlockSpec((kr, d), lambda i: (1, 0)),
        ],
        out_specs=pl.BlockSpec((tm, d), lambda i: (i, 0)),
        out_shape=jax.ShapeDtypeStruct((m, d), F32),
        compiler_params=pltpu.CompilerParams(
            dimension_semantics=("parallel",),
            vmem_limit_bytes=_vmem_limit(blocks, 0, 2 * _nbytes((tm, d), F32))),
        name="outproj",
    )(h, att, rw, w_out, w_out)


def _ple_body(x_ref, g_ref, wg_ref, p_ref, wp_ref, o_ref):
    x = x_ref[...]
    gate = jax.nn.sigmoid(_mm(_rms_norm_rows(x, g_ref[...]).astype(BF16), wg_ref[...]))
    o_ref[...] = x + gate * _mm(p_ref[...].astype(BF16), wp_ref[...])


def _ple(x, gain, w_gate, p, w_proj, *, tm):
    m, d = x.shape
    pd = p.shape[1]
    assert m % tm == 0
    blocks = (2 * _nbytes((tm, d), F32) + _nbytes((d, d), BF16) + _nbytes((tm, pd), F32)
              + _nbytes((pd, d), BF16))
    return pl.pallas_call(
        _ple_body,
        grid=(m // tm,),
        in_specs=[
            pl.BlockSpec((tm, d), lambda i: (i, 0)),
            pl.BlockSpec((1, d), lambda i: (0, 0)),
            pl.BlockSpec((d, d), lambda i: (0, 0)),
            pl.BlockSpec((tm, pd), lambda i: (i, 0)),
            pl.BlockSpec((pd, d), lambda i: (0, 0)),
        ],
        out_specs=pl.BlockSpec((tm, d), lambda i: (i, 0)),
        out_shape=jax.ShapeDtypeStruct((m, d), F32),
        compiler_params=pltpu.CompilerParams(
            dimension_semantics=("parallel",),
            vmem_limit_bytes=_vmem_limit(blocks, 0, 4 * _nbytes((tm, d), F32))),
        name="ple",
    )(x, gain.reshape(1, d), w_gate, p, w_proj)


def _pick(m, candidates):
    for c in candidates:
        if m % c == 0:
            return c
    raise ValueError(f"no tile for {m}")


def _trunk_layer(x, p, lw, ctx_k, ctx_v, left_valid, shift_prev, wkv_prev):
    b, t, d = x.shape
    m = b * t
    tm = _pick(m, (512, 256, 128))
    tm_big = _pick(m, (1024, 512, 256, 128))
    tf = 512
    x2 = x.reshape(m, d)
    att_w = lw["w_out"].shape[0] // 2
    heads = att_w // ATT_HEAD_DIM

    h = _ffn(x2, lw["norm_ffn1"], lw["ffn1_gate"], lw["ffn1_up"], lw["ffn1_down"], tm=tm_big, tf=tf)
    proj = _proj(h, lw["norm_mix"], lw["w_in"], tm=tm_big, tn=lw["w_in"].shape[1] // 5)
    proj3 = proj.reshape(b, t, -1)

    clen = _pick(t, (CHUNK, 16))
    tq = _pick(t, (2 * CHUNK, CHUNK, 16))
    group = _pick(t // tq, (4, 2, 1))
    att, k_keep = _attention(proj3, ctx_k, ctx_v, lw["q_norm"], lw["k_norm"], _band_bias_vec(lw["rel_bias"], tq),
                             heads=heads, tq=tq, chunk=min(CHUNK, tq), left_valid=left_valid, group=group)
    rw, wkv_new = _wkv(proj3, shift_prev, wkv_prev, lw["rwkv_mu"], lw["rwkv_w0"], lw["rwkv_w_lora"],
                       lw["rwkv_a0"], lw["rwkv_a_lora"], lw["rwkv_g_lora"], lw["rwkv_k_k"], lw["rwkv_k_a"],
                       lw["rwkv_r_k"], lw["rwkv_lnx_w"], lw["rwkv_lnx_b"], clen=clen, nseq=_pick(b, (2, 1)))

    h = _outproj(h, att.reshape(m, att_w), rw.reshape(m, -1), lw["w_out"], tm=tm)
    h = _ffn(h, lw["norm_ffn2"], lw["ffn2_gate"], lw["ffn2_up"], lw["ffn2_down"], tm=tm_big, tf=tf)
    h = _ple(h, lw["norm_ple"], lw["ple_gate"], p.reshape(m, -1), lw["ple_proj"], tm=tm)

    keep = k_keep.shape[1]
    k_keep = k_keep.reshape(b, keep, heads, ATT_HEAD_DIM)
    v_keep = proj3[:, t - keep:, 2 * att_w:3 * att_w].reshape(b, keep, heads, ATT_HEAD_DIM)
    return h.reshape(b, t, d), (k_keep, v_keep, wkv_new, proj3[:, -1:, 3 * att_w:])


def kernel(x_prompt, x_sample, cache_att_k, cache_att_v, state_wkv, state_shift, p_prompt, p_sample, norm_ffn1, ffn1_gate, ffn1_up, ffn1_down, norm_mix, w_in, q_norm, k_norm, rel_bias, rwkv_mu, rwkv_w0, rwkv_w_lora, rwkv_a0, rwkv_a_lora, rwkv_g_lora, rwkv_k_k, rwkv_k_a, rwkv_r_k, rwkv_lnx_w, rwkv_lnx_b, w_out, norm_ffn2, ffn2_gate, ffn2_up, ffn2_down, norm_ple, ple_gate, ple_proj):
    depth = norm_ffn1.shape[0]
    hp, hs = x_prompt, x_sample
    bp = hp.shape[0]
    bs = hs.shape[0]
    att_w = w_out.shape[1] // 2
    rwkv_heads = rwkv_r_k.shape[1]
    rwkv_pw = rwkv_mu.shape[1]
    prompt_states, sample_states = [], []
    for i in range(depth):
        lw = {
            "norm_ffn1": norm_ffn1[i], "ffn1_gate": ffn1_gate[i].astype(BF16), "ffn1_up": ffn1_up[i].astype(BF16),
            "ffn1_down": ffn1_down[i].astype(BF16), "norm_mix": norm_mix[i], "w_in": w_in[i].astype(BF16),
            "q_norm": q_norm[i], "k_norm": k_norm[i],
            "rel_bias": rel_bias[i], "rwkv_mu": rwkv_mu[i], "rwkv_w0": rwkv_w0[i],
            "rwkv_w_lora": rwkv_w_lora[i].astype(BF16), "rwkv_a0": rwkv_a0[i],
            "rwkv_a_lora": rwkv_a_lora[i].astype(BF16), "rwkv_g_lora": rwkv_g_lora[i].astype(BF16),
            "rwkv_k_k": rwkv_k_k[i], "rwkv_k_a": rwkv_k_a[i], "rwkv_r_k": rwkv_r_k[i].reshape(-1),
            "rwkv_lnx_w": rwkv_lnx_w[i], "rwkv_lnx_b": rwkv_lnx_b[i],
            "w_out": w_out[i].astype(BF16), "norm_ffn2": norm_ffn2[i], "ffn2_gate": ffn2_gate[i].astype(BF16),
            "ffn2_up": ffn2_up[i].astype(BF16), "ffn2_down": ffn2_down[i].astype(BF16),
            "norm_ple": norm_ple[i], "ple_gate": ple_gate[i].astype(BF16), "ple_proj": ple_proj[i].astype(BF16),
        }
        zero_ctx = jnp.zeros((1, LEFT, att_w // ATT_HEAD_DIM, ATT_HEAD_DIM), F32)
        zero_shift = jnp.zeros((bp, 1, rwkv_pw), hp.dtype)
        zero_wkv = jnp.zeros((bp, rwkv_heads, RWKV_HEAD_DIM, RWKV_HEAD_DIM), hp.dtype)
        hp, sp = _trunk_layer(hp, p_prompt[i], lw, zero_ctx, zero_ctx, False, zero_shift, zero_wkv)
        hs, ss = _trunk_layer(hs, p_sample[i], lw, cache_att_k[i], cache_att_v[i], True, state_shift[i],
                              state_wkv[i])
        prompt_states.append(sp)
        sample_states.append(ss)
    stack = lambda states, j: jnp.stack([s[j] for s in states])
    return (hp, hs, stack(prompt_states, 0), stack(prompt_states, 1), stack(prompt_states, 2),
            stack(prompt_states, 3), stack(sample_states, 0), stack(sample_states, 1),
            stack(sample_states, 2), stack(sample_states, 3))
```

```python
import functools
import math

import jax
import jax.numpy as jnp
from jax import lax
from jax.experimental import pallas as pl
from jax.experimental.pallas import tpu as pltpu

F32 = jnp.float32
BF16 = jnp.bfloat16

CHUNK = 64
LEFT = 512
ATT_HEAD_DIM = 128
REL_CLIP = 256
RWKV_HEAD_DIM = 64
W_LORA = 64
A_LORA = 64
G_LORA = 128
RMS_EPS = 1e-6
GN_EPS = 64e-5
NEG_INF = -1e30

LANES = 128
SUBLANES = 8
V7X_VMEM_BYTES = 64 * 1024 * 1024
VMEM_LIMIT_CAP = V7X_VMEM_BYTES - 8 * 1024 * 1024


def _vmem_limit(block_bytes, scratch_bytes, temp_bytes):
    need = 2 * block_bytes + scratch_bytes + temp_bytes + (4 << 20)
    return int(min(max(need, 16 << 20), VMEM_LIMIT_CAP))


def _nbytes(shape, dtype):
    return math.prod(shape) * jnp.dtype(dtype).itemsize


def _rms_norm_rows(x, gain):
    ms = jnp.mean(x * x, axis=-1, keepdims=True)
    return x * lax.rsqrt(ms + RMS_EPS) * gain


def _mm(a, b):
    return jnp.dot(a, b, preferred_element_type=F32)


def _mm_nt(a, b):
    return lax.dot_general(a, b, (((1,), (1,)), ((), ())), preferred_element_type=F32)


def _mm_tn(a, b):
    return lax.dot_general(a, b, (((0,), (0,)), ((), ())), preferred_element_type=F32)


def _ffn_body(x_ref, g_ref, wg_ref, wu_ref, wd_ref, o_ref, xn_ref):
    j = pl.program_id(1)
    last = pl.num_programs(1) - 1

    @pl.when(j == 0)
    def _():
        xn_ref[...] = _rms_norm_rows(x_ref[...], g_ref[...]).astype(BF16)
        o_ref[...] = jnp.zeros_like(o_ref)

    xn = xn_ref[...]
    a = _mm(xn, wg_ref[...])
    u = _mm(xn, wu_ref[...])
    h = (a * jax.nn.sigmoid(a) * u).astype(BF16)
    o_ref[...] += _mm(h, wd_ref[...])

    @pl.when(j == last)
    def _():
        o_ref[...] = x_ref[...] + 0.5 * o_ref[...]


def _ffn(x, gain, wg, wu, wd, *, tm, tf):
    m, d = x.shape
    f = wg.shape[1]
    assert m % tm == 0 and f % tf == 0
    blocks = (_nbytes((tm, d), F32) * 2 + _nbytes((d, tf), BF16) * 3)
    temps = _nbytes((tm, tf), F32) * 4
    return pl.pallas_call(
        _ffn_body,
        grid=(m // tm, f // tf),
        in_specs=[
            pl.BlockSpec((tm, d), lambda i, j: (i, 0)),
            pl.BlockSpec((1, d), lambda i, j: (0, 0)),
            pl.BlockSpec((d, tf), lambda i, j: (0, j)),
            pl.BlockSpec((d, tf), lambda i, j: (0, j)),
            pl.BlockSpec((tf, d), lambda i, j: (j, 0)),
        ],
        out_specs=pl.BlockSpec((tm, d), lambda i, j: (i, 0)),
        out_shape=jax.ShapeDtypeStruct((m, d), F32),
        scratch_shapes=[pltpu.VMEM((tm, d), BF16)],
        compiler_params=pltpu.CompilerParams(
            dimension_semantics=("parallel", "arbitrary"),
            vmem_limit_bytes=_vmem_limit(blocks, _nbytes((tm, d), BF16), temps)),
        name="ffn",
    )(x, gain.reshape(1, d), wg, wu, wd)


def _proj_body(x_ref, g_ref, w_ref, o_ref, xn_ref):
    @pl.when(pl.program_id(1) == 0)
    def _():
        xn_ref[...] = _rms_norm_rows(x_ref[...], g_ref[...]).astype(BF16)

    o_ref[...] = _mm(xn_ref[...], w_ref[...])


def _proj(x, gain, w, *, tm, tn):
    m, d = x.shape
    n = w.shape[1]
    assert m % tm == 0 and n % tn == 0
    blocks = _nbytes((tm, d), F32) + _nbytes((d, tn), BF16) + _nbytes((tm, tn), F32)
    return pl.pallas_call(
        _proj_body,
        grid=(m // tm, n // tn),
        in_specs=[
            pl.BlockSpec((tm, d), lambda i, j: (i, 0)),
            pl.BlockSpec((1, d), lambda i, j: (0, 0)),
            pl.BlockSpec((d, tn), lambda i, j: (0, j)),
        ],
        out_specs=pl.BlockSpec((tm, tn), lambda i, j: (i, j)),
        out_shape=jax.ShapeDtypeStruct((m, n), F32),
        scratch_shapes=[pltpu.VMEM((tm, d), BF16)],
        compiler_params=pltpu.CompilerParams(
            dimension_semantics=("parallel", "arbitrary"),
            vmem_limit_bytes=_vmem_limit(blocks, _nbytes((tm, d), BF16), 2 * _nbytes((tm, tn), F32))),
        name="proj",
    )(x, gain.reshape(1, d), w)


def _attn_body(q_ref, k_ref, v_ref, ck_ref, cv_ref, qg_ref, kg_ref, bvec_ref, o_ref, kn_ref,
               qn_ref, kp_ref, vp_ref, bias_ref, *, seq, tq, chunk, left_valid, group):
    keep = kn_ref.shape[1]
    qn_ref[...] = _rms_norm_rows(q_ref[0], qg_ref[...]).astype(BF16)
    kn = _rms_norm_rows(k_ref[0], kg_ref[...])
    kn_ref[0] = kn[seq - keep:, :]
    if left_valid:
        heads = ck_ref.shape[1] // LEFT
        mine = pl.ds(pl.program_id(1), LEFT, stride=heads)
        kp_ref[0:LEFT, :] = ck_ref[0, mine, :].astype(BF16)
        vp_ref[0:LEFT, :] = cv_ref[0, mine, :].astype(BF16)
    else:
        kp_ref[0:LEFT, :] = jnp.zeros((LEFT, ATT_HEAD_DIM), BF16)
        vp_ref[0:LEFT, :] = jnp.zeros((LEFT, ATT_HEAD_DIM), BF16)
    kp_ref[LEFT:LEFT + seq, :] = kn.astype(BF16)
    vp_ref[LEFT:LEFT + seq, :] = v_ref[0].astype(BF16)

    width = LEFT + tq
    bw = bvec_ref.shape[-1]
    bias = pltpu.roll(jnp.broadcast_to(bvec_ref[0], (tq, bw)), 0, 1, stride=1, stride_axis=0)[:, :width]
    if tq > chunk:
        row = lax.broadcasted_iota(jnp.int32, (tq, width), 0)
        col = lax.broadcasted_iota(jnp.int32, (tq, width), 1)
        chunk_start = (row // chunk) * chunk
        in_band = (col >= chunk_start) & (col < chunk_start + LEFT + chunk)
        bias = jnp.where(in_band, bias, NEG_INF)
    bias_ref[...] = bias
    scale = ATT_HEAD_DIM ** -0.5

    n_chunks = tq // chunk
    spans = []
    for r in range(n_chunks):
        lo = (r * chunk) // LANES * LANES
        hi = min(width, -(-(r * chunk + LEFT + chunk) // LANES) * LANES)
        spans.append((lo, hi))

    def block_bases(t):
        return [pl.multiple_of((t * group + u) * tq, tq) for u in range(group)]

    def blocks(t, mask_left):
        bases = block_bases(t)
        scores = [_mm_nt(qn_ref[pl.ds(base, tq), :], kp_ref[pl.ds(base, width), :]) for base in bases]
        pieces = [(u, r) for u in range(group) for r in range(n_chunks)]
        masked_scores = []
        for u, r in pieces:
            lo, hi = spans[r]
            rs = slice(r * chunk, (r + 1) * chunk)
            tiles = []
            for c0 in range(lo, hi, LANES):
                c1 = min(c0 + LANES, hi)
                band = bias_ref[rs, c0:c1]
                s = scores[u][rs, c0:c1] * scale + band
                inside = r * chunk <= c0 and c1 <= r * chunk + LEFT + chunk
                if mask_left or not inside:
                    ok = band > 0.5 * NEG_INF
                    if mask_left:
                        col = lax.broadcasted_iota(jnp.int32, (chunk, c1 - c0), 1) + c0
                        ok = ok & (col >= LEFT - bases[u])
                    s = jnp.where(ok, s, NEG_INF)
                tiles.append(s)
            masked_scores.append(jnp.concatenate(tiles, axis=1) if len(tiles) > 1 else tiles[0])
        maxes = [jnp.max(s, axis=-1, keepdims=True) for s in masked_scores]
        exps = [jnp.exp(s - m) for s, m in zip(masked_scores, maxes)]
        invs = [1.0 / jnp.sum(e, axis=-1, keepdims=True) for e in exps]
        rows = []
        for (u, r), e in zip(pieces, exps):
            lo, hi = spans[r]
            parts = ([jnp.zeros((chunk, lo), BF16)] if lo else []) + [e.astype(BF16)]
            parts += [jnp.zeros((chunk, width - hi), BF16)] if hi < width else []
            rows.append(jnp.concatenate(parts, axis=1) if len(parts) > 1 else parts[0])
        for u, base in enumerate(bases):
            mine = slice(u * n_chunks, (u + 1) * n_chunks)
            w = jnp.concatenate(rows[mine], axis=0) if n_chunks > 1 else rows[mine][0]
            inv = jnp.concatenate(invs[mine], axis=0) if n_chunks > 1 else invs[mine][0]
            o_ref[0, pl.ds(base, tq), :] = (_mm(w, vp_ref[pl.ds(base, width), :]) * inv).astype(o_ref.dtype)

    n_iter = seq // (tq * group)
    n_left = 0 if left_valid else min(n_iter, -(-LEFT // (tq * group)))

    def masked(t, carry):
        blocks(t, True)
        return carry

    def plain(t, carry):
        blocks(t, False)
        return carry

    lax.fori_loop(0, n_left, masked, 0)
    lax.fori_loop(n_left, n_iter, plain, 0)


def _attention(proj, ctx_k, ctx_v, q_gain, k_gain, bias_vec, *, heads, tq, chunk, left_valid, group):
    b, seq, _ = proj.shape
    dh = ATT_HEAD_DIM
    bw = bias_vec.shape[-1]
    keep = min(seq, LEFT)
    assert seq % (tq * group) == 0 and tq % chunk == 0 and ctx_k.shape[1:] == (LEFT, heads, dh)
    per_batch_ctx = ctx_k.shape[0] == b
    ctx_map = (lambda bi, h: (bi, 0, 0)) if per_batch_ctx else (lambda bi, h: (0, 0, 0))
    ctx_k, ctx_v = (c.reshape(c.shape[0], LEFT * heads, dh) for c in (ctx_k, ctx_v))
    blocks = (4 * _nbytes((seq, dh), F32) + 2 * _nbytes((LEFT, heads, dh), F32) + _nbytes((8, bw), F32)
              + _nbytes((keep, dh), F32))
    scratch = (_nbytes((seq, dh), BF16) + 2 * _nbytes((LEFT + seq, dh), BF16)
               + _nbytes((tq, LEFT + tq), F32))
    return pl.pallas_call(
        functools.partial(_attn_body, seq=seq, tq=tq, chunk=chunk, left_valid=left_valid, group=group),
        grid=(b, heads),
        in_specs=[
            pl.BlockSpec((1, seq, dh), lambda bi, h: (bi, 0, h)),
            pl.BlockSpec((1, seq, dh), lambda bi, h: (bi, 0, heads + h)),
            pl.BlockSpec((1, seq, dh), lambda bi, h: (bi, 0, 2 * heads + h)),
            pl.BlockSpec((1, LEFT * heads, dh), ctx_map),
            pl.BlockSpec((1, LEFT * heads, dh), ctx_map),
            pl.BlockSpec((1, dh), lambda bi, h: (0, 0)),
            pl.BlockSpec((1, dh), lambda bi, h: (0, 0)),
            pl.BlockSpec((1, 1, bw), lambda bi, h: (h, 0, 0)),
        ],
        out_specs=[
            pl.BlockSpec((1, seq, dh), lambda bi, h: (bi, 0, h)),
            pl.BlockSpec((1, keep, dh), lambda bi, h: (bi, 0, h)),
        ],
        out_shape=[jax.ShapeDtypeStruct((b, seq, heads * dh), BF16),
                   jax.ShapeDtypeStruct((b, keep, heads * dh), F32)],
        scratch_shapes=[pltpu.VMEM((seq, dh), BF16), pltpu.VMEM((LEFT + seq, dh), BF16),
                        pltpu.VMEM((LEFT + seq, dh), BF16), pltpu.VMEM((tq, LEFT + tq), F32)],
        compiler_params=pltpu.CompilerParams(
            dimension_semantics=("parallel", "parallel"),
            vmem_limit_bytes=_vmem_limit(blocks, scratch, 8 * group * _nbytes((tq, LEFT + tq), F32)
                                         + 3 * _nbytes((seq, dh), F32))),
        name="attn",
    )(proj, proj, proj, ctx_k, ctx_v, q_gain.reshape(1, dh), k_gain.reshape(1, dh), bias_vec)


def _band_bias_vec(rel_bias, tq):
    width = LEFT + tq
    bw = -(-(width + tq) // LANES) * LANES
    m = jnp.arange(bw)
    d = jnp.where(m < width, m, m - bw)
    idx = jnp.clip(LEFT - d, -REL_CLIP, REL_CLIP) + REL_CLIP
    return rel_bias[idx].T.astype(F32)[:, None, :]


def _split3(x):
    h1 = x.astype(BF16)
    r1 = x - h1.astype(F32)
    h2 = r1.astype(BF16)
    h3 = (r1 - h2.astype(F32)).astype(BF16)
    return h1, h2, h3


def _wkv_body(x_ref, sp_ref, s0_ref, mu_ref, w0_ref, wl_ref, a0_ref, al_ref, gl_ref, kk_ref, ka_ref,
              rk_ref, lnw_ref, lnb_ref, o_ref, so_ref, carry_ref, s_ref, *, clen, heads, col0, nseq):
    c = pl.program_id(1)
    n = RWKV_HEAD_DIM
    width = heads * n
    pairs = heads // 2
    pw = 2 * n
    tw = 2 * clen
    bf = lambda t: t.astype(BF16)

    @pl.when(c == 0)
    def _():
        zeros = jnp.zeros((n, n), F32)
        for i in range(nseq):
            carry_ref[i] = sp_ref[i]
            for q in range(pairs):
                s_ref[i * pairs + q] = jnp.concatenate(
                    [jnp.concatenate([s0_ref[i, 2 * q], zeros], axis=1),
                     jnp.concatenate([zeros, s0_ref[i, 2 * q + 1]], axis=1)], axis=0)

    trow = lax.broadcasted_iota(jnp.int32, (SUBLANES, 1), 0)
    ti = lax.broadcasted_iota(jnp.int32, (clen, clen), 0)
    si = lax.broadcasted_iota(jnp.int32, (clen, clen), 1)
    tri = (si <= ti).astype(BF16)
    levels = max(1, (clen - 1).bit_length())

    def prepare(i):
        xb = x_ref[i, :, col0:]
        rolled = pltpu.roll(xb, 1, 0)
        shifted = jnp.concatenate([jnp.where(trow == 0, carry_ref[i], rolled[:SUBLANES]), rolled[SUBLANES:]],
                                  axis=0)
        carry_ref[i] = xb[clen - 1:clen, :]
        xm = xb + mu_ref[...] * (shifted - xb)

        r = xm[:, 0:width]
        k = xm[:, width:2 * width]
        v = xm[:, 2 * width:3 * width]
        o3 = 3 * width
        w_lo = xm[:, o3:o3 + W_LORA]
        a_lo = xm[:, o3 + W_LORA:o3 + W_LORA + A_LORA]
        g_lo = xm[:, o3 + W_LORA + A_LORA:o3 + W_LORA + A_LORA + G_LORA]

        log_decay = -math.exp(-0.5) * jax.nn.sigmoid(w0_ref[...] + _mm(bf(jnp.tanh(w_lo)), wl_ref[...]))
        a = jax.nn.sigmoid(a0_ref[...] + _mm(bf(a_lo), al_ref[...]))
        g = _mm(bf(jax.nn.sigmoid(g_lo)), gl_ref[...])
        kk = k * kk_ref[...]
        k = k * (1.0 + (a - 1.0) * ka_ref[...])

        d1, d2, d3 = _split3(log_decay)
        cum = _mm(tri, d1) + _mm(tri, d2) + _mm(tri, d3)
        g_in = jnp.exp(cum)
        g_inv = jnp.exp(-cum)
        g_last = g_in[clen - 1:clen, :]
        k_h = k * g_inv
        return dict(v=v, a=a, g=g, kk=kk, rk=r * k * rk_ref[...], g_inv=g_inv, g_ex=jnp.exp(cum - log_decay),
                    g_last=g_last, r_t=r * g_in, k_h=k_h, k_l=k_h * g_last)

    pre = [prepare(i) for i in range(nseq)]

    first = lax.broadcasted_iota(jnp.int32, (1, pw), 1) < n
    first_t = lax.broadcasted_iota(jnp.int32, (1, tw), 1) < clen
    row2 = lax.broadcasted_iota(jnp.int32, (2 * clen, 2 * tw), 0)
    spos = lax.broadcasted_iota(jnp.int32, (2 * clen, 2 * tw), 1) % clen
    causal = spos < jnp.where(row2 < clen, row2, row2 - clen + 1)
    same_head = ((lax.broadcasted_iota(jnp.int32, (pw, pw), 0) < n)
                 == (lax.broadcasted_iota(jnp.int32, (pw, pw), 1) < n))

    def head_sums(x):
        s0 = jnp.sum(jnp.where(first, x, 0.0), axis=-1, keepdims=True)
        s1 = jnp.sum(jnp.where(first, 0.0, x), axis=-1, keepdims=True)
        return jnp.where(first, s0, s1)

    def block_diag(x, mask):
        zero = jnp.zeros_like(x)
        return jnp.concatenate([jnp.where(mask, x, zero), jnp.where(mask, zero, x)], axis=0)

    units = [(i, q) for i in range(nseq) for q in range(pairs)]
    us = range(len(units))
    cols = [slice(q * pw, (q + 1) * pw) for _, q in units]
    val = lambda name, u: pre[units[u][0]][name][:, cols[u]]

    b_hat, a_t, v_bd, s_prev, s_bf = [], [], [], [], []
    for u in us:
        kk_u = val("kk", u)
        kk_u = kk_u * jnp.minimum(lax.rsqrt(head_sums(kk_u * kk_u)), 1e12)
        b_hat.append(kk_u * val("a", u) * val("g_inv", u))
        a_t.append(-kk_u * val("g_ex", u))
        v_bd.append(block_diag(bf(val("v", u)), first))
        s_prev.append(s_ref[u])
        s_bf.append(bf(s_prev[u]))

    ar = [bf(jnp.concatenate([a_t[u], val("r_t", u)], axis=0)) for u in us]
    p = [_mm_nt(ar[u], jnp.concatenate([block_diag(bf(b_hat[u]), first),
                                        block_diag(bf(val("k_h", u)), first)], axis=0)) for u in us]
    pm = [bf(jnp.where(causal, p[u], 0.0)) for u in us]
    nk = [pm[u][:clen, :tw] for u in us]
    a_rb = [pm[u][clen:, :tw] for u in us]

    sv = [_mm_nt(ar[u], s_bf[u]) + _mm(pm[u][:, tw:], v_bd[u]) for u in us]
    sa = [sv[u][:clen] for u in us]
    for lvl in range(levels):
        if lvl + 1 < levels:
            both = [_mm(nk[u], jnp.concatenate([block_diag(bf(sa[u]), first), block_diag(nk[u], first_t)],
                                               axis=1)) for u in us]
            sa = [sa[u] + both[u][:, :pw] for u in us]
            nk = [bf(both[u][:, pw:]) for u in us]
        else:
            sa = [sa[u] + _mm(nk[u], block_diag(bf(sa[u]), first)) for u in us]

    sa_bf = [bf(sa[u]) for u in us]
    ys = [sv[u][clen:] + _mm(a_rb[u], block_diag(sa_bf[u], first)) for u in us]
    for u in us:
        g_last = val("g_last", u)
        upd = _mm_tn(jnp.concatenate([sa_bf[u], bf(val("v", u))], axis=0),
                     jnp.concatenate([bf(b_hat[u] * g_last), bf(val("k_l", u))], axis=0))
        s_ref[u] = s_prev[u] * g_last + jnp.where(same_head, upd, 0.0)

    outs = []
    inv_n = 1.0 / n
    for u in us:
        y = ys[u]
        mean = head_sums(y) * inv_n
        var = head_sums(jnp.square(y - mean)) * inv_n
        y = (y - mean) * lax.rsqrt(var + GN_EPS)
        y = y * lnw_ref[:, cols[u]] + lnb_ref[:, cols[u]]
        y = y + head_sums(val("rk", u)) * val("v", u)
        outs.append(y)

    for i in range(nseq):
        o_ref[i] = (jnp.concatenate(outs[i * pairs:(i + 1) * pairs], axis=1) * pre[i]["g"]).astype(o_ref.dtype)

    @pl.when(c == pl.num_programs(1) - 1)
    def _():
        for i in range(nseq):
            for q in range(pairs):
                so_ref[i, 2 * q] = s_ref[i * pairs + q, 0:n, 0:n]
                so_ref[i, 2 * q + 1] = s_ref[i * pairs + q, n:pw, n:pw]


def _wkv(proj, shift_prev, state0, mu, w0, w_lora, a0, a_lora, g_lora, k_k, k_a, r_k, lnx_w, lnx_b, *, clen,
         nseq):
    b, t, full_w = proj.shape
    heads = state0.shape[1]
    n = RWKV_HEAD_DIM
    width = heads * n
    pw = shift_prev.shape[-1]
    col0 = full_w - pw
    assert t % clen == 0 and pw == 3 * width + W_LORA + A_LORA + G_LORA and heads % 2 == 0
    assert col0 % LANES == 0 and 2 * n == LANES and b % nseq == 0
    row = lambda a: a.reshape(1, -1)
    full = lambda shape: pl.BlockSpec(shape, lambda bi, c: (0,) * len(shape))
    blocks = nseq * (_nbytes((clen, full_w), F32) + _nbytes((clen, width), F32)
                     + 2 * _nbytes((heads, n, LANES), F32))
    state_scratch = _nbytes((nseq * heads // 2, 2 * n, 2 * n), F32)
    return pl.pallas_call(
        functools.partial(_wkv_body, clen=clen, heads=heads, col0=col0, nseq=nseq),
        grid=(b // nseq, t // clen),
        in_specs=[
            pl.BlockSpec((nseq, clen, full_w), lambda bi, c: (bi, c, 0)),
            pl.BlockSpec((nseq, 1, pw), lambda bi, c: (bi, 0, 0)),
            pl.BlockSpec((nseq, heads, n, n), lambda bi, c: (bi, 0, 0, 0)),
            full((1, pw)), full((1, width)), full((W_LORA, width)), full((1, width)), full((A_LORA, width)),
            full((G_LORA, width)), full((1, width)), full((1, width)), full((1, width)), full((1, width)),
            full((1, width)),
        ],
        out_specs=[
            pl.BlockSpec((nseq, clen, width), lambda bi, c: (bi, c, 0)),
            pl.BlockSpec((nseq, heads, n, n), lambda bi, c: (bi, 0, 0, 0)),
        ],
        out_shape=[jax.ShapeDtypeStruct((b, t, width), BF16),
                   jax.ShapeDtypeStruct((b, heads, n, n), F32)],
        scratch_shapes=[pltpu.VMEM((nseq, 1, pw), F32), pltpu.VMEM((nseq * heads // 2, 2 * n, 2 * n), F32)],
        compiler_params=pltpu.CompilerParams(
            dimension_semantics=("parallel", "arbitrary"),
            vmem_limit_bytes=_vmem_limit(blocks, state_scratch, 40 * nseq * _nbytes((clen, width), F32))),
        name="wkv",
    )(proj, shift_prev, state0, row(mu), row(w0), w_lora, row(a0), a_lora, g_lora, row(k_k), row(k_a),
      row(r_k), row(lnx_w), row(lnx_b))


def _outproj_body(h_ref, att_ref, rw_ref, wa_ref, wr_ref, o_ref):
    o_ref[...] = h_ref[...] + _mm(att_ref[...], wa_ref[...]) + _mm(rw_ref[...], wr_ref[...])


def _outproj(h, att, rw, w_out, *, tm):
    m, d = h.shape
    ka, kr = att.shape[1], rw.shape[1]
    assert m % tm == 0 and ka == kr and w_out.shape == (ka + kr, d)
    blocks = (2 * _nbytes((tm, d), F32) + _nbytes((tm, ka), BF16) + _nbytes((tm, kr), BF16)
              + _nbytes((ka, d), BF16) + _nbytes((kr, d), BF16))
    return pl.pallas_call(
        _outproj_body,
        grid=(m // tm,),
        in_specs=[
            pl.BlockSpec((tm, d), lambda i: (i, 0)),
            pl.BlockSpec((tm, ka), lambda i: (i, 0)),
            pl.BlockSpec((tm, kr), lambda i: (i, 0)),
            pl.BlockSpec((ka, d), lambda i: (0, 0)),
            pl.BlockSpec((kr, d), lambda i: (1, 0)),
        ],
        out_specs=pl.BlockSpec((tm, d), lambda i: (i, 0)),
        out_shape=jax.ShapeDtypeStruct((m, d), F32),
        compiler_params=pltpu.CompilerParams(
            dimension_semantics=("parallel",),
            vmem_limit_bytes=_vmem_limit(blocks, 0, 2 * _nbytes((tm, d), F32))),
        name="outproj",
    )(h, att, rw, w_out, w_out)


def _ple_body(x_ref, g_ref, wg_ref, p_ref, wp_ref, o_ref):
    x = x_ref[...]
    gate = jax.nn.sigmoid(_mm(_rms_norm_rows(x, g_ref[...]).astype(BF16), wg_ref[...]))
    o_ref[...] = x + gate * _mm(p_ref[...].astype(BF16), wp_ref[...])


def _ple(x, gain, w_gate, p, w_proj, *, tm):
    m, d = x.shape
    pd = p.shape[1]
    assert m % tm == 0
    blocks = (2 * _nbytes((tm, d), F32) + _nbytes((d, d), BF16) + _nbytes((tm, pd), F32)
              + _nbytes((pd, d), BF16))
    return pl.pallas_call(
        _ple_body,
        grid=(m // tm,),
        in_specs=[
            pl.BlockSpec((tm, d), lambda i: (i, 0)),
            pl.BlockSpec((1, d), lambda i: (0, 0)),
            pl.BlockSpec((d, d), lambda i: (0, 0)),
            pl.BlockSpec((tm, pd), lambda i: (i, 0)),
            pl.BlockSpec((pd, d), lambda i: (0, 0)),
        ],
        out_specs=pl.BlockSpec((tm, d), lambda i: (i, 0)),
        out_shape=jax.ShapeDtypeStruct((m, d), F32),
        compiler_params=pltpu.CompilerParams(
            dimension_semantics=("parallel",),
            vmem_limit_bytes=_vmem_limit(blocks, 0, 4 * _nbytes((tm, d), F32))),
        name="ple",
    )(x, gain.reshape(1, d), w_gate, p, w_proj)


def _pick(m, candidates):
    for c in candidates:
        if m % c == 0:
            return c
    raise ValueError(f"no tile for {m}")


def _trunk_layer(x, p, lw, ctx_k, ctx_v, left_valid, shift_prev, wkv_prev):
    b, t, d = x.shape
    m = b * t
    tm = _pick(m, (512, 256, 128))
    tm_big = _pick(m, (1024, 512, 256, 128))
    tf = 512
    x2 = x.reshape(m, d)
    att_w = lw["w_out"].shape[0] // 2
    heads = att_w // ATT_HEAD_DIM

    h = _ffn(x2, lw["norm_ffn1"], lw["ffn1_gate"], lw["ffn1_up"], lw["ffn1_down"], tm=tm_big, tf=tf)
    proj = _proj(h, lw["norm_mix"], lw["w_in"], tm=tm_big, tn=lw["w_in"].shape[1] // 5)
    proj3 = proj.reshape(b, t, -1)

    clen = _pick(t, (CHUNK, 16))
    tq = _pick(t, (2 * CHUNK, CHUNK, 16))
    group = _pick(t // tq, (4, 2, 1))
    att, k_keep = _attention(proj3, ctx_k, ctx_v, lw["q_norm"], lw["k_norm"], _band_bias_vec(lw["rel_bias"], tq),
                             heads=heads, tq=tq, chunk=min(CHUNK, tq), left_valid=left_valid, group=group)
    rw, wkv_new = _wkv(proj3, shift_prev, wkv_prev, lw["rwkv_mu"], lw["rwkv_w0"], lw["rwkv_w_lora"],
                       lw["rwkv_a0"], lw["rwkv_a_lora"], lw["rwkv_g_lora"], lw["rwkv_k_k"], lw["rwkv_k_a"],
                       lw["rwkv_r_k"], lw["rwkv_lnx_w"], lw["rwkv_lnx_b"], clen=clen, nseq=_pick(b, (2, 1)))

    h = _outproj(h, att.reshape(m, att_w), rw.reshape(m, -1), lw["w_out"], tm=tm)
    h = _ffn(h, lw["norm_ffn2"], lw["ffn2_gate"], lw["ffn2_up"], lw["ffn2_down"], tm=tm_big, tf=tf)
    h = _ple(h, lw["norm_ple"], lw["ple_gate"], p.reshape(m, -1), lw["ple_proj"], tm=tm)

    keep = k_keep.shape[1]
    k_keep = k_keep.reshape(b, keep, heads, ATT_HEAD_DIM)
    v_keep = proj3[:, t - keep:, 2 * att_w:3 * att_w].reshape(b, keep, heads, ATT_HEAD_DIM)
    return h.reshape(b, t, d), (k_keep, v_keep, wkv_new, proj3[:, -1:, 3 * att_w:])


def kernel(x_prompt, x_sample, cache_att_k, cache_att_v, state_wkv, state_shift, p_prompt, p_sample, norm_ffn1, ffn1_gate, ffn1_up, ffn1_down, norm_mix, w_in, q_norm, k_norm, rel_bias, rwkv_mu, rwkv_w0, rwkv_w_lora, rwkv_a0, rwkv_a_lora, rwkv_g_lora, rwkv_k_k, rwkv_k_a, rwkv_r_k, rwkv_lnx_w, rwkv_lnx_b, w_out, norm_ffn2, ffn2_gate, ffn2_up, ffn2_down, norm_ple, ple_gate, ple_proj):
    depth = norm_ffn1.shape[0]
    hp, hs = x_prompt, x_sample
    bp = hp.shape[0]
    att_w = w_out.shape[1] // 2
    rwkv_heads = rwkv_r_k.shape[1]
    rwkv_pw = rwkv_mu.shape[1]
    prompt_states, sample_states = [], []
    for i in range(depth):
        lw = {
            "norm_ffn1": norm_ffn1[i], "ffn1_gate": ffn1_gate[i].astype(BF16), "ffn1_up": ffn1_up[i].astype(BF16),
            "ffn1_down": ffn1_down[i].astype(BF16), "norm_mix": norm_mix[i], "w_in": w_in[i].astype(BF16),
            "q_norm": q_norm[i], "k_norm": k_norm[i],
            "rel_bias": rel_bias[i], "rwkv_mu": rwkv_mu[i], "rwkv_w0": rwkv_w0[i],
            "rwkv_w_lora": rwkv_w_lora[i].astype(BF16), "rwkv_a0": rwkv_a0[i],
            "rwkv_a_lora": rwkv_a_lora[i].astype(BF16), "rwkv_g_lora": rwkv_g_lora[i].astype(BF16),
            "rwkv_k_k": rwkv_k_k[i], "rwkv_k_a": rwkv_k_a[i], "rwkv_r_k": rwkv_r_k[i].reshape(-1),
            "rwkv_lnx_w": rwkv_lnx_w[i], "rwkv_lnx_b": rwkv_lnx_b[i],
            "w_out": w_out[i].astype(BF16), "norm_ffn2": norm_ffn2[i], "ffn2_gate": ffn2_gate[i].astype(BF16),
            "ffn2_up": ffn2_up[i].astype(BF16), "ffn2_down": ffn2_down[i].astype(BF16),
            "norm_ple": norm_ple[i], "ple_gate": ple_gate[i].astype(BF16), "ple_proj": ple_proj[i].astype(BF16),
        }
        zero_ctx = jnp.zeros((1, LEFT, att_w // ATT_HEAD_DIM, ATT_HEAD_DIM), F32)
        zero_shift = jnp.zeros((bp, 1, rwkv_pw), hp.dtype)
        zero_wkv = jnp.zeros((bp, rwkv_heads, RWKV_HEAD_DIM, RWKV_HEAD_DIM), hp.dtype)
        hp, sp = _trunk_layer(hp, p_prompt[i], lw, zero_ctx, zero_ctx, False, zero_shift, zero_wkv)
        hs, ss = _trunk_layer(hs, p_sample[i], lw, cache_att_k[i], cache_att_v[i], True, state_shift[i],
                              state_wkv[i])
        prompt_states.append(sp)
        sample_states.append(ss)
    stack = lambda states, j: jnp.stack([s[j] for s in states])
    return (hp, hs, stack(prompt_states, 0), stack(prompt_states, 1), stack(prompt_states, 2),
            stack(prompt_states, 3), stack(sample_states, 0), stack(sample_states, 1),
            stack(sample_states, 2), stack(sample_states, 3))
```

```python
import functools
import math

import jax
import jax.numpy as jnp
from jax import lax
from jax.experimental import pallas as pl
from jax.experimental.pallas import tpu as pltpu

F32 = jnp.float32
BF16 = jnp.bfloat16

CHUNK = 64
LEFT = 512
ATT_HEAD_DIM = 128
REL_CLIP = 256
RWKV_HEAD_DIM = 64
W_LORA = 64
A_LORA = 64
G_LORA = 128
RMS_EPS = 1e-6
GN_EPS = 64e-5
NEG_INF = -1e30

LANES = 128
SUBLANES = 8
V7X_VMEM_BYTES = 64 * 1024 * 1024
VMEM_LIMIT_CAP = V7X_VMEM_BYTES - 8 * 1024 * 1024


def _vmem_limit(block_bytes, scratch_bytes, temp_bytes):
    need = 2 * block_bytes + scratch_bytes + temp_bytes + (4 << 20)
    return int(min(max(need, 16 << 20), VMEM_LIMIT_CAP))


def _nbytes(shape, dtype):
    return math.prod(shape) * jnp.dtype(dtype).itemsize


def _rms_norm_rows(x, gain):
    ms = jnp.mean(x * x, axis=-1, keepdims=True)
    return x * lax.rsqrt(ms + RMS_EPS) * gain


def _mm(a, b):
    return jnp.dot(a, b, preferred_element_type=F32)


def _mm_nt(a, b):
    return lax.dot_general(a, b, (((1,), (1,)), ((), ())), preferred_element_type=F32)


def _mm_tn(a, b):
    return lax.dot_general(a, b, (((0,), (0,)), ((), ())), preferred_element_type=F32)


def _ffn_body(x_ref, g_ref, wg_ref, wu_ref, wd_ref, o_ref, xn_ref):
    j = pl.program_id(1)
    last = pl.num_programs(1) - 1

    @pl.when(j == 0)
    def _():
        xn_ref[...] = _rms_norm_rows(x_ref[...], g_ref[...]).astype(BF16)
        o_ref[...] = jnp.zeros_like(o_ref)

    xn = xn_ref[...]
    half = wg_ref.shape[1] // 2
    cols = (slice(0, half), slice(half, 2 * half))
    au = [(_mm(xn, wg_ref[:, c]), _mm(xn, wu_ref[:, c])) for c in cols]
    hs = [(a * jax.nn.sigmoid(a) * u).astype(BF16) for a, u in au]
    o_ref[...] += _mm(hs[0], wd_ref[cols[0], :]) + _mm(hs[1], wd_ref[cols[1], :])

    @pl.when(j == last)
    def _():
        o_ref[...] = x_ref[...] + 0.5 * o_ref[...]


def _ffn(x, gain, wg, wu, wd, *, tm, tf):
    m, d = x.shape
    f = wg.shape[1]
    assert m % tm == 0 and f % tf == 0
    blocks = (_nbytes((tm, d), F32) * 2 + _nbytes((d, tf), BF16) * 3)
    temps = _nbytes((tm, tf), F32) * 4
    return pl.pallas_call(
        _ffn_body,
        grid=(m // tm, f // tf),
        in_specs=[
            pl.BlockSpec((tm, d), lambda i, j: (i, 0)),
            pl.BlockSpec((1, d), lambda i, j: (0, 0)),
            pl.BlockSpec((d, tf), lambda i, j: (0, j)),
            pl.BlockSpec((d, tf), lambda i, j: (0, j)),
            pl.BlockSpec((tf, d), lambda i, j: (j, 0)),
        ],
        out_specs=pl.BlockSpec((tm, d), lambda i, j: (i, 0)),
        out_shape=jax.ShapeDtypeStruct((m, d), F32),
        scratch_shapes=[pltpu.VMEM((tm, d), BF16)],
        compiler_params=pltpu.CompilerParams(
            dimension_semantics=("parallel", "arbitrary"),
            vmem_limit_bytes=_vmem_limit(blocks, _nbytes((tm, d), BF16), temps)),
        name="ffn",
    )(x, gain.reshape(1, d), wg, wu, wd)


def _proj_body(x_ref, g_ref, w_ref, o_ref, xn_ref):
    @pl.when(pl.program_id(1) == 0)
    def _():
        xn_ref[...] = _rms_norm_rows(x_ref[...], g_ref[...]).astype(BF16)

    o_ref[...] = _mm(xn_ref[...], w_ref[...])


def _proj(x, gain, w, *, tm, tn):
    m, d = x.shape
    n = w.shape[1]
    assert m % tm == 0 and n % tn == 0
    blocks = _nbytes((tm, d), F32) + _nbytes((d, tn), BF16) + _nbytes((tm, tn), F32)
    return pl.pallas_call(
        _proj_body,
        grid=(m // tm, n // tn),
        in_specs=[
            pl.BlockSpec((tm, d), lambda i, j: (i, 0)),
            pl.BlockSpec((1, d), lambda i, j: (0, 0)),
            pl.BlockSpec((d, tn), lambda i, j: (0, j)),
        ],
        out_specs=pl.BlockSpec((tm, tn), lambda i, j: (i, j)),
        out_shape=jax.ShapeDtypeStruct((m, n), F32),
        scratch_shapes=[pltpu.VMEM((tm, d), BF16)],
        compiler_params=pltpu.CompilerParams(
            dimension_semantics=("parallel", "arbitrary"),
            vmem_limit_bytes=_vmem_limit(blocks, _nbytes((tm, d), BF16), 2 * _nbytes((tm, tn), F32))),
        name="proj",
    )(x, gain.reshape(1, d), w)


def _attn_body(q_ref, k_ref, v_ref, ck_ref, cv_ref, qg_ref, kg_ref, bvec_ref, o_ref, kn_ref,
               qn_ref, kp_ref, vp_ref, bias_ref, *, seq, tq, chunk, left_valid, group):
    keep = kn_ref.shape[1]
    qn_ref[...] = _rms_norm_rows(q_ref[0], qg_ref[...]).astype(BF16)
    kn = _rms_norm_rows(k_ref[0], kg_ref[...])
    kn_ref[0] = kn[seq - keep:, :]
    if left_valid:
        heads = ck_ref.shape[1] // LEFT
        mine = pl.ds(pl.program_id(1), LEFT, stride=heads)
        kp_ref[0:LEFT, :] = ck_ref[0, mine, :].astype(BF16)
        vp_ref[0:LEFT, :] = cv_ref[0, mine, :].astype(BF16)
    else:
        kp_ref[0:LEFT, :] = jnp.zeros((LEFT, ATT_HEAD_DIM), BF16)
        vp_ref[0:LEFT, :] = jnp.zeros((LEFT, ATT_HEAD_DIM), BF16)
    kp_ref[LEFT:LEFT + seq, :] = kn.astype(BF16)
    vp_ref[LEFT:LEFT + seq, :] = v_ref[0].astype(BF16)

    width = LEFT + tq
    bw = bvec_ref.shape[-1]
    bias = pltpu.roll(jnp.broadcast_to(bvec_ref[0], (tq, bw)), 0, 1, stride=1, stride_axis=0)[:, :width]
    if tq > chunk:
        row = lax.broadcasted_iota(jnp.int32, (tq, width), 0)
        col = lax.broadcasted_iota(jnp.int32, (tq, width), 1)
        chunk_start = (row // chunk) * chunk
        in_band = (col >= chunk_start) & (col < chunk_start + LEFT + chunk)
        bias = jnp.where(in_band, bias, NEG_INF)
    bias_ref[...] = bias
    scale = ATT_HEAD_DIM ** -0.5

    n_chunks = tq // chunk
    spans = []
    for r in range(n_chunks):
        lo = (r * chunk) // LANES * LANES
        hi = min(width, -(-(r * chunk + LEFT + chunk) // LANES) * LANES)
        spans.append((lo, hi))

    def block_bases(t):
        return [pl.multiple_of((t * group + u) * tq, tq) for u in range(group)]

    def blocks(t, mask_left):
        bases = block_bases(t)
        scores = [_mm_nt(qn_ref[pl.ds(base, tq), :], kp_ref[pl.ds(base, width), :]) for base in bases]
        pieces = [(u, r) for u in range(group) for r in range(n_chunks)]
        masked_scores = []
        for u, r in pieces:
            lo, hi = spans[r]
            rs = slice(r * chunk, (r + 1) * chunk)
            tiles = []
            for c0 in range(lo, hi, LANES):
                c1 = min(c0 + LANES, hi)
                band = bias_ref[rs, c0:c1]
                s = scores[u][rs, c0:c1] * scale + band
                inside = r * chunk <= c0 and c1 <= r * chunk + LEFT + chunk
                if mask_left or not inside:
                    ok = band > 0.5 * NEG_INF
                    if mask_left:
                        col = lax.broadcasted_iota(jnp.int32, (chunk, c1 - c0), 1) + c0
                        ok = ok & (col >= LEFT - bases[u])
                    s = jnp.where(ok, s, NEG_INF)
                tiles.append(s)
            masked_scores.append(jnp.concatenate(tiles, axis=1) if len(tiles) > 1 else tiles[0])
        maxes = [jnp.max(s, axis=-1, keepdims=True) for s in masked_scores]
        exps = [jnp.exp(s - m) for s, m in zip(masked_scores, maxes)]
        invs = [1.0 / jnp.sum(e, axis=-1, keepdims=True) for e in exps]
        rows = []
        for (u, r), e in zip(pieces, exps):
            lo, hi = spans[r]
            parts = ([jnp.zeros((chunk, lo), BF16)] if lo else []) + [e.astype(BF16)]
            parts += [jnp.zeros((chunk, width - hi), BF16)] if hi < width else []
            rows.append(jnp.concatenate(parts, axis=1) if len(parts) > 1 else parts[0])
        for u, base in enumerate(bases):
            mine = slice(u * n_chunks, (u + 1) * n_chunks)
            w = jnp.concatenate(rows[mine], axis=0) if n_chunks > 1 else rows[mine][0]
            inv = jnp.concatenate(invs[mine], axis=0) if n_chunks > 1 else invs[mine][0]
            o_ref[0, pl.ds(base, tq), :] = (_mm(w, vp_ref[pl.ds(base, width), :]) * inv).astype(o_ref.dtype)

    n_iter = seq // (tq * group)
    n_left = 0 if left_valid else min(n_iter, -(-LEFT // (tq * group)))

    def masked(t, carry):
        blocks(t, True)
        return carry

    def plain(t, carry):
        blocks(t, False)
        return carry

    lax.fori_loop(0, n_left, masked, 0)
    lax.fori_loop(n_left, n_iter, plain, 0)


def _attention(proj, ctx_k, ctx_v, q_gain, k_gain, bias_vec, *, heads, tq, chunk, left_valid, group):
    b, seq, _ = proj.shape
    dh = ATT_HEAD_DIM
    bw = bias_vec.shape[-1]
    keep = min(seq, LEFT)
    assert seq % (tq * group) == 0 and tq % chunk == 0 and ctx_k.shape[1:] == (LEFT, heads, dh)
    per_batch_ctx = ctx_k.shape[0] == b
    ctx_map = (lambda bi, h: (bi, 0, 0)) if per_batch_ctx else (lambda bi, h: (0, 0, 0))
    ctx_k, ctx_v = (c.reshape(c.shape[0], LEFT * heads, dh) for c in (ctx_k, ctx_v))
    blocks = (4 * _nbytes((seq, dh), F32) + 2 * _nbytes((LEFT, heads, dh), F32) + _nbytes((8, bw), F32)
              + _nbytes((keep, dh), F32))
    scratch = (_nbytes((seq, dh), BF16) + 2 * _nbytes((LEFT + seq, dh), BF16)
               + _nbytes((tq, LEFT + tq), F32))
    return pl.pallas_call(
        functools.partial(_attn_body, seq=seq, tq=tq, chunk=chunk, left_valid=left_valid, group=group),
        grid=(b, heads),
        in_specs=[
            pl.BlockSpec((1, seq, dh), lambda bi, h: (bi, 0, h)),
            pl.BlockSpec((1, seq, dh), lambda bi, h: (bi, 0, heads + h)),
            pl.BlockSpec((1, seq, dh), lambda bi, h: (bi, 0, 2 * heads + h)),
            pl.BlockSpec((1, LEFT * heads, dh), ctx_map),
            pl.BlockSpec((1, LEFT * heads, dh), ctx_map),
            pl.BlockSpec((1, dh), lambda bi, h: (0, 0)),
            pl.BlockSpec((1, dh), lambda bi, h: (0, 0)),
            pl.BlockSpec((1, 1, bw), lambda bi, h: (h, 0, 0)),
        ],
        out_specs=[
            pl.BlockSpec((1, seq, dh), lambda bi, h: (bi, 0, h)),
            pl.BlockSpec((1, keep, dh), lambda bi, h: (bi, 0, h)),
        ],
        out_shape=[jax.ShapeDtypeStruct((b, seq, heads * dh), BF16),
                   jax.ShapeDtypeStruct((b, keep, heads * dh), F32)],
        scratch_shapes=[pltpu.VMEM((seq, dh), BF16), pltpu.VMEM((LEFT + seq, dh), BF16),
                        pltpu.VMEM((LEFT + seq, dh), BF16), pltpu.VMEM((tq, LEFT + tq), F32)],
        compiler_params=pltpu.CompilerParams(
            dimension_semantics=("parallel", "parallel"),
            vmem_limit_bytes=_vmem_limit(blocks, scratch, 8 * group * _nbytes((tq, LEFT + tq), F32)
                                         + 3 * _nbytes((seq, dh), F32))),
        name="attn",
    )(proj, proj, proj, ctx_k, ctx_v, q_gain.reshape(1, dh), k_gain.reshape(1, dh), bias_vec)


def _band_bias_vec(rel_bias, tq):
    width = LEFT + tq
    bw = -(-(width + tq) // LANES) * LANES
    m = jnp.arange(bw)
    d = jnp.where(m < width, m, m - bw)
    idx = jnp.clip(LEFT - d, -REL_CLIP, REL_CLIP) + REL_CLIP
    return rel_bias[idx].T.astype(F32)[:, None, :]


def _split3(x):
    h1 = x.astype(BF16)
    r1 = x - h1.astype(F32)
    h2 = r1.astype(BF16)
    h3 = (r1 - h2.astype(F32)).astype(BF16)
    return h1, h2, h3


def _wkv_body(x_ref, sp_ref, s0_ref, mu_ref, w0_ref, wl_ref, a0_ref, al_ref, gl_ref, kk_ref, ka_ref,
              rk_ref, lnw_ref, lnb_ref, o_ref, so_ref, carry_ref, s_ref, *, clen, heads, col0, nseq):
    c = pl.program_id(1)
    n = RWKV_HEAD_DIM
    width = heads * n
    pairs = heads // 2
    pw = 2 * n
    tw = 2 * clen
    bf = lambda t: t.astype(BF16)

    @pl.when(c == 0)
    def _():
        zeros = jnp.zeros((n, n), F32)
        for i in range(nseq):
            carry_ref[i] = sp_ref[i]
            for q in range(pairs):
                s_ref[i * pairs + q] = jnp.concatenate(
                    [jnp.concatenate([s0_ref[i, 2 * q], zeros], axis=1),
                     jnp.concatenate([zeros, s0_ref[i, 2 * q + 1]], axis=1)], axis=0)

    trow = lax.broadcasted_iota(jnp.int32, (SUBLANES, 1), 0)
    ti = lax.broadcasted_iota(jnp.int32, (clen, clen), 0)
    si = lax.broadcasted_iota(jnp.int32, (clen, clen), 1)
    tri = (si <= ti).astype(BF16)
    levels = max(1, (clen - 1).bit_length())

    def prepare(i):
        xb = x_ref[i, :, col0:]
        rolled = pltpu.roll(xb, 1, 0)
        shifted = jnp.concatenate([jnp.where(trow == 0, carry_ref[i], rolled[:SUBLANES]), rolled[SUBLANES:]],
                                  axis=0)
        carry_ref[i] = xb[clen - 1:clen, :]
        xm = xb + mu_ref[...] * (shifted - xb)

        r = xm[:, 0:width]
        k = xm[:, width:2 * width]
        v = xm[:, 2 * width:3 * width]
        o3 = 3 * width
        w_lo = xm[:, o3:o3 + W_LORA]
        a_lo = xm[:, o3 + W_LORA:o3 + W_LORA + A_LORA]
        g_lo = xm[:, o3 + W_LORA + A_LORA:o3 + W_LORA + A_LORA + G_LORA]

        log_decay = -math.exp(-0.5) * jax.nn.sigmoid(w0_ref[...] + _mm(bf(jnp.tanh(w_lo)), wl_ref[...]))
        a = jax.nn.sigmoid(a0_ref[...] + _mm(bf(a_lo), al_ref[...]))
        g = _mm(bf(jax.nn.sigmoid(g_lo)), gl_ref[...])
        kk = k * kk_ref[...]
        k = k * (1.0 + (a - 1.0) * ka_ref[...])

        d1, d2, d3 = _split3(log_decay)
        cum = _mm(tri, d1) + _mm(tri, d2) + _mm(tri, d3)
        g_in = jnp.exp(cum)
        g_inv = jnp.exp(-cum)
        g_last = g_in[clen - 1:clen, :]
        k_h = k * g_inv
        return dict(v=v, a=a, g=g, kk=kk, rk=r * k * rk_ref[...], g_inv=g_inv, g_ex=jnp.exp(cum - log_decay),
                    g_last=g_last, r_t=r * g_in, k_h=k_h, k_l=k_h * g_last)

    pre = [prepare(i) for i in range(nseq)]

    first = lax.broadcasted_iota(jnp.int32, (1, pw), 1) < n
    first_t = lax.broadcasted_iota(jnp.int32, (1, tw), 1) < clen
    row2 = lax.broadcasted_iota(jnp.int32, (2 * clen, 2 * tw), 0)
    spos = lax.broadcasted_iota(jnp.int32, (2 * clen, 2 * tw), 1) % clen
    causal = spos < jnp.where(row2 < clen, row2, row2 - clen + 1)
    same_head = ((lax.broadcasted_iota(jnp.int32, (pw, pw), 0) < n)
                 == (lax.broadcasted_iota(jnp.int32, (pw, pw), 1) < n))

    def head_sums(x):
        s0 = jnp.sum(jnp.where(first, x, 0.0), axis=-1, keepdims=True)
        s1 = jnp.sum(jnp.where(first, 0.0, x), axis=-1, keepdims=True)
        return jnp.where(first, s0, s1)

    def block_diag(x, mask):
        zero = jnp.zeros_like(x)
        return jnp.concatenate([jnp.where(mask, x, zero), jnp.where(mask, zero, x)], axis=0)

    units = [(i, q) for i in range(nseq) for q in range(pairs)]
    us = range(len(units))
    cols = [slice(q * pw, (q + 1) * pw) for _, q in units]
    val = lambda name, u: pre[units[u][0]][name][:, cols[u]]

    b_hat, a_t, v_bd, s_prev, s_bf = [], [], [], [], []
    for u in us:
        kk_u = val("kk", u)
        kk_u = kk_u * jnp.minimum(lax.rsqrt(head_sums(kk_u * kk_u)), 1e12)
        b_hat.append(kk_u * val("a", u) * val("g_inv", u))
        a_t.append(-kk_u * val("g_ex", u))
        v_bd.append(block_diag(bf(val("v", u)), first))
        s_prev.append(s_ref[u])
        s_bf.append(bf(s_prev[u]))

    ar = [bf(jnp.concatenate([a_t[u], val("r_t", u)], axis=0)) for u in us]
    p = [_mm_nt(ar[u], jnp.concatenate([block_diag(bf(b_hat[u]), first),
                                        block_diag(bf(val("k_h", u)), first)], axis=0)) for u in us]
    pm = [bf(jnp.where(causal, p[u], 0.0)) for u in us]
    nk = [pm[u][:clen, :tw] for u in us]
    a_rb = [pm[u][clen:, :tw] for u in us]

    sv = [_mm_nt(ar[u], s_bf[u]) + _mm(pm[u][:, tw:], v_bd[u]) for u in us]
    sa = [sv[u][:clen] for u in us]
    for lvl in range(levels):
        if lvl + 1 < levels:
            both = [_mm(nk[u], jnp.concatenate([block_diag(bf(sa[u]), first), block_diag(nk[u], first_t)],
                                               axis=1)) for u in us]
            sa = [sa[u] + both[u][:, :pw] for u in us]
            nk = [bf(both[u][:, pw:]) for u in us]
        else:
            sa = [sa[u] + _mm(nk[u], block_diag(bf(sa[u]), first)) for u in us]

    sa_bf = [bf(sa[u]) for u in us]
    ys = [sv[u][clen:] + _mm(a_rb[u], block_diag(sa_bf[u], first)) for u in us]
    for u in us:
        g_last = val("g_last", u)
        upd = _mm_tn(jnp.concatenate([sa_bf[u], bf(val("v", u))], axis=0),
                     jnp.concatenate([bf(b_hat[u] * g_last), bf(val("k_l", u))], axis=0))
        s_ref[u] = s_prev[u] * g_last + jnp.where(same_head, upd, 0.0)

    outs = []
    inv_n = 1.0 / n
    for u in us:
        y = ys[u]
        mean = head_sums(y) * inv_n
        var = head_sums(jnp.square(y - mean)) * inv_n
        y = (y - mean) * lax.rsqrt(var + GN_EPS)
        y = y * lnw_ref[:, cols[u]] + lnb_ref[:, cols[u]]
        y = y + head_sums(val("rk", u)) * val("v", u)
        outs.append(y)

    for i in range(nseq):
        o_ref[i] = (jnp.concatenate(outs[i * pairs:(i + 1) * pairs], axis=1) * pre[i]["g"]).astype(o_ref.dtype)

    @pl.when(c == pl.num_programs(1) - 1)
    def _():
        for i in range(nseq):
            for q in range(pairs):
                so_ref[i, 2 * q] = s_ref[i * pairs + q, 0:n, 0:n]
                so_ref[i, 2 * q + 1] = s_ref[i * pairs + q, n:pw, n:pw]


def _wkv(proj, shift_prev, state0, mu, w0, w_lora, a0, a_lora, g_lora, k_k, k_a, r_k, lnx_w, lnx_b, *, clen,
         nseq):
    b, t, full_w = proj.shape
    heads = state0.shape[1]
    n = RWKV_HEAD_DIM
    width = heads * n
    pw = shift_prev.shape[-1]
    col0 = full_w - pw
    assert t % clen == 0 and pw == 3 * width + W_LORA + A_LORA + G_LORA and heads % 2 == 0
    assert col0 % LANES == 0 and 2 * n == LANES and b % nseq == 0
    row = lambda a: a.reshape(1, -1)
    full = lambda shape: pl.BlockSpec(shape, lambda bi, c: (0,) * len(shape))
    blocks = nseq * (_nbytes((clen, full_w), F32) + _nbytes((clen, width), F32)
                     + 2 * _nbytes((heads, n, LANES), F32))
    state_scratch = _nbytes((nseq * heads // 2, 2 * n, 2 * n), F32)
    return pl.pallas_call(
        functools.partial(_wkv_body, clen=clen, heads=heads, col0=col0, nseq=nseq),
        grid=(b // nseq, t // clen),
        in_specs=[
            pl.BlockSpec((nseq, clen, full_w), lambda bi, c: (bi, c, 0)),
            pl.BlockSpec((nseq, 1, pw), lambda bi, c: (bi, 0, 0)),
            pl.BlockSpec((nseq, heads, n, n), lambda bi, c: (bi, 0, 0, 0)),
            full((1, pw)), full((1, width)), full((W_LORA, width)), full((1, width)), full((A_LORA, width)),
            full((G_LORA, width)), full((1, width)), full((1, width)), full((1, width)), full((1, width)),
            full((1, width)),
        ],
        out_specs=[
            pl.BlockSpec((nseq, clen, width), lambda bi, c: (bi, c, 0)),
            pl.BlockSpec((nseq, heads, n, n), lambda bi, c: (bi, 0, 0, 0)),
        ],
        out_shape=[jax.ShapeDtypeStruct((b, t, width), BF16),
                   jax.ShapeDtypeStruct((b, heads, n, n), F32)],
        scratch_shapes=[pltpu.VMEM((nseq, 1, pw), F32), pltpu.VMEM((nseq * heads // 2, 2 * n, 2 * n), F32)],
        compiler_params=pltpu.CompilerParams(
            dimension_semantics=("parallel", "arbitrary"),
            vmem_limit_bytes=_vmem_limit(blocks, state_scratch, 40 * nseq * _nbytes((clen, width), F32))),
        name="wkv",
    )(proj, shift_prev, state0, row(mu), row(w0), w_lora, row(a0), a_lora, g_lora, row(k_k), row(k_a),
      row(r_k), row(lnx_w), row(lnx_b))


def _outproj_body(h_ref, att_ref, rw_ref, wa_ref, wr_ref, o_ref):
    o_ref[...] = h_ref[...] + _mm(att_ref[...], wa_ref[...]) + _mm(rw_ref[...], wr_ref[...])


def _outproj(h, att, rw, w_out, *, tm):
    m, d = h.shape
    ka, kr = att.shape[1], rw.shape[1]
    assert m % tm == 0 and ka == kr and w_out.shape == (ka + kr, d)
    blocks = (2 * _nbytes((tm, d), F32) + _nbytes((tm, ka), BF16) + _nbytes((tm, kr), BF16)
              + _nbytes((ka, d), BF16) + _nbytes((kr, d), BF16))
    return pl.pallas_call(
        _outproj_body,
        grid=(m // tm,),
        in_specs=[
            pl.BlockSpec((tm, d), lambda i: (i, 0)),
            pl.BlockSpec((tm, ka), lambda i: (i, 0)),
            pl.BlockSpec((tm, kr), lambda i: (i, 0)),
            pl.BlockSpec((ka, d), lambda i: (0, 0)),
            pl.BlockSpec((kr, d), lambda i: (1, 0)),
        ],
        out_specs=pl.BlockSpec((tm, d), lambda i: (i, 0)),
        out_shape=jax.ShapeDtypeStruct((m, d), F32),
        compiler_params=pltpu.CompilerParams(
            dimension_semantics=("parallel",),
            vmem_limit_bytes=_vmem_limit(blocks, 0, 2 * _nbytes((tm, d), F32))),
        name="outproj",
    )(h, att, rw, w_out, w_out)


def _ple_body(x_ref, g_ref, wg_ref, p_ref, wp_ref, o_ref):
    x = x_ref[...]
    gate = jax.nn.sigmoid(_mm(_rms_norm_rows(x, g_ref[...]).astype(BF16), wg_ref[...]))
    o_ref[...] = x + gate * _mm(p_ref[...].astype(BF16), wp_ref[...])


def _ple(x, gain, w_gate, p, w_proj, *, tm):
    m, d = x.shape
    pd = p.shape[1]
    assert m % tm == 0
    blocks = (2 * _nbytes((tm, d), F32) + _nbytes((d, d), BF16) + _nbytes((tm, pd), F32)
              + _nbytes((pd, d), BF16))
    return pl.pallas_call(
        _ple_body,
        grid=(m // tm,),
        in_specs=[
            pl.BlockSpec((tm, d), lambda i: (i, 0)),
            pl.BlockSpec((1, d), lambda i: (0, 0)),
            pl.BlockSpec((d, d), lambda i: (0, 0)),
            pl.BlockSpec((tm, pd), lambda i: (i, 0)),
            pl.BlockSpec((pd, d), lambda i: (0, 0)),
        ],
        out_specs=pl.BlockSpec((tm, d), lambda i: (i, 0)),
        out_shape=jax.ShapeDtypeStruct((m, d), F32),
        compiler_params=pltpu.CompilerParams(
            dimension_semantics=("parallel",),
            vmem_limit_bytes=_vmem_limit(blocks, 0, 4 * _nbytes((tm, d), F32))),
        name="ple",
    )(x, gain.reshape(1, d), w_gate, p, w_proj)


def _pick(m, candidates):
    for c in candidates:
        if m % c == 0:
            return c
    raise ValueError(f"no tile for {m}")


def _trunk_layer(x, p, lw, ctx_k, ctx_v, left_valid, shift_prev, wkv_prev):
    b, t, d = x.shape
    m = b * t
    tm = _pick(m, (512, 256, 128))
    tm_big = _pick(m, (1024, 512, 256, 128))
    tf = 512
    x2 = x.reshape(m, d)
    att_w = lw["w_out"].shape[0] // 2
    heads = att_w // ATT_HEAD_DIM

    h = _ffn(x2, lw["norm_ffn1"], lw["ffn1_gate"], lw["ffn1_up"], lw["ffn1_down"], tm=tm_big, tf=tf)
    proj = _proj(h, lw["norm_mix"], lw["w_in"], tm=tm_big, tn=lw["w_in"].shape[1] // 5)
    proj3 = proj.reshape(b, t, -1)

    clen = _pick(t, (CHUNK, 16))
    tq = _pick(t, (2 * CHUNK, CHUNK, 16))
    group = _pick(t // tq, (4, 2, 1))
    att, k_keep = _attention(proj3, ctx_k, ctx_v, lw["q_norm"], lw["k_norm"], _band_bias_vec(lw["rel_bias"], tq),
                             heads=heads, tq=tq, chunk=min(CHUNK, tq), left_valid=left_valid, group=group)
    rw, wkv_new = _wkv(proj3, shift_prev, wkv_prev, lw["rwkv_mu"], lw["rwkv_w0"], lw["rwkv_w_lora"],
                       lw["rwkv_a0"], lw["rwkv_a_lora"], lw["rwkv_g_lora"], lw["rwkv_k_k"], lw["rwkv_k_a"],
                       lw["rwkv_r_k"], lw["rwkv_lnx_w"], lw["rwkv_lnx_b"], clen=clen, nseq=_pick(b, (2, 1)))

    h = _outproj(h, att.reshape(m, att_w), rw.reshape(m, -1), lw["w_out"], tm=tm)
    h = _ffn(h, lw["norm_ffn2"], lw["ffn2_gate"], lw["ffn2_up"], lw["ffn2_down"], tm=tm_big, tf=tf)
    h = _ple(h, lw["norm_ple"], lw["ple_gate"], p.reshape(m, -1), lw["ple_proj"], tm=tm)

    keep = k_keep.shape[1]
    k_keep = k_keep.reshape(b, keep, heads, ATT_HEAD_DIM)
    v_keep = proj3[:, t - keep:, 2 * att_w:3 * att_w].reshape(b, keep, heads, ATT_HEAD_DIM)
    return h.reshape(b, t, d), (k_keep, v_keep, wkv_new, proj3[:, -1:, 3 * att_w:])


def kernel(x_prompt, x_sample, cache_att_k, cache_att_v, state_wkv, state_shift, p_prompt, p_sample, norm_ffn1, ffn1_gate, ffn1_up, ffn1_down, norm_mix, w_in, q_norm, k_norm, rel_bias, rwkv_mu, rwkv_w0, rwkv_w_lora, rwkv_a0, rwkv_a_lora, rwkv_g_lora, rwkv_k_k, rwkv_k_a, rwkv_r_k, rwkv_lnx_w, rwkv_lnx_b, w_out, norm_ffn2, ffn2_gate, ffn2_up, ffn2_down, norm_ple, ple_gate, ple_proj):
    depth = norm_ffn1.shape[0]
    hp, hs = x_prompt, x_sample
    bp = hp.shape[0]
    att_w = w_out.shape[1] // 2
    rwkv_heads = rwkv_r_k.shape[1]
    rwkv_pw = rwkv_mu.shape[1]
    prompt_states, sample_states = [], []
    for i in range(depth):
        lw = {
            "norm_ffn1": norm_ffn1[i], "ffn1_gate": ffn1_gate[i].astype(BF16), "ffn1_up": ffn1_up[i].astype(BF16),
            "ffn1_down": ffn1_down[i].astype(BF16), "norm_mix": norm_mix[i], "w_in": w_in[i].astype(BF16),
            "q_norm": q_norm[i], "k_norm": k_norm[i],
            "rel_bias": rel_bias[i], "rwkv_mu": rwkv_mu[i], "rwkv_w0": rwkv_w0[i],
            "rwkv_w_lora": rwkv_w_lora[i].astype(BF16), "rwkv_a0": rwkv_a0[i],
            "rwkv_a_lora": rwkv_a_lora[i].astype(BF16), "rwkv_g_lora": rwkv_g_lora[i].astype(BF16),
            "rwkv_k_k": rwkv_k_k[i], "rwkv_k_a": rwkv_k_a[i], "rwkv_r_k": rwkv_r_k[i].reshape(-1),
            "rwkv_lnx_w": rwkv_lnx_w[i], "rwkv_lnx_b": rwkv_lnx_b[i],
            "w_out": w_out[i].astype(BF16), "norm_ffn2": norm_ffn2[i], "ffn2_gate": ffn2_gate[i].astype(BF16),
            "ffn2_up": ffn2_up[i].astype(BF16), "ffn2_down": ffn2_down[i].astype(BF16),
            "norm_ple": norm_ple[i], "ple_gate": ple_gate[i].astype(BF16), "ple_proj": ple_proj[i].astype(BF16),
        }
        zero_ctx = jnp.zeros((1, LEFT, att_w // ATT_HEAD_DIM, ATT_HEAD_DIM), F32)
        zero_shift = jnp.zeros((bp, 1, rwkv_pw), hp.dtype)
        zero_wkv = jnp.zeros((bp, rwkv_heads, RWKV_HEAD_DIM, RWKV_HEAD_DIM), hp.dtype)
        hp, sp = _trunk_layer(hp, p_prompt[i], lw, zero_ctx, zero_ctx, False, zero_shift, zero_wkv)
        hs, ss = _trunk_layer(hs, p_sample[i], lw, cache_att_k[i], cache_att_v[i], True, state_shift[i],
                              state_wkv[i])
        prompt_states.append(sp)
        sample_states.append(ss)
    stack = lambda states, j: jnp.stack([s[j] for s in states])
    return (hp, hs, stack(prompt_states, 0), stack(prompt_states, 1), stack(prompt_states, 2),
            stack(prompt_states, 3), stack(sample_states, 0), stack(sample_states, 1),
            stack(sample_states, 2), stack(sample_states, 3))
```

```python
import functools
import math

import jax
import jax.numpy as jnp
from jax import lax
from jax.experimental import pallas as pl
from jax.experimental.pallas import tpu as pltpu

F32 = jnp.float32
BF16 = jnp.bfloat16

CHUNK = 64
LEFT = 512
ATT_HEAD_DIM = 128
REL_CLIP = 256
RWKV_HEAD_DIM = 64
W_LORA = 64
A_LORA = 64
G_LORA = 128
RMS_EPS = 1e-6
GN_EPS = 64e-5
NEG_INF = -1e30

LANES = 128
SUBLANES = 8
V7X_VMEM_BYTES = 64 * 1024 * 1024
VMEM_LIMIT_CAP = V7X_VMEM_BYTES - 8 * 1024 * 1024


def _vmem_limit(block_bytes, scratch_bytes, temp_bytes):
    need = 2 * block_bytes + scratch_bytes + temp_bytes + (4 << 20)
    return int(min(max(need, 16 << 20), VMEM_LIMIT_CAP))


def _nbytes(shape, dtype):
    return math.prod(shape) * jnp.dtype(dtype).itemsize


def _rms_norm_rows(x, gain):
    ms = jnp.mean(x * x, axis=-1, keepdims=True)
    return x * lax.rsqrt(ms + RMS_EPS) * gain


def _mm(a, b):
    return jnp.dot(a, b, preferred_element_type=F32)


def _mm_nt(a, b):
    return lax.dot_general(a, b, (((1,), (1,)), ((), ())), preferred_element_type=F32)


def _mm_tn(a, b):
    return lax.dot_general(a, b, (((0,), (0,)), ((), ())), preferred_element_type=F32)


def _ffn_body(x_ref, g_ref, wg_ref, wu_ref, wd_ref, o_ref, xn_ref):
    @pl.when(pl.program_id(1) == 0)
    def _():
        x = x_ref[...]
        xn_ref[...] = _rms_norm_rows(x, g_ref[...]).astype(BF16)
        o_ref[...] = x

    xn = xn_ref[...]
    half = wg_ref.shape[1] // 2
    cols = (slice(0, half), slice(half, 2 * half))
    au = [(_mm(xn, wg_ref[:, c]), _mm(xn, wu_ref[:, c])) for c in cols]
    hs = [(a * jax.nn.sigmoid(a) * u).astype(BF16) for a, u in au]
    o_ref[...] += 0.5 * (_mm(hs[0], wd_ref[cols[0], :]) + _mm(hs[1], wd_ref[cols[1], :]))


def _ffn(x, gain, wg, wu, wd, *, tm, tf):
    m, d = x.shape
    f = wg.shape[1]
    assert m % tm == 0 and f % tf == 0
    blocks = (_nbytes((tm, d), F32) * 2 + _nbytes((d, tf), BF16) * 3)
    temps = _nbytes((tm, tf), F32) * 4
    return pl.pallas_call(
        _ffn_body,
        grid=(m // tm, f // tf),
        in_specs=[
            pl.BlockSpec((tm, d), lambda i, j: (i, 0)),
            pl.BlockSpec((1, d), lambda i, j: (0, 0)),
            pl.BlockSpec((d, tf), lambda i, j: (0, j)),
            pl.BlockSpec((d, tf), lambda i, j: (0, j)),
            pl.BlockSpec((tf, d), lambda i, j: (j, 0)),
        ],
        out_specs=pl.BlockSpec((tm, d), lambda i, j: (i, 0)),
        out_shape=jax.ShapeDtypeStruct((m, d), F32),
        scratch_shapes=[pltpu.VMEM((tm, d), BF16)],
        compiler_params=pltpu.CompilerParams(
            dimension_semantics=("parallel", "arbitrary"),
            vmem_limit_bytes=_vmem_limit(blocks, _nbytes((tm, d), BF16), temps)),
        name="ffn",
    )(x, gain.reshape(1, d), wg, wu, wd)


def _proj_body(x_ref, g_ref, w_ref, o_ref, xn_ref):
    @pl.when(pl.program_id(1) == 0)
    def _():
        xn_ref[...] = _rms_norm_rows(x_ref[...], g_ref[...]).astype(BF16)

    o_ref[...] = _mm(xn_ref[...], w_ref[...])


def _proj(x, gain, w, *, tm, tn):
    m, d = x.shape
    n = w.shape[1]
    assert m % tm == 0 and n % tn == 0
    blocks = _nbytes((tm, d), F32) + _nbytes((d, tn), BF16) + _nbytes((tm, tn), F32)
    return pl.pallas_call(
        _proj_body,
        grid=(m // tm, n // tn),
        in_specs=[
            pl.BlockSpec((tm, d), lambda i, j: (i, 0)),
            pl.BlockSpec((1, d), lambda i, j: (0, 0)),
            pl.BlockSpec((d, tn), lambda i, j: (0, j)),
        ],
        out_specs=pl.BlockSpec((tm, tn), lambda i, j: (i, j)),
        out_shape=jax.ShapeDtypeStruct((m, n), F32),
        scratch_shapes=[pltpu.VMEM((tm, d), BF16)],
        compiler_params=pltpu.CompilerParams(
            dimension_semantics=("parallel", "arbitrary"),
            vmem_limit_bytes=_vmem_limit(blocks, _nbytes((tm, d), BF16), 2 * _nbytes((tm, tn), F32))),
        name="proj",
    )(x, gain.reshape(1, d), w)


def _attn_body(q_ref, k_ref, v_ref, ck_ref, cv_ref, qg_ref, kg_ref, bvec_ref, o_ref, kn_ref,
               kp_ref, vp_ref, bias_ref, *, seq, tq, chunk, left_valid, group):
    keep = kn_ref.shape[1]
    kn = _rms_norm_rows(k_ref[0], kg_ref[...])
    kn_ref[0] = kn[seq - keep:, :]
    if left_valid:
        heads = ck_ref.shape[1] // LEFT
        mine = pl.ds(pl.program_id(1), LEFT, stride=heads)
        kp_ref[0:LEFT, :] = ck_ref[0, mine, :].astype(BF16)
        vp_ref[0:LEFT, :] = cv_ref[0, mine, :].astype(BF16)
    else:
        kp_ref[0:LEFT, :] = jnp.zeros((LEFT, ATT_HEAD_DIM), BF16)
        vp_ref[0:LEFT, :] = jnp.zeros((LEFT, ATT_HEAD_DIM), BF16)
    kp_ref[LEFT:LEFT + seq, :] = kn.astype(BF16)
    vp_ref[LEFT:LEFT + seq, :] = v_ref[0].astype(BF16)

    width = LEFT + tq
    bw = bvec_ref.shape[-1]
    bias = pltpu.roll(jnp.broadcast_to(bvec_ref[0], (tq, bw)), 0, 1, stride=1, stride_axis=0)[:, :width]
    if tq > chunk:
        row = lax.broadcasted_iota(jnp.int32, (tq, width), 0)
        col = lax.broadcasted_iota(jnp.int32, (tq, width), 1)
        chunk_start = (row // chunk) * chunk
        in_band = (col >= chunk_start) & (col < chunk_start + LEFT + chunk)
        bias = jnp.where(in_band, bias, NEG_INF)
    bias_ref[...] = bias
    scale = ATT_HEAD_DIM ** -0.5

    n_chunks = tq // chunk
    spans = []
    for r in range(n_chunks):
        lo = (r * chunk) // LANES * LANES
        hi = min(width, -(-(r * chunk + LEFT + chunk) // LANES) * LANES)
        spans.append((lo, hi))

    def block_bases(t):
        return [pl.multiple_of((t * group + u) * tq, tq) for u in range(group)]

    def blocks(t, mask_left):
        bases = block_bases(t)
        queries = [_rms_norm_rows(q_ref[0, pl.ds(base, tq), :], qg_ref[...]).astype(BF16) for base in bases]
        scores = [_mm_nt(q, kp_ref[pl.ds(base, width), :]) for q, base in zip(queries, bases)]
        pieces = [(u, r) for u in range(group) for r in range(n_chunks)]
        masked_scores = []
        for u, r in pieces:
            lo, hi = spans[r]
            rs = slice(r * chunk, (r + 1) * chunk)
            tiles = []
            for c0 in range(lo, hi, LANES):
                c1 = min(c0 + LANES, hi)
                band = bias_ref[rs, c0:c1]
                s = scores[u][rs, c0:c1] * scale + band
                inside = r * chunk <= c0 and c1 <= r * chunk + LEFT + chunk
                if mask_left or not inside:
                    ok = band > 0.5 * NEG_INF
                    if mask_left:
                        col = lax.broadcasted_iota(jnp.int32, (chunk, c1 - c0), 1) + c0
                        ok = ok & (col >= LEFT - bases[u])
                    s = jnp.where(ok, s, NEG_INF)
                tiles.append(s)
            masked_scores.append(jnp.concatenate(tiles, axis=1) if len(tiles) > 1 else tiles[0])
        maxes = [jnp.max(s, axis=-1, keepdims=True) for s in masked_scores]
        exps = [jnp.exp(s - m) for s, m in zip(masked_scores, maxes)]
        invs = [1.0 / jnp.sum(e, axis=-1, keepdims=True) for e in exps]
        rows = []
        for (u, r), e in zip(pieces, exps):
            lo, hi = spans[r]
            parts = ([jnp.zeros((chunk, lo), BF16)] if lo else []) + [e.astype(BF16)]
            parts += [jnp.zeros((chunk, width - hi), BF16)] if hi < width else []
            rows.append(jnp.concatenate(parts, axis=1) if len(parts) > 1 else parts[0])
        for u, base in enumerate(bases):
            mine = slice(u * n_chunks, (u + 1) * n_chunks)
            w = jnp.concatenate(rows[mine], axis=0) if n_chunks > 1 else rows[mine][0]
            inv = jnp.concatenate(invs[mine], axis=0) if n_chunks > 1 else invs[mine][0]
            o_ref[0, pl.ds(base, tq), :] = (_mm(w, vp_ref[pl.ds(base, width), :]) * inv).astype(o_ref.dtype)

    n_iter = seq // (tq * group)
    n_left = 0 if left_valid else min(n_iter, -(-LEFT // (tq * group)))

    def masked(t, carry):
        blocks(t, True)
        return carry

    def plain(t, carry):
        blocks(t, False)
        return carry

    lax.fori_loop(0, n_left, masked, 0)
    lax.fori_loop(n_left, n_iter, plain, 0)


def _attention(proj, ctx_k, ctx_v, q_gain, k_gain, bias_vec, *, heads, tq, chunk, left_valid, group):
    b, seq, _ = proj.shape
    dh = ATT_HEAD_DIM
    bw = bias_vec.shape[-1]
    keep = min(seq, LEFT)
    assert seq % (tq * group) == 0 and tq % chunk == 0 and ctx_k.shape[1:] == (LEFT, heads, dh)
    per_batch_ctx = ctx_k.shape[0] == b
    ctx_map = (lambda bi, h: (bi, 0, 0)) if per_batch_ctx else (lambda bi, h: (0, 0, 0))
    ctx_k, ctx_v = (c.reshape(c.shape[0], LEFT * heads, dh) for c in (ctx_k, ctx_v))
    blocks = (4 * _nbytes((seq, dh), F32) + 2 * _nbytes((LEFT, heads, dh), F32) + _nbytes((8, bw), F32)
              + _nbytes((keep, dh), F32))
    scratch = (2 * _nbytes((LEFT + seq, dh), BF16)
               + _nbytes((tq, LEFT + tq), F32))
    return pl.pallas_call(
        functools.partial(_attn_body, seq=seq, tq=tq, chunk=chunk, left_valid=left_valid, group=group),
        grid=(b, heads),
        in_specs=[
            pl.BlockSpec((1, seq, dh), lambda bi, h: (bi, 0, h)),
            pl.BlockSpec((1, seq, dh), lambda bi, h: (bi, 0, heads + h)),
            pl.BlockSpec((1, seq, dh), lambda bi, h: (bi, 0, 2 * heads + h)),
            pl.BlockSpec((1, LEFT * heads, dh), ctx_map),
            pl.BlockSpec((1, LEFT * heads, dh), ctx_map),
            pl.BlockSpec((1, dh), lambda bi, h: (0, 0)),
            pl.BlockSpec((1, dh), lambda bi, h: (0, 0)),
            pl.BlockSpec((1, 1, bw), lambda bi, h: (h, 0, 0)),
        ],
        out_specs=[
            pl.BlockSpec((1, seq, dh), lambda bi, h: (bi, 0, h)),
            pl.BlockSpec((1, keep, dh), lambda bi, h: (bi, 0, h)),
        ],
        out_shape=[jax.ShapeDtypeStruct((b, seq, heads * dh), BF16),
                   jax.ShapeDtypeStruct((b, keep, heads * dh), F32)],
        scratch_shapes=[pltpu.VMEM((LEFT + seq, dh), BF16),
                        pltpu.VMEM((LEFT + seq, dh), BF16), pltpu.VMEM((tq, LEFT + tq), F32)],
        compiler_params=pltpu.CompilerParams(
            dimension_semantics=("parallel", "parallel"),
            vmem_limit_bytes=_vmem_limit(blocks, scratch, 8 * group * _nbytes((tq, LEFT + tq), F32)
                                         + 3 * _nbytes((seq, dh), F32))),
        name="attn",
    )(proj, proj, proj, ctx_k, ctx_v, q_gain.reshape(1, dh), k_gain.reshape(1, dh), bias_vec)


def _band_bias_vec(rel_bias, tq):
    width = LEFT + tq
    bw = -(-(width + tq) // LANES) * LANES
    m = jnp.arange(bw)
    d = jnp.where(m < width, m, m - bw)
    idx = jnp.clip(LEFT - d, -REL_CLIP, REL_CLIP) + REL_CLIP
    return rel_bias[idx].T.astype(F32)[:, None, :]


def _split3(x):
    h1 = x.astype(BF16)
    r1 = x - h1.astype(F32)
    h2 = r1.astype(BF16)
    h3 = (r1 - h2.astype(F32)).astype(BF16)
    return h1, h2, h3


def _wkv_body(x_ref, sp_ref, s0_ref, mu_ref, w0_ref, wl_ref, a0_ref, al_ref, gl_ref, kk_ref, ka_ref,
              rk_ref, lnw_ref, lnb_ref, o_ref, so_ref, carry_ref, s_ref, *, clen, heads, col0, nseq):
    c = pl.program_id(1)
    n = RWKV_HEAD_DIM
    width = heads * n
    pairs = heads // 2
    pw = 2 * n
    tw = 2 * clen
    bf = lambda t: t.astype(BF16)

    @pl.when(c == 0)
    def _():
        zeros = jnp.zeros((n, n), F32)
        for i in range(nseq):
            carry_ref[i] = sp_ref[i]
            for q in range(pairs):
                s_ref[i * pairs + q] = jnp.concatenate(
                    [jnp.concatenate([s0_ref[i, 2 * q], zeros], axis=1),
                     jnp.concatenate([zeros, s0_ref[i, 2 * q + 1]], axis=1)], axis=0)

    trow = lax.broadcasted_iota(jnp.int32, (SUBLANES, 1), 0)
    ti = lax.broadcasted_iota(jnp.int32, (clen, clen), 0)
    si = lax.broadcasted_iota(jnp.int32, (clen, clen), 1)
    tri = (si <= ti).astype(BF16)
    levels = max(1, (clen - 1).bit_length())

    def prepare(i):
        xb = x_ref[i, :, col0:]
        rolled = pltpu.roll(xb, 1, 0)
        shifted = jnp.concatenate([jnp.where(trow == 0, carry_ref[i], rolled[:SUBLANES]), rolled[SUBLANES:]],
                                  axis=0)
        carry_ref[i] = xb[clen - 1:clen, :]
        xm = xb + mu_ref[...] * (shifted - xb)

        r = xm[:, 0:width]
        k = xm[:, width:2 * width]
        v = xm[:, 2 * width:3 * width]
        o3 = 3 * width
        w_lo = xm[:, o3:o3 + W_LORA]
        a_lo = xm[:, o3 + W_LORA:o3 + W_LORA + A_LORA]
        g_lo = xm[:, o3 + W_LORA + A_LORA:o3 + W_LORA + A_LORA + G_LORA]

        log_decay = -math.exp(-0.5) * jax.nn.sigmoid(w0_ref[...] + _mm(bf(jnp.tanh(w_lo)), wl_ref[...]))
        a = jax.nn.sigmoid(a0_ref[...] + _mm(bf(a_lo), al_ref[...]))
        g = _mm(bf(jax.nn.sigmoid(g_lo)), gl_ref[...])
        kk = k * kk_ref[...]
        k = k * (1.0 + (a - 1.0) * ka_ref[...])

        d1, d2, d3 = _split3(log_decay)
        cum = _mm(tri, d1) + _mm(tri, d2) + _mm(tri, d3)
        g_in = jnp.exp(cum)
        g_inv = jnp.exp(-cum)
        g_last = g_in[clen - 1:clen, :]
        k_h = k * g_inv
        return dict(v=v, a=a, g=g, kk=kk, rk=r * k * rk_ref[...], g_inv=g_inv, g_ex=jnp.exp(cum - log_decay),
                    g_last=g_last, r_t=r * g_in, k_h=k_h, k_l=k_h * g_last)

    pre = [prepare(i) for i in range(nseq)]

    first = lax.broadcasted_iota(jnp.int32, (1, pw), 1) < n
    first_t = lax.broadcasted_iota(jnp.int32, (1, tw), 1) < clen
    row2 = lax.broadcasted_iota(jnp.int32, (2 * clen, 2 * tw), 0)
    spos = lax.broadcasted_iota(jnp.int32, (2 * clen, 2 * tw), 1) % clen
    causal = spos < jnp.where(row2 < clen, row2, row2 - clen + 1)
    same_head = ((lax.broadcasted_iota(jnp.int32, (pw, pw), 0) < n)
                 == (lax.broadcasted_iota(jnp.int32, (pw, pw), 1) < n))

    def head_sums(x):
        s0 = jnp.sum(jnp.where(first, x, 0.0), axis=-1, keepdims=True)
        s1 = jnp.sum(jnp.where(first, 0.0, x), axis=-1, keepdims=True)
        return jnp.where(first, s0, s1)

    def block_diag(x, mask):
        zero = jnp.zeros_like(x)
        return jnp.concatenate([jnp.where(mask, x, zero), jnp.where(mask, zero, x)], axis=0)

    units = [(i, q) for i in range(nseq) for q in range(pairs)]
    us = range(len(units))
    cols = [slice(q * pw, (q + 1) * pw) for _, q in units]
    val = lambda name, u: pre[units[u][0]][name][:, cols[u]]

    b_hat, a_t, v_bd, s_prev, s_bf = [], [], [], [], []
    for u in us:
        kk_u = val("kk", u)
        kk_u = kk_u * jnp.minimum(lax.rsqrt(head_sums(kk_u * kk_u)), 1e12)
        b_hat.append(kk_u * val("a", u) * val("g_inv", u))
        a_t.append(-kk_u * val("g_ex", u))
        v_bd.append(block_diag(bf(val("v", u)), first))
        s_prev.append(s_ref[u])
        s_bf.append(bf(s_prev[u]))

    ar = [bf(jnp.concatenate([a_t[u], val("r_t", u)], axis=0)) for u in us]
    p = [_mm_nt(ar[u], jnp.concatenate([block_diag(bf(b_hat[u]), first),
                                        block_diag(bf(val("k_h", u)), first)], axis=0)) for u in us]
    pm = [bf(jnp.where(causal, p[u], 0.0)) for u in us]
    nk = [pm[u][:clen, :tw] for u in us]
    a_rb = [pm[u][clen:, :tw] for u in us]

    sv = [_mm_nt(ar[u], s_bf[u]) + _mm(pm[u][:, tw:], v_bd[u]) for u in us]
    sa = [sv[u][:clen] for u in us]
    for lvl in range(levels):
        if lvl + 1 < levels:
            both = [_mm(nk[u], jnp.concatenate([block_diag(bf(sa[u]), first), block_diag(nk[u], first_t)],
                                               axis=1)) for u in us]
            sa = [sa[u] + both[u][:, :pw] for u in us]
            nk = [bf(both[u][:, pw:]) for u in us]
        else:
            sa = [sa[u] + _mm(nk[u], block_diag(bf(sa[u]), first)) for u in us]

    sa_bf = [bf(sa[u]) for u in us]
    ys = [sv[u][clen:] + _mm(a_rb[u], block_diag(sa_bf[u], first)) for u in us]
    for u in us:
        g_last = val("g_last", u)
        upd = _mm_tn(jnp.concatenate([sa_bf[u], bf(val("v", u))], axis=0),
                     jnp.concatenate([bf(b_hat[u] * g_last), bf(val("k_l", u))], axis=0))
        s_ref[u] = s_prev[u] * g_last + jnp.where(same_head, upd, 0.0)

    outs = []
    inv_n = 1.0 / n
    for u in us:
        y = ys[u]
        mean = head_sums(y) * inv_n
        var = head_sums(jnp.square(y - mean)) * inv_n
        y = (y - mean) * lax.rsqrt(var + GN_EPS)
        y = y * lnw_ref[:, cols[u]] + lnb_ref[:, cols[u]]
        y = y + head_sums(val("rk", u)) * val("v", u)
        outs.append(y)

    for i in range(nseq):
        o_ref[i] = (jnp.concatenate(outs[i * pairs:(i + 1) * pairs], axis=1) * pre[i]["g"]).astype(o_ref.dtype)

    @pl.when(c == pl.num_programs(1) - 1)
    def _():
        for i in range(nseq):
            for q in range(pairs):
                so_ref[i, 2 * q] = s_ref[i * pairs + q, 0:n, 0:n]
                so_ref[i, 2 * q + 1] = s_ref[i * pairs + q, n:pw, n:pw]


def _wkv(proj, shift_prev, state0, mu, w0, w_lora, a0, a_lora, g_lora, k_k, k_a, r_k, lnx_w, lnx_b, *, clen,
         nseq):
    b, t, full_w = proj.shape
    heads = state0.shape[1]
    n = RWKV_HEAD_DIM
    width = heads * n
    pw = shift_prev.shape[-1]
    col0 = full_w - pw
    assert t % clen == 0 and pw == 3 * width + W_LORA + A_LORA + G_LORA and heads % 2 == 0
    assert col0 % LANES == 0 and 2 * n == LANES and b % nseq == 0
    row = lambda a: a.reshape(1, -1)
    full = lambda shape: pl.BlockSpec(shape, lambda bi, c: (0,) * len(shape))
    blocks = nseq * (_nbytes((clen, full_w), F32) + _nbytes((clen, width), F32)
                     + 2 * _nbytes((heads, n, LANES), F32))
    state_scratch = _nbytes((nseq * heads // 2, 2 * n, 2 * n), F32)
    return pl.pallas_call(
        functools.partial(_wkv_body, clen=clen, heads=heads, col0=col0, nseq=nseq),
        grid=(b // nseq, t // clen),
        in_specs=[
            pl.BlockSpec((nseq, clen, full_w), lambda bi, c: (bi, c, 0)),
            pl.BlockSpec((nseq, 1, pw), lambda bi, c: (bi, 0, 0)),
            pl.BlockSpec((nseq, heads, n, n), lambda bi, c: (bi, 0, 0, 0)),
            full((1, pw)), full((1, width)), full((W_LORA, width)), full((1, width)), full((A_LORA, width)),
            full((G_LORA, width)), full((1, width)), full((1, width)), full((1, width)), full((1, width)),
            full((1, width)),
        ],
        out_specs=[
            pl.BlockSpec((nseq, clen, width), lambda bi, c: (bi, c, 0)),
            pl.BlockSpec((nseq, heads, n, n), lambda bi, c: (bi, 0, 0, 0)),
        ],
        out_shape=[jax.ShapeDtypeStruct((b, t, width), BF16),
                   jax.ShapeDtypeStruct((b, heads, n, n), F32)],
        scratch_shapes=[pltpu.VMEM((nseq, 1, pw), F32), pltpu.VMEM((nseq * heads // 2, 2 * n, 2 * n), F32)],
        compiler_params=pltpu.CompilerParams(
            dimension_semantics=("parallel", "arbitrary"),
            vmem_limit_bytes=_vmem_limit(blocks, state_scratch, 40 * nseq * _nbytes((clen, width), F32))),
        name="wkv",
    )(proj, shift_prev, state0, row(mu), row(w0), w_lora, row(a0), a_lora, g_lora, row(k_k), row(k_a),
      row(r_k), row(lnx_w), row(lnx_b))


def _outproj_body(h_ref, att_ref, rw_ref, wa_ref, wr_ref, o_ref):
    o_ref[...] = h_ref[...] + _mm(att_ref[...], wa_ref[...]) + _mm(rw_ref[...], wr_ref[...])


def _outproj(h, att, rw, w_out, *, tm):
    m, d = h.shape
    ka, kr = att.shape[1], rw.shape[1]
    assert m % tm == 0 and ka == kr and w_out.shape == (ka + kr, d)
    blocks = (2 * _nbytes((tm, d), F32) + _nbytes((tm, ka), BF16) + _nbytes((tm, kr), BF16)
              + _nbytes((ka, d), BF16) + _nbytes((kr, d), BF16))
    return pl.pallas_call(
        _outproj_body,
        grid=(m // tm,),
        in_specs=[
            pl.BlockSpec((tm, d), lambda i: (i, 0)),
            pl.BlockSpec((tm, ka), lambda i: (i, 0)),
            pl.BlockSpec((tm, kr), lambda i: (i, 0)),
            pl.BlockSpec((ka, d), lambda i: (0, 0)),
            pl.BlockSpec((kr, d), lambda i: (1, 0)),
        ],
        out_specs=pl.BlockSpec((tm, d), lambda i: (i, 0)),
        out_shape=jax.ShapeDtypeStruct((m, d), F32),
        compiler_params=pltpu.CompilerParams(
            dimension_semantics=("parallel",),
            vmem_limit_bytes=_vmem_limit(blocks, 0, 2 * _nbytes((tm, d), F32))),
        name="outproj",
    )(h, att, rw, w_out, w_out)


def _ple_body(x_ref, g_ref, wg_ref, p_ref, wp_ref, o_ref):
    x = x_ref[...]
    gate = jax.nn.sigmoid(_mm(_rms_norm_rows(x, g_ref[...]).astype(BF16), wg_ref[...]))
    o_ref[...] = x + gate * _mm(p_ref[...].astype(BF16), wp_ref[...])


def _ple(x, gain, w_gate, p, w_proj, *, tm):
    m, d = x.shape
    pd = p.shape[1]
    assert m % tm == 0
    blocks = (2 * _nbytes((tm, d), F32) + _nbytes((d, d), BF16) + _nbytes((tm, pd), F32)
              + _nbytes((pd, d), BF16))
    return pl.pallas_call(
        _ple_body,
        grid=(m // tm,),
        in_specs=[
            pl.BlockSpec((tm, d), lambda i: (i, 0)),
            pl.BlockSpec((1, d), lambda i: (0, 0)),
            pl.BlockSpec((d, d), lambda i: (0, 0)),
            pl.BlockSpec((tm, pd), lambda i: (i, 0)),
            pl.BlockSpec((pd, d), lambda i: (0, 0)),
        ],
        out_specs=pl.BlockSpec((tm, d), lambda i: (i, 0)),
        out_shape=jax.ShapeDtypeStruct((m, d), F32),
        compiler_params=pltpu.CompilerParams(
            dimension_semantics=("parallel",),
            vmem_limit_bytes=_vmem_limit(blocks, 0, 4 * _nbytes((tm, d), F32))),
        name="ple",
    )(x, gain.reshape(1, d), w_gate, p, w_proj)


def _pick(m, candidates):
    for c in candidates:
        if m % c == 0:
            return c
    raise ValueError(f"no tile for {m}")


def _trunk_layer(x, p, lw, ctx_k, ctx_v, left_valid, shift_prev, wkv_prev):
    b, t, d = x.shape
    m = b * t
    tm = _pick(m, (512, 256, 128))
    tm_big = _pick(m, (1024, 512, 256, 128))
    tf = 512
    x2 = x.reshape(m, d)
    att_w = lw["w_out"].shape[0] // 2
    heads = att_w // ATT_HEAD_DIM

    h = _ffn(x2, lw["norm_ffn1"], lw["ffn1_gate"], lw["ffn1_up"], lw["ffn1_down"], tm=tm_big, tf=tf)
    proj = _proj(h, lw["norm_mix"], lw["w_in"], tm=tm_big, tn=lw["w_in"].shape[1] // 5)
    proj3 = proj.reshape(b, t, -1)

    clen = _pick(t, (CHUNK, 16))
    tq = _pick(t, (2 * CHUNK, CHUNK, 16))
    group = _pick(t // tq, (4, 2, 1))
    att, k_keep = _attention(proj3, ctx_k, ctx_v, lw["q_norm"], lw["k_norm"], _band_bias_vec(lw["rel_bias"], tq),
                             heads=heads, tq=tq, chunk=min(CHUNK, tq), left_valid=left_valid, group=group)
    rw, wkv_new = _wkv(proj3, shift_prev, wkv_prev, lw["rwkv_mu"], lw["rwkv_w0"], lw["rwkv_w_lora"],
                       lw["rwkv_a0"], lw["rwkv_a_lora"], lw["rwkv_g_lora"], lw["rwkv_k_k"], lw["rwkv_k_a"],
                       lw["rwkv_r_k"], lw["rwkv_lnx_w"], lw["rwkv_lnx_b"], clen=clen, nseq=_pick(b, (2, 1)))

    h = _outproj(h, att.reshape(m, att_w), rw.reshape(m, -1), lw["w_out"], tm=tm)
    h = _ffn(h, lw["norm_ffn2"], lw["ffn2_gate"], lw["ffn2_up"], lw["ffn2_down"], tm=tm_big, tf=tf)
    h = _ple(h, lw["norm_ple"], lw["ple_gate"], p.reshape(m, -1), lw["ple_proj"], tm=tm)

    keep = k_keep.shape[1]
    k_keep = k_keep.reshape(b, keep, heads, ATT_HEAD_DIM)
    v_keep = proj3[:, t - keep:, 2 * att_w:3 * att_w].reshape(b, keep, heads, ATT_HEAD_DIM)
    return h.reshape(b, t, d), (k_keep, v_keep, wkv_new, proj3[:, -1:, 3 * att_w:])


def kernel(x_prompt, x_sample, cache_att_k, cache_att_v, state_wkv, state_shift, p_prompt, p_sample, norm_ffn1, ffn1_gate, ffn1_up, ffn1_down, norm_mix, w_in, q_norm, k_norm, rel_bias, rwkv_mu, rwkv_w0, rwkv_w_lora, rwkv_a0, rwkv_a_lora, rwkv_g_lora, rwkv_k_k, rwkv_k_a, rwkv_r_k, rwkv_lnx_w, rwkv_lnx_b, w_out, norm_ffn2, ffn2_gate, ffn2_up, ffn2_down, norm_ple, ple_gate, ple_proj):
    depth = norm_ffn1.shape[0]
    hp, hs = x_prompt, x_sample
    bp = hp.shape[0]
    att_w = w_out.shape[1] // 2
    rwkv_heads = rwkv_r_k.shape[1]
    rwkv_pw = rwkv_mu.shape[1]
    prompt_states, sample_states = [], []
    for i in range(depth):
        lw = {
            "norm_ffn1": norm_ffn1[i], "ffn1_gate": ffn1_gate[i].astype(BF16), "ffn1_up": ffn1_up[i].astype(BF16),
            "ffn1_down": ffn1_down[i].astype(BF16), "norm_mix": norm_mix[i], "w_in": w_in[i].astype(BF16),
            "q_norm": q_norm[i], "k_norm": k_norm[i],
            "rel_bias": rel_bias[i], "rwkv_mu": rwkv_mu[i], "rwkv_w0": rwkv_w0[i],
            "rwkv_w_lora": rwkv_w_lora[i].astype(BF16), "rwkv_a0": rwkv_a0[i],
            "rwkv_a_lora": rwkv_a_lora[i].astype(BF16), "rwkv_g_lora": rwkv_g_lora[i].astype(BF16),
            "rwkv_k_k": rwkv_k_k[i], "rwkv_k_a": rwkv_k_a[i], "rwkv_r_k": rwkv_r_k[i].reshape(-1),
            "rwkv_lnx_w": rwkv_lnx_w[i], "rwkv_lnx_b": rwkv_lnx_b[i],
            "w_out": w_out[i].astype(BF16), "norm_ffn2": norm_ffn2[i], "ffn2_gate": ffn2_gate[i].astype(BF16),
            "ffn2_up": ffn2_up[i].astype(BF16), "ffn2_down": ffn2_down[i].astype(BF16),
            "norm_ple": norm_ple[i], "ple_gate": ple_gate[i].astype(BF16), "ple_proj": ple_proj[i].astype(BF16),
        }
        zero_ctx = jnp.zeros((1, LEFT, att_w // ATT_HEAD_DIM, ATT_HEAD_DIM), F32)
        zero_shift = jnp.zeros((bp, 1, rwkv_pw), hp.dtype)
        zero_wkv = jnp.zeros((bp, rwkv_heads, RWKV_HEAD_DIM, RWKV_HEAD_DIM), hp.dtype)
        hp, sp = _trunk_layer(hp, p_prompt[i], lw, zero_ctx, zero_ctx, False, zero_shift, zero_wkv)
        hs, ss = _trunk_layer(hs, p_sample[i], lw, cache_att_k[i], cache_att_v[i], True, state_shift[i],
                              state_wkv[i])
        prompt_states.append(sp)
        sample_states.append(ss)
    stack = lambda states, j: jnp.stack([s[j] for s in states])
    return (hp, hs, stack(prompt_states, 0), stack(prompt_states, 1), stack(prompt_states, 2),
            stack(prompt_states, 3), stack(sample_states, 0), stack(sample_states, 1),
            stack(sample_states, 2), stack(sample_states, 3))
```

```python
import functools
import math

import jax
import jax.numpy as jnp
from jax import lax
from jax.experimental import pallas as pl
from jax.experimental.pallas import tpu as pltpu

F32 = jnp.float32
BF16 = jnp.bfloat16

CHUNK = 64
LEFT = 512
ATT_HEAD_DIM = 128
REL_CLIP = 256
RWKV_HEAD_DIM = 64
W_LORA = 64
A_LORA = 64
G_LORA = 128
RMS_EPS = 1e-6
GN_EPS = 64e-5
NEG_INF = -1e30

LANES = 128
SUBLANES = 8
V7X_VMEM_BYTES = 64 * 1024 * 1024
VMEM_LIMIT_CAP = V7X_VMEM_BYTES - 8 * 1024 * 1024


def _vmem_limit(block_bytes, scratch_bytes, temp_bytes):
    need = 2 * block_bytes + scratch_bytes + temp_bytes + (4 << 20)
    return int(min(max(need, 16 << 20), VMEM_LIMIT_CAP))


def _nbytes(shape, dtype):
    return math.prod(shape) * jnp.dtype(dtype).itemsize


def _rms_norm_rows(x, gain):
    ms = jnp.mean(x * x, axis=-1, keepdims=True)
    return x * lax.rsqrt(ms + RMS_EPS) * gain


def _mm(a, b):
    return jnp.dot(a, b, preferred_element_type=F32)


def _mm_nt(a, b):
    return lax.dot_general(a, b, (((1,), (1,)), ((), ())), preferred_element_type=F32)


def _mm_tn(a, b):
    return lax.dot_general(a, b, (((0,), (0,)), ((), ())), preferred_element_type=F32)


def _slab_spec(array, grid):
    n_i, n_j = grid
    rows = array.shape[0] // (n_i * n_j)
    if array.shape[0] % (n_i * n_j) or rows % (2 * SUBLANES):
        return None
    return pl.BlockSpec((rows, array.shape[1]), lambda i, j: (i * n_j + j, 0))


def _cast_slabs(src_refs, dst_refs):
    for src, dst in zip(src_refs, dst_refs):
        dst[...] = src[...].astype(BF16)


def _ffn_body(x_ref, g_ref, wg_ref, wu_ref, wd_ref, o_ref, xn_ref):
    @pl.when(pl.program_id(1) == 0)
    def _():
        x = x_ref[...]
        xn_ref[...] = _rms_norm_rows(x, g_ref[...]).astype(BF16)
        o_ref[...] = x

    xn = xn_ref[...]
    half = wg_ref.shape[1] // 2
    cols = (slice(0, half), slice(half, 2 * half))
    au = [(_mm(xn, wg_ref[:, c]), _mm(xn, wu_ref[:, c])) for c in cols]
    hs = [(a * jax.nn.sigmoid(a) * u).astype(BF16) for a, u in au]
    o_ref[...] += 0.5 * (_mm(hs[0], wd_ref[cols[0], :]) + _mm(hs[1], wd_ref[cols[1], :]))


def _ffn(x, gain, wg, wu, wd, *, tm, tf):
    m, d = x.shape
    f = wg.shape[1]
    assert m % tm == 0 and f % tf == 0
    blocks = (_nbytes((tm, d), F32) * 2 + _nbytes((d, tf), BF16) * 3)
    temps = _nbytes((tm, tf), F32) * 4
    return pl.pallas_call(
        _ffn_body,
        grid=(m // tm, f // tf),
        in_specs=[
            pl.BlockSpec((tm, d), lambda i, j: (i, 0)),
            pl.BlockSpec((1, d), lambda i, j: (0, 0)),
            pl.BlockSpec((d, tf), lambda i, j: (0, j)),
            pl.BlockSpec((d, tf), lambda i, j: (0, j)),
            pl.BlockSpec((tf, d), lambda i, j: (j, 0)),
        ],
        out_specs=pl.BlockSpec((tm, d), lambda i, j: (i, 0)),
        out_shape=jax.ShapeDtypeStruct((m, d), F32),
        scratch_shapes=[pltpu.VMEM((tm, d), BF16)],
        compiler_params=pltpu.CompilerParams(
            dimension_semantics=("parallel", "arbitrary"),
            vmem_limit_bytes=_vmem_limit(blocks, _nbytes((tm, d), BF16), temps)),
        name="ffn",
    )(x, gain.reshape(1, d), wg, wu, wd)


def _proj_body(x_ref, g_ref, w_ref, o_ref, xn_ref):
    @pl.when(pl.program_id(1) == 0)
    def _():
        xn_ref[...] = _rms_norm_rows(x_ref[...], g_ref[...]).astype(BF16)

    o_ref[...] = _mm(xn_ref[...], w_ref[...])


def _proj(x, gain, w, *, tm, tn):
    m, d = x.shape
    n = w.shape[1]
    assert m % tm == 0 and n % tn == 0
    blocks = _nbytes((tm, d), F32) + _nbytes((d, tn), BF16) + _nbytes((tm, tn), F32)
    return pl.pallas_call(
        _proj_body,
        grid=(m // tm, n // tn),
        in_specs=[
            pl.BlockSpec((tm, d), lambda i, j: (i, 0)),
            pl.BlockSpec((1, d), lambda i, j: (0, 0)),
            pl.BlockSpec((d, tn), lambda i, j: (0, j)),
        ],
        out_specs=pl.BlockSpec((tm, tn), lambda i, j: (i, j)),
        out_shape=jax.ShapeDtypeStruct((m, n), F32),
        scratch_shapes=[pltpu.VMEM((tm, d), BF16)],
        compiler_params=pltpu.CompilerParams(
            dimension_semantics=("parallel", "arbitrary"),
            vmem_limit_bytes=_vmem_limit(blocks, _nbytes((tm, d), BF16), 2 * _nbytes((tm, tn), F32))),
        name="proj",
    )(x, gain.reshape(1, d), w)


def _attn_body(*refs, seq, tq, chunk, left_valid, group, n_cast):
    q_ref, k_ref, v_ref, ck_ref, cv_ref, qg_ref, kg_ref, bvec_ref = refs[:8]
    o_ref, kn_ref = refs[8 + n_cast:10 + n_cast]
    kp_ref, vp_ref, bias_ref = refs[10 + 2 * n_cast:]
    _cast_slabs(refs[8:8 + n_cast], refs[10 + n_cast:10 + 2 * n_cast])

    keep = kn_ref.shape[1]
    kn = _rms_norm_rows(k_ref[0], kg_ref[...])
    kn_ref[0] = kn[seq - keep:, :]
    if left_valid:
        heads = ck_ref.shape[1] // LEFT
        mine = pl.ds(pl.program_id(1), LEFT, stride=heads)
        kp_ref[0:LEFT, :] = ck_ref[0, mine, :].astype(BF16)
        vp_ref[0:LEFT, :] = cv_ref[0, mine, :].astype(BF16)
    else:
        kp_ref[0:LEFT, :] = jnp.zeros((LEFT, ATT_HEAD_DIM), BF16)
        vp_ref[0:LEFT, :] = jnp.zeros((LEFT, ATT_HEAD_DIM), BF16)
    kp_ref[LEFT:LEFT + seq, :] = kn.astype(BF16)
    vp_ref[LEFT:LEFT + seq, :] = v_ref[0].astype(BF16)

    width = LEFT + tq
    bw = bvec_ref.shape[-1]
    bias = pltpu.roll(jnp.broadcast_to(bvec_ref[0], (tq, bw)), 0, 1, stride=1, stride_axis=0)[:, :width]
    if tq > chunk:
        row = lax.broadcasted_iota(jnp.int32, (tq, width), 0)
        col = lax.broadcasted_iota(jnp.int32, (tq, width), 1)
        chunk_start = (row // chunk) * chunk
        in_band = (col >= chunk_start) & (col < chunk_start + LEFT + chunk)
        bias = jnp.where(in_band, bias, NEG_INF)
    bias_ref[...] = bias
    scale = ATT_HEAD_DIM ** -0.5

    n_chunks = tq // chunk
    spans = []
    for r in range(n_chunks):
        lo = (r * chunk) // LANES * LANES
        hi = min(width, -(-(r * chunk + LEFT + chunk) // LANES) * LANES)
        spans.append((lo, hi))

    def block_bases(t):
        return [pl.multiple_of((t * group + u) * tq, tq) for u in range(group)]

    def blocks(t, mask_left):
        bases = block_bases(t)
        queries = [_rms_norm_rows(q_ref[0, pl.ds(base, tq), :], qg_ref[...]).astype(BF16) for base in bases]
        scores = [_mm_nt(q, kp_ref[pl.ds(base, width), :]) for q, base in zip(queries, bases)]
        pieces = [(u, r) for u in range(group) for r in range(n_chunks)]
        masked_scores = []
        for u, r in pieces:
            lo, hi = spans[r]
            rs = slice(r * chunk, (r + 1) * chunk)
            tiles = []
            for c0 in range(lo, hi, LANES):
                c1 = min(c0 + LANES, hi)
                band = bias_ref[rs, c0:c1]
                s = scores[u][rs, c0:c1] * scale + band
                inside = r * chunk <= c0 and c1 <= r * chunk + LEFT + chunk
                if mask_left or not inside:
                    ok = band > 0.5 * NEG_INF
                    if mask_left:
                        col = lax.broadcasted_iota(jnp.int32, (chunk, c1 - c0), 1) + c0
                        ok = ok & (col >= LEFT - bases[u])
                    s = jnp.where(ok, s, NEG_INF)
                tiles.append(s)
            masked_scores.append(jnp.concatenate(tiles, axis=1) if len(tiles) > 1 else tiles[0])
        maxes = [jnp.max(s, axis=-1, keepdims=True) for s in masked_scores]
        exps = [jnp.exp(s - m) for s, m in zip(masked_scores, maxes)]
        invs = [1.0 / jnp.sum(e, axis=-1, keepdims=True) for e in exps]
        rows = []
        for (u, r), e in zip(pieces, exps):
            lo, hi = spans[r]
            parts = ([jnp.zeros((chunk, lo), BF16)] if lo else []) + [e.astype(BF16)]
            parts += [jnp.zeros((chunk, width - hi), BF16)] if hi < width else []
            rows.append(jnp.concatenate(parts, axis=1) if len(parts) > 1 else parts[0])
        for u, base in enumerate(bases):
            mine = slice(u * n_chunks, (u + 1) * n_chunks)
            w = jnp.concatenate(rows[mine], axis=0) if n_chunks > 1 else rows[mine][0]
            inv = jnp.concatenate(invs[mine], axis=0) if n_chunks > 1 else invs[mine][0]
            o_ref[0, pl.ds(base, tq), :] = (_mm(w, vp_ref[pl.ds(base, width), :]) * inv).astype(o_ref.dtype)

    n_iter = seq // (tq * group)
    n_left = 0 if left_valid else min(n_iter, -(-LEFT // (tq * group)))

    def masked(t, carry):
        blocks(t, True)
        return carry

    def plain(t, carry):
        blocks(t, False)
        return carry

    lax.fori_loop(0, n_left, masked, 0)
    lax.fori_loop(n_left, n_iter, plain, 0)


def _attention(proj, ctx_k, ctx_v, q_gain, k_gain, bias_vec, *, heads, tq, chunk, left_valid, group, casts=()):
    b, seq, _ = proj.shape
    dh = ATT_HEAD_DIM
    bw = bias_vec.shape[-1]
    keep = min(seq, LEFT)
    assert seq % (tq * group) == 0 and tq % chunk == 0 and ctx_k.shape[1:] == (LEFT, heads, dh)
    per_batch_ctx = ctx_k.shape[0] == b
    ctx_map = (lambda bi, h: (bi, 0, 0)) if per_batch_ctx else (lambda bi, h: (0, 0, 0))
    ctx_k, ctx_v = (c.reshape(c.shape[0], LEFT * heads, dh) for c in (ctx_k, ctx_v))
    slabs = [_slab_spec(a, (b, heads)) for a in casts]
    blocks = (4 * _nbytes((seq, dh), F32) + 2 * _nbytes((LEFT, heads, dh), F32) + _nbytes((8, bw), F32)
              + _nbytes((keep, dh), F32)
              + sum(_nbytes(s.block_shape, F32) + _nbytes(s.block_shape, BF16) for s in slabs))
    scratch = (2 * _nbytes((LEFT + seq, dh), BF16)
               + _nbytes((tq, LEFT + tq), F32))
    outs = pl.pallas_call(
        functools.partial(_attn_body, seq=seq, tq=tq, chunk=chunk, left_valid=left_valid, group=group,
                          n_cast=len(casts)),
        grid=(b, heads),
        in_specs=[
            pl.BlockSpec((1, seq, dh), lambda bi, h: (bi, 0, h)),
            pl.BlockSpec((1, seq, dh), lambda bi, h: (bi, 0, heads + h)),
            pl.BlockSpec((1, seq, dh), lambda bi, h: (bi, 0, 2 * heads + h)),
            pl.BlockSpec((1, LEFT * heads, dh), ctx_map),
            pl.BlockSpec((1, LEFT * heads, dh), ctx_map),
            pl.BlockSpec((1, dh), lambda bi, h: (0, 0)),
            pl.BlockSpec((1, dh), lambda bi, h: (0, 0)),
            pl.BlockSpec((1, 1, bw), lambda bi, h: (h, 0, 0)),
        ] + slabs,
        out_specs=[
            pl.BlockSpec((1, seq, dh), lambda bi, h: (bi, 0, h)),
            pl.BlockSpec((1, keep, dh), lambda bi, h: (bi, 0, h)),
        ] + [_slab_spec(a, (b, heads)) for a in casts],
        out_shape=[jax.ShapeDtypeStruct((b, seq, heads * dh), BF16),
                   jax.ShapeDtypeStruct((b, keep, heads * dh), F32)]
                  + [jax.ShapeDtypeStruct(a.shape, BF16) for a in casts],
        scratch_shapes=[pltpu.VMEM((LEFT + seq, dh), BF16),
                        pltpu.VMEM((LEFT + seq, dh), BF16), pltpu.VMEM((tq, LEFT + tq), F32)],
        compiler_params=pltpu.CompilerParams(
            dimension_semantics=("parallel", "parallel"),
            vmem_limit_bytes=_vmem_limit(blocks, scratch, 8 * group * _nbytes((tq, LEFT + tq), F32)
                                         + 3 * _nbytes((seq, dh), F32))),
        name="attn",
    )(proj, proj, proj, ctx_k, ctx_v, q_gain.reshape(1, dh), k_gain.reshape(1, dh), bias_vec, *casts)
    return outs[0], outs[1], list(outs[2:])


def _band_bias_vec(rel_bias, tq):
    width = LEFT + tq
    bw = -(-(width + tq) // LANES) * LANES
    m = jnp.arange(bw)
    d = jnp.where(m < width, m, m - bw)
    idx = jnp.clip(LEFT - d, -REL_CLIP, REL_CLIP) + REL_CLIP
    return rel_bias[idx].T.astype(F32)[:, None, :]


def _split3(x):
    h1 = x.astype(BF16)
    r1 = x - h1.astype(F32)
    h2 = r1.astype(BF16)
    h3 = (r1 - h2.astype(F32)).astype(BF16)
    return h1, h2, h3


def _wkv_body(*refs, clen, heads, col0, nseq, n_cast):
    (x_ref, sp_ref, s0_ref, mu_ref, w0_ref, wl_ref, a0_ref, al_ref, gl_ref, kk_ref, ka_ref, rk_ref, lnw_ref,
     lnb_ref) = refs[:14]
    o_ref, so_ref = refs[14 + n_cast:16 + n_cast]
    carry_ref, s_ref = refs[16 + 2 * n_cast:]
    _cast_slabs(refs[14:14 + n_cast], refs[16 + n_cast:16 + 2 * n_cast])
    c = pl.program_id(1)
    n = RWKV_HEAD_DIM
    width = heads * n
    pairs = heads // 2
    pw = 2 * n
    tw = 2 * clen
    bf = lambda t: t.astype(BF16)

    @pl.when(c == 0)
    def _():
        zeros = jnp.zeros((n, n), F32)
        for i in range(nseq):
            carry_ref[i] = sp_ref[i]
            for q in range(pairs):
                s_ref[i * pairs + q] = jnp.concatenate(
                    [jnp.concatenate([s0_ref[i, 2 * q], zeros], axis=1),
                     jnp.concatenate([zeros, s0_ref[i, 2 * q + 1]], axis=1)], axis=0)

    trow = lax.broadcasted_iota(jnp.int32, (SUBLANES, 1), 0)
    ti = lax.broadcasted_iota(jnp.int32, (clen, clen), 0)
    si = lax.broadcasted_iota(jnp.int32, (clen, clen), 1)
    tri = (si <= ti).astype(BF16)
    levels = max(1, (clen - 1).bit_length())

    def prepare(i):
        xb = x_ref[i, :, col0:]
        rolled = pltpu.roll(xb, 1, 0)
        shifted = jnp.concatenate([jnp.where(trow == 0, carry_ref[i], rolled[:SUBLANES]), rolled[SUBLANES:]],
                                  axis=0)
        carry_ref[i] = xb[clen - 1:clen, :]
        xm = xb + mu_ref[...] * (shifted - xb)

        r = xm[:, 0:width]
        k = xm[:, width:2 * width]
        v = xm[:, 2 * width:3 * width]
        o3 = 3 * width
        w_lo = xm[:, o3:o3 + W_LORA]
        a_lo = xm[:, o3 + W_LORA:o3 + W_LORA + A_LORA]
        g_lo = xm[:, o3 + W_LORA + A_LORA:o3 + W_LORA + A_LORA + G_LORA]

        log_decay = -math.exp(-0.5) * jax.nn.sigmoid(w0_ref[...] + _mm(bf(jnp.tanh(w_lo)), wl_ref[...]))
        a = jax.nn.sigmoid(a0_ref[...] + _mm(bf(a_lo), al_ref[...]))
        g = _mm(bf(jax.nn.sigmoid(g_lo)), gl_ref[...])
        kk = k * kk_ref[...]
        k = k * (1.0 + (a - 1.0) * ka_ref[...])

        d1, d2, d3 = _split3(log_decay)
        cum = _mm(tri, d1) + _mm(tri, d2) + _mm(tri, d3)
        g_in = jnp.exp(cum)
        g_inv = jnp.exp(-cum)
        g_last = g_in[clen - 1:clen, :]
        k_h = k * g_inv
        return dict(v=v, a=a, g=g, kk=kk, rk=r * k * rk_ref[...], g_inv=g_inv, g_ex=jnp.exp(cum - log_decay),
                    g_last=g_last, r_t=r * g_in, k_h=k_h, k_l=k_h * g_last)

    pre = [prepare(i) for i in range(nseq)]

    first = lax.broadcasted_iota(jnp.int32, (1, pw), 1) < n
    first_t = lax.broadcasted_iota(jnp.int32, (1, tw), 1) < clen
    row2 = lax.broadcasted_iota(jnp.int32, (2 * clen, 2 * tw), 0)
    spos = lax.broadcasted_iota(jnp.int32, (2 * clen, 2 * tw), 1) % clen
    causal = spos < jnp.where(row2 < clen, row2, row2 - clen + 1)
    same_head = ((lax.broadcasted_iota(jnp.int32, (pw, pw), 0) < n)
                 == (lax.broadcasted_iota(jnp.int32, (pw, pw), 1) < n))

    def head_sums(x):
        s0 = jnp.sum(jnp.where(first, x, 0.0), axis=-1, keepdims=True)
        s1 = jnp.sum(jnp.where(first, 0.0, x), axis=-1, keepdims=True)
        return jnp.where(first, s0, s1)

    def block_diag(x, mask):
        zero = jnp.zeros_like(x)
        return jnp.concatenate([jnp.where(mask, x, zero), jnp.where(mask, zero, x)], axis=0)

    units = [(i, q) for i in range(nseq) for q in range(pairs)]
    us = range(len(units))
    cols = [slice(q * pw, (q + 1) * pw) for _, q in units]
    val = lambda name, u: pre[units[u][0]][name][:, cols[u]]

    b_hat, a_t, v_bd, s_prev, s_bf = [], [], [], [], []
    for u in us:
        kk_u = val("kk", u)
        kk_u = kk_u * jnp.minimum(lax.rsqrt(head_sums(kk_u * kk_u)), 1e12)
        b_hat.append(kk_u * val("a", u) * val("g_inv", u))
        a_t.append(-kk_u * val("g_ex", u))
        v_bd.append(block_diag(bf(val("v", u)), first))
        s_prev.append(s_ref[u])
        s_bf.append(bf(s_prev[u]))

    ar = [bf(jnp.concatenate([a_t[u], val("r_t", u)], axis=0)) for u in us]
    p = [_mm_nt(ar[u], jnp.concatenate([block_diag(bf(b_hat[u]), first),
                                        block_diag(bf(val("k_h", u)), first)], axis=0)) for u in us]
    pm = [bf(jnp.where(causal, p[u], 0.0)) for u in us]
    nk = [pm[u][:clen, :tw] for u in us]
    a_rb = [pm[u][clen:, :tw] for u in us]

    sv = [_mm_nt(ar[u], s_bf[u]) + _mm(pm[u][:, tw:], v_bd[u]) for u in us]
    sa = [sv[u][:clen] for u in us]
    for lvl in range(levels):
        if lvl + 1 < levels:
            both = [_mm(nk[u], jnp.concatenate([block_diag(bf(sa[u]), first), block_diag(nk[u], first_t)],
                                               axis=1)) for u in us]
            sa = [sa[u] + both[u][:, :pw] for u in us]
            nk = [bf(both[u][:, pw:]) for u in us]
        else:
            sa = [sa[u] + _mm(nk[u], block_diag(bf(sa[u]), first)) for u in us]

    sa_bf = [bf(sa[u]) for u in us]
    ys = [sv[u][clen:] + _mm(a_rb[u], block_diag(sa_bf[u], first)) for u in us]
    for u in us:
        g_last = val("g_last", u)
        upd = _mm_tn(jnp.concatenate([sa_bf[u], bf(val("v", u))], axis=0),
                     jnp.concatenate([bf(b_hat[u] * g_last), bf(val("k_l", u))], axis=0))
        s_ref[u] = s_prev[u] * g_last + jnp.where(same_head, upd, 0.0)

    outs = []
    inv_n = 1.0 / n
    for u in us:
        y = ys[u]
        mean = head_sums(y) * inv_n
        var = head_sums(jnp.square(y - mean)) * inv_n
        y = (y - mean) * lax.rsqrt(var + GN_EPS)
        y = y * lnw_ref[:, cols[u]] + lnb_ref[:, cols[u]]
        y = y + head_sums(val("rk", u)) * val("v", u)
        outs.append(y)

    for i in range(nseq):
        o_ref[i] = (jnp.concatenate(outs[i * pairs:(i + 1) * pairs], axis=1) * pre[i]["g"]).astype(o_ref.dtype)

    @pl.when(c == pl.num_programs(1) - 1)
    def _():
        for i in range(nseq):
            for q in range(pairs):
                so_ref[i, 2 * q] = s_ref[i * pairs + q, 0:n, 0:n]
                so_ref[i, 2 * q + 1] = s_ref[i * pairs + q, n:pw, n:pw]


def _wkv(proj, shift_prev, state0, mu, w0, w_lora, a0, a_lora, g_lora, k_k, k_a, r_k, lnx_w, lnx_b, *, clen,
         nseq, casts=()):
    b, t, full_w = proj.shape
    heads = state0.shape[1]
    n = RWKV_HEAD_DIM
    width = heads * n
    pw = shift_prev.shape[-1]
    col0 = full_w - pw
    assert t % clen == 0 and pw == 3 * width + W_LORA + A_LORA + G_LORA and heads % 2 == 0
    assert col0 % LANES == 0 and 2 * n == LANES and b % nseq == 0
    row = lambda a: a.reshape(1, -1)
    full = lambda shape: pl.BlockSpec(shape, lambda bi, c: (0,) * len(shape))
    grid = (b // nseq, t // clen)
    slabs = [_slab_spec(a, grid) for a in casts]
    blocks = (nseq * (_nbytes((clen, full_w), F32) + _nbytes((clen, width), F32)
                      + 2 * _nbytes((heads, n, LANES), F32))
              + sum(_nbytes(s.block_shape, F32) + _nbytes(s.block_shape, BF16) for s in slabs))
    state_scratch = _nbytes((nseq * heads // 2, 2 * n, 2 * n), F32)
    outs = pl.pallas_call(
        functools.partial(_wkv_body, clen=clen, heads=heads, col0=col0, nseq=nseq, n_cast=len(casts)),
        grid=grid,
        in_specs=[
            pl.BlockSpec((nseq, clen, full_w), lambda bi, c: (bi, c, 0)),
            pl.BlockSpec((nseq, 1, pw), lambda bi, c: (bi, 0, 0)),
            pl.BlockSpec((nseq, heads, n, n), lambda bi, c: (bi, 0, 0, 0)),
            full((1, pw)), full((1, width)), full((W_LORA, width)), full((1, width)), full((A_LORA, width)),
            full((G_LORA, width)), full((1, width)), full((1, width)), full((1, width)), full((1, width)),
            full((1, width)),
        ] + slabs,
        out_specs=[
            pl.BlockSpec((nseq, clen, width), lambda bi, c: (bi, c, 0)),
            pl.BlockSpec((nseq, heads, n, n), lambda bi, c: (bi, 0, 0, 0)),
        ] + [_slab_spec(a, grid) for a in casts],
        out_shape=[jax.ShapeDtypeStruct((b, t, width), BF16),
                   jax.ShapeDtypeStruct((b, heads, n, n), F32)]
                  + [jax.ShapeDtypeStruct(a.shape, BF16) for a in casts],
        scratch_shapes=[pltpu.VMEM((nseq, 1, pw), F32), pltpu.VMEM((nseq * heads // 2, 2 * n, 2 * n), F32)],
        compiler_params=pltpu.CompilerParams(
            dimension_semantics=("parallel", "arbitrary"),
            vmem_limit_bytes=_vmem_limit(blocks, state_scratch, 40 * nseq * _nbytes((clen, width), F32))),
        name="wkv",
    )(proj, shift_prev, state0, row(mu), row(w0), w_lora, row(a0), a_lora, g_lora, row(k_k), row(k_a),
      row(r_k), row(lnx_w), row(lnx_b), *casts)
    return outs[0], outs[1], list(outs[2:])


def _outproj_body(h_ref, att_ref, rw_ref, wa_ref, wr_ref, o_ref):
    o_ref[...] = h_ref[...] + _mm(att_ref[...], wa_ref[...]) + _mm(rw_ref[...], wr_ref[...])


def _outproj(h, att, rw, w_out, *, tm):
    m, d = h.shape
    ka, kr = att.shape[1], rw.shape[1]
    assert m % tm == 0 and ka == kr and w_out.shape == (ka + kr, d)
    blocks = (2 * _nbytes((tm, d), F32) + _nbytes((tm, ka), BF16) + _nbytes((tm, kr), BF16)
              + _nbytes((ka, d), BF16) + _nbytes((kr, d), BF16))
    return pl.pallas_call(
        _outproj_body,
        grid=(m // tm,),
        in_specs=[
            pl.BlockSpec((tm, d), lambda i: (i, 0)),
            pl.BlockSpec((tm, ka), lambda i: (i, 0)),
            pl.BlockSpec((tm, kr), lambda i: (i, 0)),
            pl.BlockSpec((ka, d), lambda i: (0, 0)),
            pl.BlockSpec((kr, d), lambda i: (1, 0)),
        ],
        out_specs=pl.BlockSpec((tm, d), lambda i: (i, 0)),
        out_shape=jax.ShapeDtypeStruct((m, d), F32),
        compiler_params=pltpu.CompilerParams(
            dimension_semantics=("parallel",),
            vmem_limit_bytes=_vmem_limit(blocks, 0, 2 * _nbytes((tm, d), F32))),
        name="outproj",
    )(h, att, rw, w_out, w_out)


def _ple_body(x_ref, g_ref, wg_ref, p_ref, wp_ref, o_ref):
    x = x_ref[...]
    gate = jax.nn.sigmoid(_mm(_rms_norm_rows(x, g_ref[...]).astype(BF16), wg_ref[...]))
    o_ref[...] = x + gate * _mm(p_ref[...].astype(BF16), wp_ref[...])


def _ple(x, gain, w_gate, p, w_proj, *, tm):
    m, d = x.shape
    pd = p.shape[1]
    assert m % tm == 0
    blocks = (2 * _nbytes((tm, d), F32) + _nbytes((d, d), BF16) + _nbytes((tm, pd), F32)
              + _nbytes((pd, d), BF16))
    return pl.pallas_call(
        _ple_body,
        grid=(m // tm,),
        in_specs=[
            pl.BlockSpec((tm, d), lambda i: (i, 0)),
            pl.BlockSpec((1, d), lambda i: (0, 0)),
            pl.BlockSpec((d, d), lambda i: (0, 0)),
            pl.BlockSpec((tm, pd), lambda i: (i, 0)),
            pl.BlockSpec((pd, d), lambda i: (0, 0)),
        ],
        out_specs=pl.BlockSpec((tm, d), lambda i: (i, 0)),
        out_shape=jax.ShapeDtypeStruct((m, d), F32),
        compiler_params=pltpu.CompilerParams(
            dimension_semantics=("parallel",),
            vmem_limit_bytes=_vmem_limit(blocks, 0, 4 * _nbytes((tm, d), F32))),
        name="ple",
    )(x, gain.reshape(1, d), w_gate, p, w_proj)


def _pick(m, candidates):
    for c in candidates:
        if m % c == 0:
            return c
    raise ValueError(f"no tile for {m}")


LATE_WEIGHTS = ("ffn2_gate", "ffn2_up", "ffn2_down", "w_out", "ple_gate")


def _trunk_layer(x, p, lw, ctx_k, ctx_v, left_valid, shift_prev, wkv_prev):
    b, t, d = x.shape
    m = b * t
    tm = _pick(m, (512, 256, 128))
    tm_big = _pick(m, (1024, 512, 256, 128))
    tf = 512
    x2 = x.reshape(m, d)
    att_w = lw["w_out"].shape[0] // 2
    heads = att_w // ATT_HEAD_DIM

    h = _ffn(x2, lw["norm_ffn1"], lw["ffn1_gate"], lw["ffn1_up"], lw["ffn1_down"], tm=tm_big, tf=tf)
    proj = _proj(h, lw["norm_mix"], lw["w_in"], tm=tm_big, tn=lw["w_in"].shape[1] // 5)
    proj3 = proj.reshape(b, t, -1)

    clen = _pick(t, (CHUNK, 16))
    nseq = _pick(b, (2, 1))
    tq = _pick(t, (2 * CHUNK, CHUNK, 16))
    group = _pick(t // tq, (4, 2, 1))
    att_grid, wkv_grid = (b, heads), (b // nseq, t // clen)
    pending = [name for name in LATE_WEIGHTS if lw[name].dtype != BF16]
    in_wkv = [name for name in pending if _slab_spec(lw[name], wkv_grid)]
    in_att = [name for name in pending if name not in in_wkv and _slab_spec(lw[name], att_grid)]
    lw = {**lw, **{name: lw[name].astype(BF16) for name in pending if name not in in_wkv + in_att}}

    att, k_keep, copies = _attention(proj3, ctx_k, ctx_v, lw["q_norm"], lw["k_norm"],
                                     _band_bias_vec(lw["rel_bias"], tq), heads=heads, tq=tq, chunk=min(CHUNK, tq),
                                     left_valid=left_valid, group=group, casts=[lw[name] for name in in_att])
    lw = {**lw, **dict(zip(in_att, copies))}
    rw, wkv_new, copies = _wkv(proj3, shift_prev, wkv_prev, lw["rwkv_mu"], lw["rwkv_w0"], lw["rwkv_w_lora"],
                               lw["rwkv_a0"], lw["rwkv_a_lora"], lw["rwkv_g_lora"], lw["rwkv_k_k"], lw["rwkv_k_a"],
                               lw["rwkv_r_k"], lw["rwkv_lnx_w"], lw["rwkv_lnx_b"], clen=clen, nseq=nseq,
                               casts=[lw[name] for name in in_wkv])
    lw = {**lw, **dict(zip(in_wkv, copies))}

    h = _outproj(h, att.reshape(m, att_w), rw.reshape(m, -1), lw["w_out"], tm=tm)
    h = _ffn(h, lw["norm_ffn2"], lw["ffn2_gate"], lw["ffn2_up"], lw["ffn2_down"], tm=tm_big, tf=tf)
    h = _ple(h, lw["norm_ple"], lw["ple_gate"], p.reshape(m, -1), lw["ple_proj"], tm=tm)

    keep = k_keep.shape[1]
    k_keep = k_keep.reshape(b, keep, heads, ATT_HEAD_DIM)
    v_keep = proj3[:, t - keep:, 2 * att_w:3 * att_w].reshape(b, keep, heads, ATT_HEAD_DIM)
    return h.reshape(b, t, d), (k_keep, v_keep, wkv_new, proj3[:, -1:, 3 * att_w:]), lw


def kernel(x_prompt, x_sample, cache_att_k, cache_att_v, state_wkv, state_shift, p_prompt, p_sample, norm_ffn1, ffn1_gate, ffn1_up, ffn1_down, norm_mix, w_in, q_norm, k_norm, rel_bias, rwkv_mu, rwkv_w0, rwkv_w_lora, rwkv_a0, rwkv_a_lora, rwkv_g_lora, rwkv_k_k, rwkv_k_a, rwkv_r_k, rwkv_lnx_w, rwkv_lnx_b, w_out, norm_ffn2, ffn2_gate, ffn2_up, ffn2_down, norm_ple, ple_gate, ple_proj):
    depth = norm_ffn1.shape[0]
    hp, hs = x_prompt, x_sample
    bp = hp.shape[0]
    att_w = w_out.shape[1] // 2
    rwkv_heads = rwkv_r_k.shape[1]
    rwkv_pw = rwkv_mu.shape[1]
    prompt_states, sample_states = [], []
    for i in range(depth):
        lw = {
            "norm_ffn1": norm_ffn1[i], "ffn1_gate": ffn1_gate[i].astype(BF16), "ffn1_up": ffn1_up[i].astype(BF16),
            "ffn1_down": ffn1_down[i].astype(BF16), "norm_mix": norm_mix[i], "w_in": w_in[i].astype(BF16),
            "q_norm": q_norm[i], "k_norm": k_norm[i],
            "rel_bias": rel_bias[i], "rwkv_mu": rwkv_mu[i], "rwkv_w0": rwkv_w0[i],
            "rwkv_w_lora": rwkv_w_lora[i].astype(BF16), "rwkv_a0": rwkv_a0[i],
            "rwkv_a_lora": rwkv_a_lora[i].astype(BF16), "rwkv_g_lora": rwkv_g_lora[i].astype(BF16),
            "rwkv_k_k": rwkv_k_k[i], "rwkv_k_a": rwkv_k_a[i], "rwkv_r_k": rwkv_r_k[i].reshape(-1),
            "rwkv_lnx_w": rwkv_lnx_w[i], "rwkv_lnx_b": rwkv_lnx_b[i],
            "w_out": w_out[i], "norm_ffn2": norm_ffn2[i], "ffn2_gate": ffn2_gate[i], "ffn2_up": ffn2_up[i],
            "ffn2_down": ffn2_down[i], "norm_ple": norm_ple[i], "ple_gate": ple_gate[i],
            "ple_proj": ple_proj[i].astype(BF16),
        }
        zero_ctx = jnp.zeros((1, LEFT, att_w // ATT_HEAD_DIM, ATT_HEAD_DIM), F32)
        zero_shift = jnp.zeros((bp, 1, rwkv_pw), hp.dtype)
        zero_wkv = jnp.zeros((bp, rwkv_heads, RWKV_HEAD_DIM, RWKV_HEAD_DIM), hp.dtype)
        hp, sp, lw = _trunk_layer(hp, p_prompt[i], lw, zero_ctx, zero_ctx, False, zero_shift, zero_wkv)
        hs, ss, _ = _trunk_layer(hs, p_sample[i], lw, cache_att_k[i], cache_att_v[i], True, state_shift[i],
                                 state_wkv[i])
        prompt_states.append(sp)
        sample_states.append(ss)
    stack = lambda states, j: jnp.stack([s[j] for s in states])
    return (hp, hs, stack(prompt_states, 0), stack(prompt_states, 1), stack(prompt_states, 2),
            stack(prompt_states, 3), stack(sample_states, 0), stack(sample_states, 1),
            stack(sample_states, 2), stack(sample_states, 3))
```

```python
import functools
import math

import jax
import jax.numpy as jnp
from jax import lax
from jax.experimental import pallas as pl
from jax.experimental.pallas import tpu as pltpu

F32 = jnp.float32
BF16 = jnp.bfloat16

CHUNK = 64
LEFT = 512
ATT_HEAD_DIM = 128
REL_CLIP = 256
RWKV_HEAD_DIM = 64
W_LORA = 64
A_LORA = 64
G_LORA = 128
RMS_EPS = 1e-6
GN_EPS = 64e-5
NEG_INF = -1e30

LANES = 128
SUBLANES = 8
V7X_VMEM_BYTES = 64 * 1024 * 1024
VMEM_LIMIT_CAP = V7X_VMEM_BYTES - 8 * 1024 * 1024


def _vmem_limit(block_bytes, scratch_bytes, temp_bytes):
    need = 2 * block_bytes + scratch_bytes + temp_bytes + (4 << 20)
    return int(min(max(need, 16 << 20), VMEM_LIMIT_CAP))


def _nbytes(shape, dtype):
    return math.prod(shape) * jnp.dtype(dtype).itemsize


def _rms_norm_rows(x, gain):
    ms = jnp.mean(x * x, axis=-1, keepdims=True)
    return x * lax.rsqrt(ms + RMS_EPS) * gain


def _mm(a, b):
    return jnp.dot(a, b, preferred_element_type=F32)


def _mm_nt(a, b):
    return lax.dot_general(a, b, (((1,), (1,)), ((), ())), preferred_element_type=F32)


def _mm_tn(a, b):
    return lax.dot_general(a, b, (((0,), (0,)), ((), ())), preferred_element_type=F32)


def _slab_spec(array, grid):
    n_i, n_j = grid
    rows = array.shape[0] // (n_i * n_j)
    if array.shape[0] % (n_i * n_j) or rows % (2 * SUBLANES):
        return None
    return pl.BlockSpec((rows, array.shape[1]), lambda i, j: (i * n_j + j, 0))


def _cast_slabs(src_refs, dst_refs):
    for src, dst in zip(src_refs, dst_refs):
        dst[...] = src[...].astype(BF16)


def _ffn_body(x_ref, g_ref, wg_ref, wu_ref, wd_ref, o_ref, xn_ref):
    @pl.when(pl.program_id(1) == 0)
    def _():
        x = x_ref[...]
        xn_ref[...] = _rms_norm_rows(x, g_ref[...]).astype(BF16)
        o_ref[...] = x

    xn = xn_ref[...]
    half = wg_ref.shape[1] // 2
    cols = (slice(0, half), slice(half, 2 * half))
    au = [(_mm(xn, wg_ref[:, c]), _mm(xn, wu_ref[:, c])) for c in cols]
    hs = [(a * jax.nn.sigmoid(a) * u).astype(BF16) for a, u in au]
    o_ref[...] += 0.5 * (_mm(hs[0], wd_ref[cols[0], :]) + _mm(hs[1], wd_ref[cols[1], :]))


def _ffn(x, gain, wg, wu, wd, *, tm, tf):
    m, d = x.shape
    f = wg.shape[1]
    assert m % tm == 0 and f % tf == 0
    blocks = (_nbytes((tm, d), F32) * 2 + _nbytes((d, tf), BF16) * 3)
    temps = _nbytes((tm, tf), F32) * 4
    return pl.pallas_call(
        _ffn_body,
        grid=(m // tm, f // tf),
        in_specs=[
            pl.BlockSpec((tm, d), lambda i, j: (i, 0)),
            pl.BlockSpec((1, d), lambda i, j: (0, 0)),
            pl.BlockSpec((d, tf), lambda i, j: (0, j)),
            pl.BlockSpec((d, tf), lambda i, j: (0, j)),
            pl.BlockSpec((tf, d), lambda i, j: (j, 0)),
        ],
        out_specs=pl.BlockSpec((tm, d), lambda i, j: (i, 0)),
        out_shape=jax.ShapeDtypeStruct((m, d), F32),
        scratch_shapes=[pltpu.VMEM((tm, d), BF16)],
        compiler_params=pltpu.CompilerParams(
            dimension_semantics=("parallel", "arbitrary"),
            vmem_limit_bytes=_vmem_limit(blocks, _nbytes((tm, d), BF16), temps)),
        name="ffn",
    )(x, gain.reshape(1, d), wg, wu, wd)


def _proj_body(x_ref, g_ref, w_ref, o_ref, xn_ref):
    @pl.when(pl.program_id(1) == 0)
    def _():
        xn_ref[...] = _rms_norm_rows(x_ref[...], g_ref[...]).astype(BF16)

    o_ref[...] = _mm(xn_ref[...], w_ref[...])


def _proj(x, gain, w, *, tm, tn):
    m, d = x.shape
    n = w.shape[1]
    assert m % tm == 0 and n % tn == 0
    blocks = _nbytes((tm, d), F32) + _nbytes((d, tn), BF16) + _nbytes((tm, tn), F32)
    return pl.pallas_call(
        _proj_body,
        grid=(m // tm, n // tn),
        in_specs=[
            pl.BlockSpec((tm, d), lambda i, j: (i, 0)),
            pl.BlockSpec((1, d), lambda i, j: (0, 0)),
            pl.BlockSpec((d, tn), lambda i, j: (0, j)),
        ],
        out_specs=pl.BlockSpec((tm, tn), lambda i, j: (i, j)),
        out_shape=jax.ShapeDtypeStruct((m, n), F32),
        scratch_shapes=[pltpu.VMEM((tm, d), BF16)],
        compiler_params=pltpu.CompilerParams(
            dimension_semantics=("parallel", "arbitrary"),
            vmem_limit_bytes=_vmem_limit(blocks, _nbytes((tm, d), BF16), 2 * _nbytes((tm, tn), F32))),
        name="proj",
    )(x, gain.reshape(1, d), w)


def _attn_body(*refs, seq, tq, chunk, left_valid, group, n_cast, heads):
    q_ref, k_ref, v_ref, ck_ref, cv_ref, qg_ref, kg_ref, bvec_ref = refs[:8]
    o_ref, kn_ref, vk_ref = refs[8 + n_cast:11 + n_cast]
    kp_ref, vp_ref, bias_ref = refs[11 + 2 * n_cast:]
    _cast_slabs(refs[8:8 + n_cast], refs[11 + n_cast:11 + 2 * n_cast])

    keep = kn_ref.shape[1] // heads
    kn = _rms_norm_rows(k_ref[0], kg_ref[...])
    kept = pl.ds(pl.program_id(1), keep, stride=heads)
    kn_ref[0, kept, :] = kn[seq - keep:, :]
    vk_ref[0, kept, :] = v_ref[0, seq - keep:, :]
    if left_valid:
        mine = pl.ds(pl.program_id(1), LEFT, stride=heads)
        kp_ref[0:LEFT, :] = ck_ref[0, mine, :].astype(BF16)
        vp_ref[0:LEFT, :] = cv_ref[0, mine, :].astype(BF16)
    else:
        kp_ref[0:LEFT, :] = jnp.zeros((LEFT, ATT_HEAD_DIM), BF16)
        vp_ref[0:LEFT, :] = jnp.zeros((LEFT, ATT_HEAD_DIM), BF16)
    kp_ref[LEFT:LEFT + seq, :] = kn.astype(BF16)
    vp_ref[LEFT:LEFT + seq, :] = v_ref[0].astype(BF16)

    width = LEFT + tq
    bw = bvec_ref.shape[-1]
    bias = pltpu.roll(jnp.broadcast_to(bvec_ref[0], (tq, bw)), 0, 1, stride=1, stride_axis=0)[:, :width]
    if tq > chunk:
        row = lax.broadcasted_iota(jnp.int32, (tq, width), 0)
        col = lax.broadcasted_iota(jnp.int32, (tq, width), 1)
        chunk_start = (row // chunk) * chunk
        in_band = (col >= chunk_start) & (col < chunk_start + LEFT + chunk)
        bias = jnp.where(in_band, bias, NEG_INF)
    bias_ref[...] = bias
    scale = ATT_HEAD_DIM ** -0.5

    n_chunks = tq // chunk
    spans = []
    for r in range(n_chunks):
        lo = (r * chunk) // LANES * LANES
        hi = min(width, -(-(r * chunk + LEFT + chunk) // LANES) * LANES)
        spans.append((lo, hi))

    def block_bases(t):
        return [pl.multiple_of((t * group + u) * tq, tq) for u in range(group)]

    def blocks(t, mask_left):
        bases = block_bases(t)
        queries = [_rms_norm_rows(q_ref[0, pl.ds(base, tq), :], qg_ref[...]).astype(BF16) for base in bases]
        scores = [_mm_nt(q, kp_ref[pl.ds(base, width), :]) for q, base in zip(queries, bases)]
        pieces = [(u, r) for u in range(group) for r in range(n_chunks)]
        masked_scores = []
        for u, r in pieces:
            lo, hi = spans[r]
            rs = slice(r * chunk, (r + 1) * chunk)
            tiles = []
            for c0 in range(lo, hi, LANES):
                c1 = min(c0 + LANES, hi)
                band = bias_ref[rs, c0:c1]
                s = scores[u][rs, c0:c1] * scale + band
                inside = r * chunk <= c0 and c1 <= r * chunk + LEFT + chunk
                if mask_left or not inside:
                    ok = band > 0.5 * NEG_INF
                    if mask_left:
                        col = lax.broadcasted_iota(jnp.int32, (chunk, c1 - c0), 1) + c0
                        ok = ok & (col >= LEFT - bases[u])
                    s = jnp.where(ok, s, NEG_INF)
                tiles.append(s)
            masked_scores.append(jnp.concatenate(tiles, axis=1) if len(tiles) > 1 else tiles[0])
        maxes = [jnp.max(s, axis=-1, keepdims=True) for s in masked_scores]
        exps = [jnp.exp(s - m) for s, m in zip(masked_scores, maxes)]
        invs = [1.0 / jnp.sum(e, axis=-1, keepdims=True) for e in exps]
        rows = []
        for (u, r), e in zip(pieces, exps):
            lo, hi = spans[r]
            parts = ([jnp.zeros((chunk, lo), BF16)] if lo else []) + [e.astype(BF16)]
            parts += [jnp.zeros((chunk, width - hi), BF16)] if hi < width else []
            rows.append(jnp.concatenate(parts, axis=1) if len(parts) > 1 else parts[0])
        for u, base in enumerate(bases):
            mine = slice(u * n_chunks, (u + 1) * n_chunks)
            w = jnp.concatenate(rows[mine], axis=0) if n_chunks > 1 else rows[mine][0]
            inv = jnp.concatenate(invs[mine], axis=0) if n_chunks > 1 else invs[mine][0]
            o_ref[0, pl.ds(base, tq), :] = (_mm(w, vp_ref[pl.ds(base, width), :]) * inv).astype(o_ref.dtype)

    n_iter = seq // (tq * group)
    n_left = 0 if left_valid else min(n_iter, -(-LEFT // (tq * group)))

    def masked(t, carry):
        blocks(t, True)
        return carry

    def plain(t, carry):
        blocks(t, False)
        return carry

    lax.fori_loop(0, n_left, masked, 0)
    lax.fori_loop(n_left, n_iter, plain, 0)


def _attention(proj, ctx_k, ctx_v, q_gain, k_gain, bias_vec, *, heads, tq, chunk, left_valid, group, casts=()):
    b, seq, _ = proj.shape
    dh = ATT_HEAD_DIM
    bw = bias_vec.shape[-1]
    keep = min(seq, LEFT)
    assert seq % (tq * group) == 0 and tq % chunk == 0 and ctx_k.shape[1:] == (LEFT, heads, dh)
    per_batch_ctx = ctx_k.shape[0] == b
    ctx_map = (lambda bi, h: (bi, 0, 0)) if per_batch_ctx else (lambda bi, h: (0, 0, 0))
    ctx_k, ctx_v = (c.reshape(c.shape[0], LEFT * heads, dh) for c in (ctx_k, ctx_v))
    slabs = [_slab_spec(a, (b, heads)) for a in casts]
    blocks = (4 * _nbytes((seq, dh), F32) + 2 * _nbytes((LEFT, heads, dh), F32) + _nbytes((8, bw), F32)
              + 2 * _nbytes((keep * heads, dh), F32)
              + sum(_nbytes(s.block_shape, F32) + _nbytes(s.block_shape, BF16) for s in slabs))
    scratch = (2 * _nbytes((LEFT + seq, dh), BF16)
               + _nbytes((tq, LEFT + tq), F32))
    outs = pl.pallas_call(
        functools.partial(_attn_body, seq=seq, tq=tq, chunk=chunk, left_valid=left_valid, group=group,
                          n_cast=len(casts), heads=heads),
        grid=(b, heads),
        in_specs=[
            pl.BlockSpec((1, seq, dh), lambda bi, h: (bi, 0, h)),
            pl.BlockSpec((1, seq, dh), lambda bi, h: (bi, 0, heads + h)),
            pl.BlockSpec((1, seq, dh), lambda bi, h: (bi, 0, 2 * heads + h)),
            pl.BlockSpec((1, LEFT * heads, dh), ctx_map),
            pl.BlockSpec((1, LEFT * heads, dh), ctx_map),
            pl.BlockSpec((1, dh), lambda bi, h: (0, 0)),
            pl.BlockSpec((1, dh), lambda bi, h: (0, 0)),
            pl.BlockSpec((1, 1, bw), lambda bi, h: (h, 0, 0)),
        ] + slabs,
        out_specs=[
            pl.BlockSpec((1, seq, dh), lambda bi, h: (bi, 0, h)),
            pl.BlockSpec((1, keep * heads, dh), lambda bi, h: (bi, 0, 0)),
            pl.BlockSpec((1, keep * heads, dh), lambda bi, h: (bi, 0, 0)),
        ] + [_slab_spec(a, (b, heads)) for a in casts],
        out_shape=[jax.ShapeDtypeStruct((b, seq, heads * dh), BF16),
                   jax.ShapeDtypeStruct((b, keep * heads, dh), F32),
                   jax.ShapeDtypeStruct((b, keep * heads, dh), F32)]
                  + [jax.ShapeDtypeStruct(a.shape, BF16) for a in casts],
        scratch_shapes=[pltpu.VMEM((LEFT + seq, dh), BF16),
                        pltpu.VMEM((LEFT + seq, dh), BF16), pltpu.VMEM((tq, LEFT + tq), F32)],
        compiler_params=pltpu.CompilerParams(
            dimension_semantics=("parallel", "arbitrary"),
            vmem_limit_bytes=_vmem_limit(blocks, scratch, 8 * group * _nbytes((tq, LEFT + tq), F32)
                                         + 3 * _nbytes((seq, dh), F32))),
        name="attn",
    )(proj, proj, proj, ctx_k, ctx_v, q_gain.reshape(1, dh), k_gain.reshape(1, dh), bias_vec, *casts)
    kept = [o.reshape(b, keep, heads, dh) for o in outs[1:3]]
    return outs[0], kept[0], kept[1], list(outs[3:])


def _band_bias_vec(rel_bias, tq):
    width = LEFT + tq
    bw = -(-(width + tq) // LANES) * LANES
    m = jnp.arange(bw)
    d = jnp.where(m < width, m, m - bw)
    idx = jnp.clip(LEFT - d, -REL_CLIP, REL_CLIP) + REL_CLIP
    return rel_bias[idx].T.astype(F32)[:, None, :]


def _split3(x):
    h1 = x.astype(BF16)
    r1 = x - h1.astype(F32)
    h2 = r1.astype(BF16)
    h3 = (r1 - h2.astype(F32)).astype(BF16)
    return h1, h2, h3


def _wkv_body(*refs, clen, heads, col0, nseq, n_cast):
    (x_ref, sp_ref, s0_ref, mu_ref, w0_ref, wl_ref, a0_ref, al_ref, gl_ref, kk_ref, ka_ref, rk_ref, lnw_ref,
     lnb_ref) = refs[:14]
    o_ref, so_ref = refs[14 + n_cast:16 + n_cast]
    carry_ref, s_ref = refs[16 + 2 * n_cast:]
    _cast_slabs(refs[14:14 + n_cast], refs[16 + n_cast:16 + 2 * n_cast])
    c = pl.program_id(1)
    n = RWKV_HEAD_DIM
    width = heads * n
    pairs = heads // 2
    pw = 2 * n
    tw = 2 * clen
    bf = lambda t: t.astype(BF16)

    @pl.when(c == 0)
    def _():
        zeros = jnp.zeros((n, n), F32)
        for i in range(nseq):
            carry_ref[i] = sp_ref[i]
            for q in range(pairs):
                s_ref[i * pairs + q] = jnp.concatenate(
                    [jnp.concatenate([s0_ref[i, 2 * q], zeros], axis=1),
                     jnp.concatenate([zeros, s0_ref[i, 2 * q + 1]], axis=1)], axis=0)

    trow = lax.broadcasted_iota(jnp.int32, (SUBLANES, 1), 0)
    ti = lax.broadcasted_iota(jnp.int32, (clen, clen), 0)
    si = lax.broadcasted_iota(jnp.int32, (clen, clen), 1)
    tri = (si <= ti).astype(BF16)
    levels = max(1, (clen - 1).bit_length())

    def prepare(i):
        xb = x_ref[i, :, col0:]
        rolled = pltpu.roll(xb, 1, 0)
        shifted = jnp.concatenate([jnp.where(trow == 0, carry_ref[i], rolled[:SUBLANES]), rolled[SUBLANES:]],
                                  axis=0)
        carry_ref[i] = xb[clen - 1:clen, :]
        xm = xb + mu_ref[...] * (shifted - xb)

        r = xm[:, 0:width]
        k = xm[:, width:2 * width]
        v = xm[:, 2 * width:3 * width]
        o3 = 3 * width
        w_lo = xm[:, o3:o3 + W_LORA]
        a_lo = xm[:, o3 + W_LORA:o3 + W_LORA + A_LORA]
        g_lo = xm[:, o3 + W_LORA + A_LORA:o3 + W_LORA + A_LORA + G_LORA]

        log_decay = -math.exp(-0.5) * jax.nn.sigmoid(w0_ref[...] + _mm(bf(jnp.tanh(w_lo)), wl_ref[...]))
        a = jax.nn.sigmoid(a0_ref[...] + _mm(bf(a_lo), al_ref[...]))
        g = _mm(bf(jax.nn.sigmoid(g_lo)), gl_ref[...])
        kk = k * kk_ref[...]
        k = k * (1.0 + (a - 1.0) * ka_ref[...])

        d1, d2, d3 = _split3(log_decay)
        cum = _mm(tri, d1) + _mm(tri, d2) + _mm(tri, d3)
        g_in = jnp.exp(cum)
        g_inv = jnp.exp(-cum)
        g_last = g_in[clen - 1:clen, :]
        k_h = k * g_inv
        return dict(v=v, a=a, g=g, kk=kk, rk=r * k * rk_ref[...], g_inv=g_inv, g_ex=jnp.exp(cum - log_decay),
                    g_last=g_last, r_t=r * g_in, k_h=k_h, k_l=k_h * g_last)

    pre = [prepare(i) for i in range(nseq)]

    first = lax.broadcasted_iota(jnp.int32, (1, pw), 1) < n
    first_t = lax.broadcasted_iota(jnp.int32, (1, tw), 1) < clen
    row2 = lax.broadcasted_iota(jnp.int32, (2 * clen, 2 * tw), 0)
    spos = lax.broadcasted_iota(jnp.int32, (2 * clen, 2 * tw), 1) % clen
    causal = spos < jnp.where(row2 < clen, row2, row2 - clen + 1)
    same_head = ((lax.broadcasted_iota(jnp.int32, (pw, pw), 0) < n)
                 == (lax.broadcasted_iota(jnp.int32, (pw, pw), 1) < n))

    def head_sums(x):
        s0 = jnp.sum(jnp.where(first, x, 0.0), axis=-1, keepdims=True)
        s1 = jnp.sum(jnp.where(first, 0.0, x), axis=-1, keepdims=True)
        return jnp.where(first, s0, s1)

    def block_diag(x, mask):
        zero = jnp.zeros_like(x)
        return jnp.concatenate([jnp.where(mask, x, zero), jnp.where(mask, zero, x)], axis=0)

    units = [(i, q) for i in range(nseq) for q in range(pairs)]
    us = range(len(units))
    cols = [slice(q * pw, (q + 1) * pw) for _, q in units]
    val = lambda name, u: pre[units[u][0]][name][:, cols[u]]

    b_hat, a_t, v_bd, s_prev, s_bf = [], [], [], [], []
    for u in us:
        kk_u = val("kk", u)
        kk_u = kk_u * jnp.minimum(lax.rsqrt(head_sums(kk_u * kk_u)), 1e12)
        b_hat.append(kk_u * val("a", u) * val("g_inv", u))
        a_t.append(-kk_u * val("g_ex", u))
        v_bd.append(block_diag(bf(val("v", u)), first))
        s_prev.append(s_ref[u])
        s_bf.append(bf(s_prev[u]))

    ar = [bf(jnp.concatenate([a_t[u], val("r_t", u)], axis=0)) for u in us]
    p = [_mm_nt(ar[u], jnp.concatenate([block_diag(bf(b_hat[u]), first),
                                        block_diag(bf(val("k_h", u)), first)], axis=0)) for u in us]
    pm = [bf(jnp.where(causal, p[u], 0.0)) for u in us]
    nk = [pm[u][:clen, :tw] for u in us]
    a_rb = [pm[u][clen:, :tw] for u in us]

    sv = [_mm_nt(ar[u], s_bf[u]) + _mm(pm[u][:, tw:], v_bd[u]) for u in us]
    sa = [sv[u][:clen] for u in us]
    for lvl in range(levels):
        if lvl + 1 < levels:
            both = [_mm(nk[u], jnp.concatenate([block_diag(bf(sa[u]), first), block_diag(nk[u], first_t)],
                                               axis=1)) for u in us]
            sa = [sa[u] + both[u][:, :pw] for u in us]
            nk = [bf(both[u][:, pw:]) for u in us]
        else:
            sa = [sa[u] + _mm(nk[u], block_diag(bf(sa[u]), first)) for u in us]

    sa_bf = [bf(sa[u]) for u in us]
    ys = [sv[u][clen:] + _mm(a_rb[u], block_diag(sa_bf[u], first)) for u in us]
    for u in us:
        g_last = val("g_last", u)
        upd = _mm_tn(jnp.concatenate([sa_bf[u], bf(val("v", u))], axis=0),
                     jnp.concatenate([bf(b_hat[u] * g_last), bf(val("k_l", u))], axis=0))
        s_ref[u] = s_prev[u] * g_last + jnp.where(same_head, upd, 0.0)

    outs = []
    inv_n = 1.0 / n
    for u in us:
        y = ys[u]
        mean = head_sums(y) * inv_n
        var = head_sums(jnp.square(y - mean)) * inv_n
        y = (y - mean) * lax.rsqrt(var + GN_EPS)
        y = y * lnw_ref[:, cols[u]] + lnb_ref[:, cols[u]]
        y = y + head_sums(val("rk", u)) * val("v", u)
        outs.append(y)

    for i in range(nseq):
        o_ref[i] = (jnp.concatenate(outs[i * pairs:(i + 1) * pairs], axis=1) * pre[i]["g"]).astype(o_ref.dtype)

    @pl.when(c == pl.num_programs(1) - 1)
    def _():
        for i in range(nseq):
            for q in range(pairs):
                so_ref[i, 2 * q] = s_ref[i * pairs + q, 0:n, 0:n]
                so_ref[i, 2 * q + 1] = s_ref[i * pairs + q, n:pw, n:pw]


def _wkv(proj, shift_prev, state0, mu, w0, w_lora, a0, a_lora, g_lora, k_k, k_a, r_k, lnx_w, lnx_b, *, clen,
         nseq, casts=()):
    b, t, full_w = proj.shape
    heads = state0.shape[1]
    n = RWKV_HEAD_DIM
    width = heads * n
    pw = shift_prev.shape[-1]
    col0 = full_w - pw
    assert t % clen == 0 and pw == 3 * width + W_LORA + A_LORA + G_LORA and heads % 2 == 0
    assert col0 % LANES == 0 and 2 * n == LANES and b % nseq == 0
    row = lambda a: a.reshape(1, -1)
    full = lambda shape: pl.BlockSpec(shape, lambda bi, c: (0,) * len(shape))
    grid = (b // nseq, t // clen)
    slabs = [_slab_spec(a, grid) for a in casts]
    blocks = (nseq * (_nbytes((clen, full_w), F32) + _nbytes((clen, width), F32)
                      + 2 * _nbytes((heads, n, LANES), F32))
              + sum(_nbytes(s.block_shape, F32) + _nbytes(s.block_shape, BF16) for s in slabs))
    state_scratch = _nbytes((nseq * heads // 2, 2 * n, 2 * n), F32)
    outs = pl.pallas_call(
        functools.partial(_wkv_body, clen=clen, heads=heads, col0=col0, nseq=nseq, n_cast=len(casts)),
        grid=grid,
        in_specs=[
            pl.BlockSpec((nseq, clen, full_w), lambda bi, c: (bi, c, 0)),
            pl.BlockSpec((nseq, 1, pw), lambda bi, c: (bi, 0, 0)),
            pl.BlockSpec((nseq, heads, n, n), lambda bi, c: (bi, 0, 0, 0)),
            full((1, pw)), full((1, width)), full((W_LORA, width)), full((1, width)), full((A_LORA, width)),
            full((G_LORA, width)), full((1, width)), full((1, width)), full((1, width)), full((1, width)),
            full((1, width)),
        ] + slabs,
        out_specs=[
            pl.BlockSpec((nseq, clen, width), lambda bi, c: (bi, c, 0)),
            pl.BlockSpec((nseq, heads, n, n), lambda bi, c: (bi, 0, 0, 0)),
        ] + [_slab_spec(a, grid) for a in casts],
        out_shape=[jax.ShapeDtypeStruct((b, t, width), BF16),
                   jax.ShapeDtypeStruct((b, heads, n, n), F32)]
                  + [jax.ShapeDtypeStruct(a.shape, BF16) for a in casts],
        scratch_shapes=[pltpu.VMEM((nseq, 1, pw), F32), pltpu.VMEM((nseq * heads // 2, 2 * n, 2 * n), F32)],
        compiler_params=pltpu.CompilerParams(
            dimension_semantics=("parallel", "arbitrary"),
            vmem_limit_bytes=_vmem_limit(blocks, state_scratch, 40 * nseq * _nbytes((clen, width), F32))),
        name="wkv",
    )(proj, shift_prev, state0, row(mu), row(w0), w_lora, row(a0), a_lora, g_lora, row(k_k), row(k_a),
      row(r_k), row(lnx_w), row(lnx_b), *casts)
    return outs[0], outs[1], list(outs[2:])


def _outproj_body(h_ref, att_ref, rw_ref, wa_ref, wr_ref, o_ref):
    o_ref[...] = h_ref[...] + _mm(att_ref[...], wa_ref[...]) + _mm(rw_ref[...], wr_ref[...])


def _outproj(h, att, rw, w_out, *, tm):
    m, d = h.shape
    ka, kr = att.shape[1], rw.shape[1]
    assert m % tm == 0 and ka == kr and w_out.shape == (ka + kr, d)
    blocks = (2 * _nbytes((tm, d), F32) + _nbytes((tm, ka), BF16) + _nbytes((tm, kr), BF16)
              + _nbytes((ka, d), BF16) + _nbytes((kr, d), BF16))
    return pl.pallas_call(
        _outproj_body,
        grid=(m // tm,),
        in_specs=[
            pl.BlockSpec((tm, d), lambda i: (i, 0)),
            pl.BlockSpec((tm, ka), lambda i: (i, 0)),
            pl.BlockSpec((tm, kr), lambda i: (i, 0)),
            pl.BlockSpec((ka, d), lambda i: (0, 0)),
            pl.BlockSpec((kr, d), lambda i: (1, 0)),
        ],
        out_specs=pl.BlockSpec((tm, d), lambda i: (i, 0)),
        out_shape=jax.ShapeDtypeStruct((m, d), F32),
        compiler_params=pltpu.CompilerParams(
            dimension_semantics=("parallel",),
            vmem_limit_bytes=_vmem_limit(blocks, 0, 2 * _nbytes((tm, d), F32))),
        name="outproj",
    )(h, att, rw, w_out, w_out)


def _ple_body(x_ref, g_ref, wg_ref, p_ref, wp_ref, o_ref):
    x = x_ref[...]
    gate = jax.nn.sigmoid(_mm(_rms_norm_rows(x, g_ref[...]).astype(BF16), wg_ref[...]))
    o_ref[...] = x + gate * _mm(p_ref[...].astype(BF16), wp_ref[...])


def _ple(x, gain, w_gate, p, w_proj, *, tm):
    m, d = x.shape
    pd = p.shape[1]
    assert m % tm == 0
    blocks = (2 * _nbytes((tm, d), F32) + _nbytes((d, d), BF16) + _nbytes((tm, pd), F32)
              + _nbytes((pd, d), BF16))
    return pl.pallas_call(
        _ple_body,
        grid=(m // tm,),
        in_specs=[
            pl.BlockSpec((tm, d), lambda i: (i, 0)),
            pl.BlockSpec((1, d), lambda i: (0, 0)),
            pl.BlockSpec((d, d), lambda i: (0, 0)),
            pl.BlockSpec((tm, pd), lambda i: (i, 0)),
            pl.BlockSpec((pd, d), lambda i: (0, 0)),
        ],
        out_specs=pl.BlockSpec((tm, d), lambda i: (i, 0)),
        out_shape=jax.ShapeDtypeStruct((m, d), F32),
        compiler_params=pltpu.CompilerParams(
            dimension_semantics=("parallel",),
            vmem_limit_bytes=_vmem_limit(blocks, 0, 4 * _nbytes((tm, d), F32))),
        name="ple",
    )(x, gain.reshape(1, d), w_gate, p, w_proj)


def _pick(m, candidates):
    for c in candidates:
        if m % c == 0:
            return c
    raise ValueError(f"no tile for {m}")


LATE_WEIGHTS = ("ffn2_gate", "ffn2_up", "ffn2_down", "w_out", "ple_gate")


def _trunk_layer(x, p, lw, ctx_k, ctx_v, left_valid, shift_prev, wkv_prev):
    b, t, d = x.shape
    m = b * t
    tm = _pick(m, (512, 256, 128))
    tm_big = _pick(m, (1024, 512, 256, 128))
    tf = 512
    x2 = x.reshape(m, d)
    att_w = lw["w_out"].shape[0] // 2
    heads = att_w // ATT_HEAD_DIM

    h = _ffn(x2, lw["norm_ffn1"], lw["ffn1_gate"], lw["ffn1_up"], lw["ffn1_down"], tm=tm_big, tf=tf)
    proj = _proj(h, lw["norm_mix"], lw["w_in"], tm=tm_big, tn=lw["w_in"].shape[1] // 5)
    proj3 = proj.reshape(b, t, -1)

    clen = _pick(t, (CHUNK, 16))
    nseq = _pick(b, (2, 1))
    tq = _pick(t, (2 * CHUNK, CHUNK, 16))
    group = _pick(t // tq, (4, 2, 1))
    att_grid, wkv_grid = (b, heads), (b // nseq, t // clen)
    pending = [name for name in LATE_WEIGHTS if lw[name].dtype != BF16]
    in_wkv = [name for name in pending if _slab_spec(lw[name], wkv_grid)]
    in_att = [name for name in pending if name not in in_wkv and _slab_spec(lw[name], att_grid)]
    lw = {**lw, **{name: lw[name].astype(BF16) for name in pending if name not in in_wkv + in_att}}

    att, k_keep, v_keep, copies = _attention(
        proj3, ctx_k, ctx_v, lw["q_norm"], lw["k_norm"], _band_bias_vec(lw["rel_bias"], tq), heads=heads, tq=tq,
        chunk=min(CHUNK, tq), left_valid=left_valid, group=group, casts=[lw[name] for name in in_att])
    lw = {**lw, **dict(zip(in_att, copies))}
    rw, wkv_new, copies = _wkv(proj3, shift_prev, wkv_prev, lw["rwkv_mu"], lw["rwkv_w0"], lw["rwkv_w_lora"],
                               lw["rwkv_a0"], lw["rwkv_a_lora"], lw["rwkv_g_lora"], lw["rwkv_k_k"], lw["rwkv_k_a"],
                               lw["rwkv_r_k"], lw["rwkv_lnx_w"], lw["rwkv_lnx_b"], clen=clen, nseq=nseq,
                               casts=[lw[name] for name in in_wkv])
    lw = {**lw, **dict(zip(in_wkv, copies))}

    h = _outproj(h, att.reshape(m, att_w), rw.reshape(m, -1), lw["w_out"], tm=tm)
    h = _ffn(h, lw["norm_ffn2"], lw["ffn2_gate"], lw["ffn2_up"], lw["ffn2_down"], tm=tm_big, tf=tf)
    h = _ple(h, lw["norm_ple"], lw["ple_gate"], p.reshape(m, -1), lw["ple_proj"], tm=tm)

    return h.reshape(b, t, d), (k_keep, v_keep, wkv_new, proj3[:, -1:, 3 * att_w:]), lw


def kernel(x_prompt, x_sample, cache_att_k, cache_att_v, state_wkv, state_shift, p_prompt, p_sample, norm_ffn1, ffn1_gate, ffn1_up, ffn1_down, norm_mix, w_in, q_norm, k_norm, rel_bias, rwkv_mu, rwkv_w0, rwkv_w_lora, rwkv_a0, rwkv_a_lora, rwkv_g_lora, rwkv_k_k, rwkv_k_a, rwkv_r_k, rwkv_lnx_w, rwkv_lnx_b, w_out, norm_ffn2, ffn2_gate, ffn2_up, ffn2_down, norm_ple, ple_gate, ple_proj):
    depth = norm_ffn1.shape[0]
    hp, hs = x_prompt, x_sample
    bp = hp.shape[0]
    att_w = w_out.shape[1] // 2
    rwkv_heads = rwkv_r_k.shape[1]
    rwkv_pw = rwkv_mu.shape[1]
    prompt_states, sample_states = [], []
    for i in range(depth):
        lw = {
            "norm_ffn1": norm_ffn1[i], "ffn1_gate": ffn1_gate[i].astype(BF16), "ffn1_up": ffn1_up[i].astype(BF16),
            "ffn1_down": ffn1_down[i].astype(BF16), "norm_mix": norm_mix[i], "w_in": w_in[i].astype(BF16),
            "q_norm": q_norm[i], "k_norm": k_norm[i],
            "rel_bias": rel_bias[i], "rwkv_mu": rwkv_mu[i], "rwkv_w0": rwkv_w0[i],
            "rwkv_w_lora": rwkv_w_lora[i].astype(BF16), "rwkv_a0": rwkv_a0[i],
            "rwkv_a_lora": rwkv_a_lora[i].astype(BF16), "rwkv_g_lora": rwkv_g_lora[i].astype(BF16),
            "rwkv_k_k": rwkv_k_k[i], "rwkv_k_a": rwkv_k_a[i], "rwkv_r_k": rwkv_r_k[i].reshape(-1),
            "rwkv_lnx_w": rwkv_lnx_w[i], "rwkv_lnx_b": rwkv_lnx_b[i],
            "w_out": w_out[i], "norm_ffn2": norm_ffn2[i], "ffn2_gate": ffn2_gate[i], "ffn2_up": ffn2_up[i],
            "ffn2_down": ffn2_down[i], "norm_ple": norm_ple[i], "ple_gate": ple_gate[i],
            "ple_proj": ple_proj[i].astype(BF16),
        }
        zero_ctx = jnp.zeros((1, LEFT, att_w // ATT_HEAD_DIM, ATT_HEAD_DIM), F32)
        zero_shift = jnp.zeros((bp, 1, rwkv_pw), hp.dtype)
        zero_wkv = jnp.zeros((bp, rwkv_heads, RWKV_HEAD_DIM, RWKV_HEAD_DIM), hp.dtype)
        hp, sp, lw = _trunk_layer(hp, p_prompt[i], lw, zero_ctx, zero_ctx, False, zero_shift, zero_wkv)
        hs, ss, _ = _trunk_layer(hs, p_sample[i], lw, cache_att_k[i], cache_att_v[i], True, state_shift[i],
                                 state_wkv[i])
        prompt_states.append(sp)
        sample_states.append(ss)
    stack = lambda states, j: jnp.stack([s[j] for s in states])
    return (hp, hs, stack(prompt_states, 0), stack(prompt_states, 1), stack(prompt_states, 2),
            stack(prompt_states, 3), stack(sample_states, 0), stack(sample_states, 1),
            stack(sample_states, 2), stack(sample_states, 3))
```

```python
import functools
import math

import jax
import jax.numpy as jnp
from jax import lax
from jax.experimental import pallas as pl
from jax.experimental.pallas import tpu as pltpu

F32 = jnp.float32
BF16 = jnp.bfloat16

CHUNK = 64
LEFT = 512
ATT_HEAD_DIM = 128
REL_CLIP = 256
RWKV_HEAD_DIM = 64
W_LORA = 64
A_LORA = 64
G_LORA = 128
RMS_EPS = 1e-6
GN_EPS = 64e-5
NEG_INF = -1e30

LANES = 128
SUBLANES = 8
V7X_VMEM_BYTES = 64 * 1024 * 1024
VMEM_LIMIT_CAP = V7X_VMEM_BYTES - 8 * 1024 * 1024


def _vmem_limit(block_bytes, scratch_bytes, temp_bytes):
    need = 2 * block_bytes + scratch_bytes + temp_bytes + (4 << 20)
    return int(min(max(need, 16 << 20), VMEM_LIMIT_CAP))


def _nbytes(shape, dtype):
    return math.prod(shape) * jnp.dtype(dtype).itemsize


def _rms_norm_rows(x, gain):
    ms = jnp.mean(x * x, axis=-1, keepdims=True)
    return x * lax.rsqrt(ms + RMS_EPS) * gain


def _mm(a, b):
    return jnp.dot(a, b, preferred_element_type=F32)


def _mm_nt(a, b):
    return lax.dot_general(a, b, (((1,), (1,)), ((), ())), preferred_element_type=F32)


def _mm_tn(a, b):
    return lax.dot_general(a, b, (((0,), (0,)), ((), ())), preferred_element_type=F32)


def _slab_spec(array, grid):
    n_i, n_j = grid
    rows = array.shape[0] // (n_i * n_j)
    if array.shape[0] % (n_i * n_j) or rows % (2 * SUBLANES):
        return None
    return pl.BlockSpec((rows, array.shape[1]), lambda i, j: (i * n_j + j, 0))


def _cast_slabs(src_refs, dst_refs):
    for src, dst in zip(src_refs, dst_refs):
        dst[...] = src[...].astype(BF16)


def _ffn_body(x_ref, g_ref, wg_ref, wu_ref, wd_ref, o_ref, xn_ref):
    @pl.when(pl.program_id(1) == 0)
    def _():
        x = x_ref[...]
        xn_ref[...] = _rms_norm_rows(x, g_ref[...]).astype(BF16)
        o_ref[...] = x

    xn = xn_ref[...]
    half = wg_ref.shape[1] // 2
    cols = (slice(0, half), slice(half, 2 * half))
    au = [(_mm(xn, wg_ref[:, c]), _mm(xn, wu_ref[:, c])) for c in cols]
    hs = [(a * jax.nn.sigmoid(a) * u).astype(BF16) for a, u in au]
    o_ref[...] += 0.5 * (_mm(hs[0], wd_ref[cols[0], :]) + _mm(hs[1], wd_ref[cols[1], :]))


def _ffn(x, gain, wg, wu, wd, *, tm, tf):
    m, d = x.shape
    f = wg.shape[1]
    assert m % tm == 0 and f % tf == 0
    blocks = (_nbytes((tm, d), F32) * 2 + _nbytes((d, tf), BF16) * 3)
    temps = _nbytes((tm, tf), F32) * 4
    return pl.pallas_call(
        _ffn_body,
        grid=(m // tm, f // tf),
        in_specs=[
            pl.BlockSpec((tm, d), lambda i, j: (i, 0)),
            pl.BlockSpec((1, d), lambda i, j: (0, 0)),
            pl.BlockSpec((d, tf), lambda i, j: (0, j)),
            pl.BlockSpec((d, tf), lambda i, j: (0, j)),
            pl.BlockSpec((tf, d), lambda i, j: (j, 0)),
        ],
        out_specs=pl.BlockSpec((tm, d), lambda i, j: (i, 0)),
        out_shape=jax.ShapeDtypeStruct((m, d), F32),
        scratch_shapes=[pltpu.VMEM((tm, d), BF16)],
        compiler_params=pltpu.CompilerParams(
            dimension_semantics=("parallel", "arbitrary"),
            vmem_limit_bytes=_vmem_limit(blocks, _nbytes((tm, d), BF16), temps)),
        name="ffn",
    )(x, gain.reshape(1, d), wg, wu, wd)


def _proj_body(x_ref, g_ref, w_ref, o_ref, xn_ref):
    @pl.when(pl.program_id(1) == 0)
    def _():
        xn_ref[...] = _rms_norm_rows(x_ref[...], g_ref[...]).astype(BF16)

    o_ref[...] = _mm(xn_ref[...], w_ref[...])


def _proj(x, gain, w, *, tm, tn):
    m, d = x.shape
    n = w.shape[1]
    assert m % tm == 0 and n % tn == 0
    blocks = _nbytes((tm, d), F32) + _nbytes((d, tn), BF16) + _nbytes((tm, tn), F32)
    return pl.pallas_call(
        _proj_body,
        grid=(m // tm, n // tn),
        in_specs=[
            pl.BlockSpec((tm, d), lambda i, j: (i, 0)),
            pl.BlockSpec((1, d), lambda i, j: (0, 0)),
            pl.BlockSpec((d, tn), lambda i, j: (0, j)),
        ],
        out_specs=pl.BlockSpec((tm, tn), lambda i, j: (i, j)),
        out_shape=jax.ShapeDtypeStruct((m, n), F32),
        scratch_shapes=[pltpu.VMEM((tm, d), BF16)],
        compiler_params=pltpu.CompilerParams(
            dimension_semantics=("parallel", "arbitrary"),
            vmem_limit_bytes=_vmem_limit(blocks, _nbytes((tm, d), BF16), 2 * _nbytes((tm, tn), F32))),
        name="proj",
    )(x, gain.reshape(1, d), w)


def _attn_body(*refs, seq, tq, chunk, left_valid, group, n_cast, heads):
    q_ref, k_ref, v_ref, ck_ref, cv_ref, qg_ref, kg_ref, bvec_ref = refs[:8]
    o_ref, kn_ref, vk_ref = refs[8 + n_cast:11 + n_cast]
    kp_ref, vp_ref, bias_ref = refs[11 + 2 * n_cast:]
    _cast_slabs(refs[8:8 + n_cast], refs[11 + n_cast:11 + 2 * n_cast])

    keep = kn_ref.shape[1] // heads
    kn = _rms_norm_rows(k_ref[0], kg_ref[...])
    kept = pl.ds(pl.program_id(1), keep, stride=heads)
    kn_ref[0, kept, :] = kn[seq - keep:, :]
    vk_ref[0, kept, :] = v_ref[0, seq - keep:, :]
    if left_valid:
        mine = pl.ds(pl.program_id(1), LEFT, stride=heads)
        kp_ref[0:LEFT, :] = ck_ref[0, mine, :].astype(BF16)
        vp_ref[0:LEFT, :] = cv_ref[0, mine, :].astype(BF16)
    else:
        kp_ref[0:LEFT, :] = jnp.zeros((LEFT, ATT_HEAD_DIM), BF16)
        vp_ref[0:LEFT, :] = jnp.zeros((LEFT, ATT_HEAD_DIM), BF16)
    kp_ref[LEFT:LEFT + seq, :] = kn.astype(BF16)
    vp_ref[LEFT:LEFT + seq, :] = v_ref[0].astype(BF16)

    width = LEFT + tq
    bw = bvec_ref.shape[-1]
    bias = pltpu.roll(jnp.broadcast_to(bvec_ref[0], (tq, bw)), 0, 1, stride=1, stride_axis=0)[:, :width]
    if tq > chunk:
        row = lax.broadcasted_iota(jnp.int32, (tq, width), 0)
        col = lax.broadcasted_iota(jnp.int32, (tq, width), 1)
        chunk_start = (row // chunk) * chunk
        in_band = (col >= chunk_start) & (col < chunk_start + LEFT + chunk)
        bias = jnp.where(in_band, bias, NEG_INF)
    bias_ref[...] = bias
    scale = ATT_HEAD_DIM ** -0.5

    n_chunks = tq // chunk
    spans = []
    for r in range(n_chunks):
        lo = (r * chunk) // LANES * LANES
        hi = min(width, -(-(r * chunk + LEFT + chunk) // LANES) * LANES)
        spans.append((lo, hi))

    def block_bases(t):
        return [pl.multiple_of((t * group + u) * tq, tq) for u in range(group)]

    def blocks(t, mask_left):
        bases = block_bases(t)
        queries = [_rms_norm_rows(q_ref[0, pl.ds(base, tq), :], qg_ref[...]).astype(BF16) for base in bases]
        scores = [_mm_nt(q, kp_ref[pl.ds(base, width), :]) for q, base in zip(queries, bases)]
        pieces = [(u, r) for u in range(group) for r in range(n_chunks)]
        masked_scores = []
        for u, r in pieces:
            lo, hi = spans[r]
            rs = slice(r * chunk, (r + 1) * chunk)
            tiles = []
            for c0 in range(lo, hi, LANES):
                c1 = min(c0 + LANES, hi)
                band = bias_ref[rs, c0:c1]
                s = scores[u][rs, c0:c1] * scale + band
                inside = r * chunk <= c0 and c1 <= r * chunk + LEFT + chunk
                if mask_left or not inside:
                    ok = band > 0.5 * NEG_INF
                    if mask_left:
                        col = lax.broadcasted_iota(jnp.int32, (chunk, c1 - c0), 1) + c0
                        ok = ok & (col >= LEFT - bases[u])
                    s = jnp.where(ok, s, NEG_INF)
                tiles.append(s)
            masked_scores.append(jnp.concatenate(tiles, axis=1) if len(tiles) > 1 else tiles[0])
        maxes = [jnp.max(s, axis=-1, keepdims=True) for s in masked_scores]
        exps = [jnp.exp(s - m) for s, m in zip(masked_scores, maxes)]
        invs = [1.0 / jnp.sum(e, axis=-1, keepdims=True) for e in exps]
        rows = []
        for (u, r), e in zip(pieces, exps):
            lo, hi = spans[r]
            parts = ([jnp.zeros((chunk, lo), BF16)] if lo else []) + [e.astype(BF16)]
            parts += [jnp.zeros((chunk, width - hi), BF16)] if hi < width else []
            rows.append(jnp.concatenate(parts, axis=1) if len(parts) > 1 else parts[0])
        for u, base in enumerate(bases):
            mine = slice(u * n_chunks, (u + 1) * n_chunks)
            w = jnp.concatenate(rows[mine], axis=0) if n_chunks > 1 else rows[mine][0]
            inv = jnp.concatenate(invs[mine], axis=0) if n_chunks > 1 else invs[mine][0]
            o_ref[0, pl.ds(base, tq), :] = (_mm(w, vp_ref[pl.ds(base, width), :]) * inv).astype(o_ref.dtype)

    n_iter = seq // (tq * group)
    n_left = 0 if left_valid else min(n_iter, -(-LEFT // (tq * group)))

    def masked(t, carry):
        blocks(t, True)
        return carry

    def plain(t, carry):
        blocks(t, False)
        return carry

    lax.fori_loop(0, n_left, masked, 0)
    lax.fori_loop(n_left, n_iter, plain, 0)


def _attention(proj, ctx_k, ctx_v, q_gain, k_gain, bias_vec, *, heads, tq, chunk, left_valid, group, casts=()):
    b, seq, _ = proj.shape
    dh = ATT_HEAD_DIM
    bw = bias_vec.shape[-1]
    keep = min(seq, LEFT)
    assert seq % (tq * group) == 0 and tq % chunk == 0 and ctx_k.shape[1:] == (LEFT, heads, dh)
    per_batch_ctx = ctx_k.shape[0] == b
    ctx_map = (lambda bi, h: (bi, 0, 0)) if per_batch_ctx else (lambda bi, h: (0, 0, 0))
    ctx_k, ctx_v = (c.reshape(c.shape[0], LEFT * heads, dh) for c in (ctx_k, ctx_v))
    slabs = [_slab_spec(a, (b, heads)) for a in casts]
    blocks = (4 * _nbytes((seq, dh), F32) + 2 * _nbytes((LEFT, heads, dh), F32) + _nbytes((8, bw), F32)
              + 2 * _nbytes((keep * heads, dh), F32)
              + sum(_nbytes(s.block_shape, F32) + _nbytes(s.block_shape, BF16) for s in slabs))
    scratch = (2 * _nbytes((LEFT + seq, dh), BF16)
               + _nbytes((tq, LEFT + tq), F32))
    outs = pl.pallas_call(
        functools.partial(_attn_body, seq=seq, tq=tq, chunk=chunk, left_valid=left_valid, group=group,
                          n_cast=len(casts), heads=heads),
        grid=(b, heads),
        in_specs=[
            pl.BlockSpec((1, seq, dh), lambda bi, h: (bi, 0, h)),
            pl.BlockSpec((1, seq, dh), lambda bi, h: (bi, 0, heads + h)),
            pl.BlockSpec((1, seq, dh), lambda bi, h: (bi, 0, 2 * heads + h)),
            pl.BlockSpec((1, LEFT * heads, dh), ctx_map),
            pl.BlockSpec((1, LEFT * heads, dh), ctx_map),
            pl.BlockSpec((1, dh), lambda bi, h: (0, 0)),
            pl.BlockSpec((1, dh), lambda bi, h: (0, 0)),
            pl.BlockSpec((1, 1, bw), lambda bi, h: (h, 0, 0)),
        ] + slabs,
        out_specs=[
            pl.BlockSpec((1, seq, dh), lambda bi, h: (bi, 0, h)),
            pl.BlockSpec((1, keep * heads, dh), lambda bi, h: (bi, 0, 0)),
            pl.BlockSpec((1, keep * heads, dh), lambda bi, h: (bi, 0, 0)),
        ] + [_slab_spec(a, (b, heads)) for a in casts],
        out_shape=[jax.ShapeDtypeStruct((b, seq, heads * dh), BF16),
                   jax.ShapeDtypeStruct((b, keep * heads, dh), F32),
                   jax.ShapeDtypeStruct((b, keep * heads, dh), F32)]
                  + [jax.ShapeDtypeStruct(a.shape, BF16) for a in casts],
        scratch_shapes=[pltpu.VMEM((LEFT + seq, dh), BF16),
                        pltpu.VMEM((LEFT + seq, dh), BF16), pltpu.VMEM((tq, LEFT + tq), F32)],
        compiler_params=pltpu.CompilerParams(
            dimension_semantics=("parallel", "arbitrary"),
            vmem_limit_bytes=_vmem_limit(blocks, scratch, 8 * group * _nbytes((tq, LEFT + tq), F32)
                                         + 3 * _nbytes((seq, dh), F32))),
        name="attn",
    )(proj, proj, proj, ctx_k, ctx_v, q_gain.reshape(1, dh), k_gain.reshape(1, dh), bias_vec, *casts)
    kept = [o.reshape(b, keep, heads, dh) for o in outs[1:3]]
    return outs[0], kept[0], kept[1], list(outs[3:])


def _band_bias_vec(rel_bias, tq):
    width = LEFT + tq
    bw = -(-(width + tq) // LANES) * LANES
    m = jnp.arange(bw)
    d = jnp.where(m < width, m, m - bw)
    idx = jnp.clip(LEFT - d, -REL_CLIP, REL_CLIP) + REL_CLIP
    return rel_bias[idx].T.astype(F32)[:, None, :]


def _split3(x):
    h1 = x.astype(BF16)
    r1 = x - h1.astype(F32)
    h2 = r1.astype(BF16)
    h3 = (r1 - h2.astype(F32)).astype(BF16)
    return h1, h2, h3


def _wkv_body(*refs, clen, heads, col0, nseq, n_cast):
    (x_ref, sp_ref, s0_ref, mu_ref, w0_ref, wl_ref, a0_ref, al_ref, gl_ref, kk_ref, ka_ref, rk_ref, lnw_ref,
     lnb_ref) = refs[:14]
    o_ref, so_ref = refs[14 + n_cast:16 + n_cast]
    carry_ref, s_ref = refs[16 + 2 * n_cast:]
    _cast_slabs(refs[14:14 + n_cast], refs[16 + n_cast:16 + 2 * n_cast])
    c = pl.program_id(1)
    n = RWKV_HEAD_DIM
    width = heads * n
    pairs = heads // 2
    pw = 2 * n
    tw = 2 * clen
    bf = lambda t: t.astype(BF16)

    @pl.when(c == 0)
    def _():
        zeros = jnp.zeros((n, n), F32)
        for i in range(nseq):
            carry_ref[i] = sp_ref[i]
            for q in range(pairs):
                s_ref[i * pairs + q] = jnp.concatenate(
                    [jnp.concatenate([s0_ref[i, 2 * q], zeros], axis=1),
                     jnp.concatenate([zeros, s0_ref[i, 2 * q + 1]], axis=1)], axis=0)

    trow = lax.broadcasted_iota(jnp.int32, (SUBLANES, 1), 0)
    ti = lax.broadcasted_iota(jnp.int32, (clen, clen), 0)
    si = lax.broadcasted_iota(jnp.int32, (clen, clen), 1)
    tri = (si <= ti).astype(BF16)
    levels = max(1, (clen - 1).bit_length())

    def prepare(i):
        xb = x_ref[i, :, col0:]
        rolled = pltpu.roll(xb, 1, 0)
        shifted = jnp.concatenate([jnp.where(trow == 0, carry_ref[i], rolled[:SUBLANES]), rolled[SUBLANES:]],
                                  axis=0)
        carry_ref[i] = xb[clen - 1:clen, :]
        xm = xb + mu_ref[...] * (shifted - xb)

        r = xm[:, 0:width]
        k = xm[:, width:2 * width]
        v = xm[:, 2 * width:3 * width]
        o3 = 3 * width
        w_lo = xm[:, o3:o3 + W_LORA]
        a_lo = xm[:, o3 + W_LORA:o3 + W_LORA + A_LORA]
        g_lo = xm[:, o3 + W_LORA + A_LORA:o3 + W_LORA + A_LORA + G_LORA]

        log_decay = -math.exp(-0.5) * jax.nn.sigmoid(w0_ref[...] + _mm(bf(jnp.tanh(w_lo)), wl_ref[...]))
        a = jax.nn.sigmoid(a0_ref[...] + _mm(bf(a_lo), al_ref[...]))
        g = _mm(bf(jax.nn.sigmoid(g_lo)), gl_ref[...])
        kk = k * kk_ref[...]
        k = k * (1.0 + (a - 1.0) * ka_ref[...])

        d1, d2, d3 = _split3(log_decay)
        cum = _mm(tri, d1) + _mm(tri, d2) + _mm(tri, d3)
        g_in = jnp.exp(cum)
        g_inv = jnp.exp(-cum)
        g_last = g_in[clen - 1:clen, :]
        k_h = k * g_inv
        return dict(v=v, a=a, g=g, kk=kk, rk=r * k * rk_ref[...], g_inv=g_inv, g_ex=jnp.exp(cum - log_decay),
                    g_last=g_last, r_t=r * g_in, k_h=k_h, k_l=k_h * g_last)

    pre = [prepare(i) for i in range(nseq)]

    first = lax.broadcasted_iota(jnp.int32, (1, pw), 1) < n
    first_t = lax.broadcasted_iota(jnp.int32, (1, tw), 1) < clen
    row2 = lax.broadcasted_iota(jnp.int32, (2 * clen, 2 * tw), 0)
    spos = lax.broadcasted_iota(jnp.int32, (2 * clen, 2 * tw), 1) % clen
    causal = spos < jnp.where(row2 < clen, row2, row2 - clen + 1)
    same_head = ((lax.broadcasted_iota(jnp.int32, (pw, pw), 0) < n)
                 == (lax.broadcasted_iota(jnp.int32, (pw, pw), 1) < n))

    def head_sums(x):
        s0 = jnp.sum(jnp.where(first, x, 0.0), axis=-1, keepdims=True)
        s1 = jnp.sum(jnp.where(first, 0.0, x), axis=-1, keepdims=True)
        return jnp.where(first, s0, s1)

    def block_diag(x, mask):
        zero = jnp.zeros_like(x)
        return jnp.concatenate([jnp.where(mask, x, zero), jnp.where(mask, zero, x)], axis=0)

    units = [(i, q) for i in range(nseq) for q in range(pairs)]
    us = range(len(units))
    cols = [slice(q * pw, (q + 1) * pw) for _, q in units]
    val = lambda name, u: pre[units[u][0]][name][:, cols[u]]

    b_hat, a_t, v_bd, s_prev, s_bf = [], [], [], [], []
    for u in us:
        kk_u = val("kk", u)
        kk_u = kk_u * jnp.minimum(lax.rsqrt(head_sums(kk_u * kk_u)), 1e12)
        b_hat.append(kk_u * val("a", u) * val("g_inv", u))
        a_t.append(-kk_u * val("g_ex", u))
        v_bd.append(block_diag(bf(val("v", u)), first))
        s_prev.append(s_ref[u])
        s_bf.append(bf(s_prev[u]))

    ar = [bf(jnp.concatenate([a_t[u], val("r_t", u)], axis=0)) for u in us]
    p = [_mm_nt(ar[u], jnp.concatenate([block_diag(bf(b_hat[u]), first),
                                        block_diag(bf(val("k_h", u)), first)], axis=0)) for u in us]
    pm = [bf(jnp.where(causal, p[u], 0.0)) for u in us]
    nk = [pm[u][:clen, :tw] for u in us]
    a_rb = [pm[u][clen:, :tw] for u in us]

    sv = [_mm_nt(ar[u], s_bf[u]) + _mm(pm[u][:, tw:], v_bd[u]) for u in us]
    sa = [sv[u][:clen] for u in us]
    for lvl in range(levels):
        if lvl + 1 < levels:
            both = [_mm(nk[u], jnp.concatenate([block_diag(bf(sa[u]), first), block_diag(nk[u], first_t)],
                                               axis=1)) for u in us]
            sa = [sa[u] + both[u][:, :pw] for u in us]
            nk = [bf(both[u][:, pw:]) for u in us]
        else:
            sa = [sa[u] + _mm(nk[u], block_diag(bf(sa[u]), first)) for u in us]

    sa_bf = [bf(sa[u]) for u in us]
    ys = [sv[u][clen:] + _mm(a_rb[u], block_diag(sa_bf[u], first)) for u in us]
    for u in us:
        g_last = val("g_last", u)
        upd = _mm_tn(jnp.concatenate([sa_bf[u], bf(val("v", u))], axis=0),
                     jnp.concatenate([bf(b_hat[u] * g_last), bf(val("k_l", u))], axis=0))
        s_ref[u] = s_prev[u] * g_last + jnp.where(same_head, upd, 0.0)

    outs = []
    inv_n = 1.0 / n
    for u in us:
        y = ys[u]
        mean = head_sums(y) * inv_n
        var = head_sums(jnp.square(y - mean)) * inv_n
        y = (y - mean) * lax.rsqrt(var + GN_EPS)
        y = y * lnw_ref[:, cols[u]] + lnb_ref[:, cols[u]]
        y = y + head_sums(val("rk", u)) * val("v", u)
        outs.append(y)

    for i in range(nseq):
        o_ref[i] = (jnp.concatenate(outs[i * pairs:(i + 1) * pairs], axis=1) * pre[i]["g"]).astype(o_ref.dtype)

    @pl.when(c == pl.num_programs(1) - 1)
    def _():
        for i in range(nseq):
            for q in range(pairs):
                so_ref[i, 2 * q] = s_ref[i * pairs + q, 0:n, 0:n]
                so_ref[i, 2 * q + 1] = s_ref[i * pairs + q, n:pw, n:pw]


def _wkv(proj, shift_prev, state0, mu, w0, w_lora, a0, a_lora, g_lora, k_k, k_a, r_k, lnx_w, lnx_b, *, clen,
         nseq, casts=()):
    b, t, full_w = proj.shape
    heads = state0.shape[1]
    n = RWKV_HEAD_DIM
    width = heads * n
    pw = shift_prev.shape[-1]
    col0 = full_w - pw
    assert t % clen == 0 and pw == 3 * width + W_LORA + A_LORA + G_LORA and heads % 2 == 0
    assert col0 % LANES == 0 and 2 * n == LANES and b % nseq == 0
    row = lambda a: a.reshape(1, -1)
    full = lambda shape: pl.BlockSpec(shape, lambda bi, c: (0,) * len(shape))
    grid = (b // nseq, t // clen)
    slabs = [_slab_spec(a, grid) for a in casts]
    blocks = (nseq * (_nbytes((clen, full_w), F32) + _nbytes((clen, width), F32)
                      + 2 * _nbytes((heads, n, LANES), F32))
              + sum(_nbytes(s.block_shape, F32) + _nbytes(s.block_shape, BF16) for s in slabs))
    state_scratch = _nbytes((nseq * heads // 2, 2 * n, 2 * n), F32)
    outs = pl.pallas_call(
        functools.partial(_wkv_body, clen=clen, heads=heads, col0=col0, nseq=nseq, n_cast=len(casts)),
        grid=grid,
        in_specs=[
            pl.BlockSpec((nseq, clen, full_w), lambda bi, c: (bi, c, 0)),
            pl.BlockSpec((nseq, 1, pw), lambda bi, c: (bi, 0, 0)),
            pl.BlockSpec((nseq, heads, n, n), lambda bi, c: (bi, 0, 0, 0)),
            full((1, pw)), full((1, width)), full((W_LORA, width)), full((1, width)), full((A_LORA, width)),
            full((G_LORA, width)), full((1, width)), full((1, width)), full((1, width)), full((1, width)),
            full((1, width)),
        ] + slabs,
        out_specs=[
            pl.BlockSpec((nseq, clen, width), lambda bi, c: (bi, c, 0)),
            pl.BlockSpec((nseq, heads, n, n), lambda bi, c: (bi, 0, 0, 0)),
        ] + [_slab_spec(a, grid) for a in casts],
        out_shape=[jax.ShapeDtypeStruct((b, t, width), BF16),
                   jax.ShapeDtypeStruct((b, heads, n, n), F32)]
                  + [jax.ShapeDtypeStruct(a.shape, BF16) for a in casts],
        scratch_shapes=[pltpu.VMEM((nseq, 1, pw), F32), pltpu.VMEM((nseq * heads // 2, 2 * n, 2 * n), F32)],
        compiler_params=pltpu.CompilerParams(
            dimension_semantics=("parallel", "arbitrary"),
            vmem_limit_bytes=_vmem_limit(blocks, state_scratch, 40 * nseq * _nbytes((clen, width), F32))),
        name="wkv",
    )(proj, shift_prev, state0, row(mu), row(w0), w_lora, row(a0), a_lora, g_lora, row(k_k), row(k_a),
      row(r_k), row(lnx_w), row(lnx_b), *casts)
    return outs[0], outs[1], list(outs[2:])


def _outproj_body(h_ref, att_ref, rw_ref, wa_ref, wr_ref, o_ref):
    o_ref[...] = h_ref[...] + _mm(att_ref[...], wa_ref[...]) + _mm(rw_ref[...], wr_ref[...])


def _outproj(h, att, rw, w_out, *, tm):
    m, d = h.shape
    ka, kr = att.shape[1], rw.shape[1]
    assert m % tm == 0 and ka == kr and w_out.shape == (ka + kr, d)
    blocks = (2 * _nbytes((tm, d), F32) + _nbytes((tm, ka), BF16) + _nbytes((tm, kr), BF16)
              + _nbytes((ka, d), BF16) + _nbytes((kr, d), BF16))
    return pl.pallas_call(
        _outproj_body,
        grid=(m // tm,),
        in_specs=[
            pl.BlockSpec((tm, d), lambda i: (i, 0)),
            pl.BlockSpec((tm, ka), lambda i: (i, 0)),
            pl.BlockSpec((tm, kr), lambda i: (i, 0)),
            pl.BlockSpec((ka, d), lambda i: (0, 0)),
            pl.BlockSpec((kr, d), lambda i: (1, 0)),
        ],
        out_specs=pl.BlockSpec((tm, d), lambda i: (i, 0)),
        out_shape=jax.ShapeDtypeStruct((m, d), F32),
        compiler_params=pltpu.CompilerParams(
            dimension_semantics=("parallel",),
            vmem_limit_bytes=_vmem_limit(blocks, 0, 2 * _nbytes((tm, d), F32))),
        name="outproj",
    )(h, att, rw, w_out, w_out)


def _ple_body(x_ref, g_ref, wg_ref, p_ref, wp_ref, o_ref):
    x = x_ref[...]
    gate = jax.nn.sigmoid(_mm(_rms_norm_rows(x, g_ref[...]).astype(BF16), wg_ref[...]))
    o_ref[...] = x + gate * _mm(p_ref[...].astype(BF16), wp_ref[...])


def _ple(x, gain, w_gate, p, w_proj, *, tm):
    m, d = x.shape
    pd = p.shape[1]
    assert m % tm == 0
    blocks = (2 * _nbytes((tm, d), F32) + _nbytes((d, d), BF16) + _nbytes((tm, pd), F32)
              + _nbytes((pd, d), BF16))
    return pl.pallas_call(
        _ple_body,
        grid=(m // tm,),
        in_specs=[
            pl.BlockSpec((tm, d), lambda i: (i, 0)),
            pl.BlockSpec((1, d), lambda i: (0, 0)),
            pl.BlockSpec((d, d), lambda i: (0, 0)),
            pl.BlockSpec((tm, pd), lambda i: (i, 0)),
            pl.BlockSpec((pd, d), lambda i: (0, 0)),
        ],
        out_specs=pl.BlockSpec((tm, d), lambda i: (i, 0)),
        out_shape=jax.ShapeDtypeStruct((m, d), F32),
        compiler_params=pltpu.CompilerParams(
            dimension_semantics=("parallel",),
            vmem_limit_bytes=_vmem_limit(blocks, 0, 4 * _nbytes((tm, d), F32))),
        name="ple",
    )(x, gain.reshape(1, d), w_gate, p, w_proj)


def _pick(m, candidates):
    for c in candidates:
        if m % c == 0:
            return c
    raise ValueError(f"no tile for {m}")


LATE_WEIGHTS = ("ffn2_gate", "ffn2_up", "ffn2_down", "w_out", "ple_gate")


def _trunk_layer(x, p, lw, ctx_k, ctx_v, left_valid, shift_prev, wkv_prev):
    b, t, d = x.shape
    m = b * t
    tm = _pick(m, (512, 256, 128))
    tm_big = _pick(m, (1024, 512, 256, 128))
    tf = 512
    x2 = x.reshape(m, d)
    att_w = lw["w_out"].shape[0] // 2
    heads = att_w // ATT_HEAD_DIM

    h = _ffn(x2, lw["norm_ffn1"], lw["ffn1_gate"], lw["ffn1_up"], lw["ffn1_down"], tm=tm_big, tf=tf)
    proj = _proj(h, lw["norm_mix"], lw["w_in"], tm=tm_big, tn=lw["w_in"].shape[1] // 5)
    proj3 = proj.reshape(b, t, -1)

    clen = _pick(t, (CHUNK, 16))
    nseq = _pick(b, (2, 1))
    tq = _pick(t, (2 * CHUNK, CHUNK, 16))
    group = _pick(t // tq, (4, 2, 1))
    att_grid, wkv_grid = (b, heads), (b // nseq, t // clen)
    pending = [name for name in LATE_WEIGHTS if lw[name].dtype != BF16]
    in_wkv = [name for name in pending if _slab_spec(lw[name], wkv_grid)]
    in_att = [name for name in pending if name not in in_wkv and _slab_spec(lw[name], att_grid)]
    lw = {**lw, **{name: lw[name].astype(BF16) for name in pending if name not in in_wkv + in_att}}

    att, k_keep, v_keep, copies = _attention(
        proj3, ctx_k, ctx_v, lw["q_norm"], lw["k_norm"], _band_bias_vec(lw["rel_bias"], tq), heads=heads, tq=tq,
        chunk=min(CHUNK, tq), left_valid=left_valid, group=group, casts=[lw[name] for name in in_att])
    lw = {**lw, **dict(zip(in_att, copies))}
    rw, wkv_new, copies = _wkv(proj3, shift_prev, wkv_prev, lw["rwkv_mu"], lw["rwkv_w0"], lw["rwkv_w_lora"],
                               lw["rwkv_a0"], lw["rwkv_a_lora"], lw["rwkv_g_lora"], lw["rwkv_k_k"], lw["rwkv_k_a"],
                               lw["rwkv_r_k"], lw["rwkv_lnx_w"], lw["rwkv_lnx_b"], clen=clen, nseq=nseq,
                               casts=[lw[name] for name in in_wkv])
    lw = {**lw, **dict(zip(in_wkv, copies))}

    h = _outproj(h, att.reshape(m, att_w), rw.reshape(m, -1), lw["w_out"], tm=tm)
    h = _ffn(h, lw["norm_ffn2"], lw["ffn2_gate"], lw["ffn2_up"], lw["ffn2_down"], tm=tm_big, tf=tf)
    h = _ple(h, lw["norm_ple"], lw["ple_gate"], p.reshape(m, -1), lw["ple_proj"], tm=tm)

    return h.reshape(b, t, d), (k_keep, v_keep, wkv_new, proj3[:, -1:, 3 * att_w:]), lw


def kernel(x_prompt, x_sample, cache_att_k, cache_att_v, state_wkv, state_shift, p_prompt, p_sample, norm_ffn1, ffn1_gate, ffn1_up, ffn1_down, norm_mix, w_in, q_norm, k_norm, rel_bias, rwkv_mu, rwkv_w0, rwkv_w_lora, rwkv_a0, rwkv_a_lora, rwkv_g_lora, rwkv_k_k, rwkv_k_a, rwkv_r_k, rwkv_lnx_w, rwkv_lnx_b, w_out, norm_ffn2, ffn2_gate, ffn2_up, ffn2_down, norm_ple, ple_gate, ple_proj):
    depth = norm_ffn1.shape[0]
    hp, hs = x_prompt, x_sample
    bp = hp.shape[0]
    att_w = w_out.shape[1] // 2
    rwkv_heads = rwkv_r_k.shape[1]
    rwkv_pw = rwkv_mu.shape[1]
    prompt_states, sample_states = [], []
    for i in range(depth):
        lw = {
            "norm_ffn1": norm_ffn1[i], "ffn1_gate": ffn1_gate[i].astype(BF16), "ffn1_up": ffn1_up[i].astype(BF16),
            "ffn1_down": ffn1_down[i].astype(BF16), "norm_mix": norm_mix[i], "w_in": w_in[i].astype(BF16),
            "q_norm": q_norm[i], "k_norm": k_norm[i],
            "rel_bias": rel_bias[i], "rwkv_mu": rwkv_mu[i], "rwkv_w0": rwkv_w0[i],
            "rwkv_w_lora": rwkv_w_lora[i].astype(BF16), "rwkv_a0": rwkv_a0[i],
            "rwkv_a_lora": rwkv_a_lora[i].astype(BF16), "rwkv_g_lora": rwkv_g_lora[i].astype(BF16),
            "rwkv_k_k": rwkv_k_k[i], "rwkv_k_a": rwkv_k_a[i], "rwkv_r_k": rwkv_r_k[i].reshape(-1),
            "rwkv_lnx_w": rwkv_lnx_w[i], "rwkv_lnx_b": rwkv_lnx_b[i],
            "w_out": w_out[i], "norm_ffn2": norm_ffn2[i], "ffn2_gate": ffn2_gate[i], "ffn2_up": ffn2_up[i],
            "ffn2_down": ffn2_down[i], "norm_ple": norm_ple[i], "ple_gate": ple_gate[i],
            "ple_proj": ple_proj[i].astype(BF16),
        }
        zero_ctx = jnp.zeros((1, LEFT, att_w // ATT_HEAD_DIM, ATT_HEAD_DIM), F32)
        zero_shift = jnp.zeros((bp, 1, rwkv_pw), hp.dtype)
        zero_wkv = jnp.zeros((bp, rwkv_heads, RWKV_HEAD_DIM, RWKV_HEAD_DIM), hp.dtype)
        hp, sp, lw = _trunk_layer(hp, p_prompt[i], lw, zero_ctx, zero_ctx, False, zero_shift, zero_wkv)
        hs, ss, _ = _trunk_layer(hs, p_sample[i], lw, cache_att_k[i], cache_att_v[i], True, state_shift[i],
                                 state_wkv[i])
        prompt_states.append(sp)
        sample_states.append(ss)
    stack = lambda states, j: states[0][j][None] if depth == 1 else jnp.stack([s[j] for s in states])
    return (hp, hs, stack(prompt_states, 0), stack(prompt_states, 1), stack(prompt_states, 2),
            stack(prompt_states, 3), stack(sample_states, 0), stack(sample_states, 1),
            stack(sample_states, 2), stack(sample_states, 3))
```

```python
import functools
import math

import jax
import jax.numpy as jnp
from jax import lax
from jax.experimental import pallas as pl
from jax.experimental.pallas import tpu as pltpu

F32 = jnp.float32
BF16 = jnp.bfloat16

CHUNK = 64
LEFT = 512
ATT_HEAD_DIM = 128
REL_CLIP = 256
RWKV_HEAD_DIM = 64
W_LORA = 64
A_LORA = 64
G_LORA = 128
RMS_EPS = 1e-6
GN_EPS = 64e-5
NEG_INF = -1e30

LANES = 128
SUBLANES = 8
V7X_VMEM_BYTES = 64 * 1024 * 1024
VMEM_LIMIT_CAP = V7X_VMEM_BYTES - 8 * 1024 * 1024


def _vmem_limit(block_bytes, scratch_bytes, temp_bytes):
    need = 2 * block_bytes + scratch_bytes + temp_bytes + (4 << 20)
    return int(min(max(need, 16 << 20), VMEM_LIMIT_CAP))


def _nbytes(shape, dtype):
    return math.prod(shape) * jnp.dtype(dtype).itemsize


def _rms_norm_rows(x, gain):
    ms = jnp.mean(x * x, axis=-1, keepdims=True)
    return x * lax.rsqrt(ms + RMS_EPS) * gain


def _mm(a, b):
    return jnp.dot(a, b, preferred_element_type=F32)


def _mm_nt(a, b):
    return lax.dot_general(a, b, (((1,), (1,)), ((), ())), preferred_element_type=F32)


def _mm_tn(a, b):
    return lax.dot_general(a, b, (((0,), (0,)), ((), ())), preferred_element_type=F32)


def _slab_spec(array, grid):
    n_i, n_j = grid
    rows = array.shape[0] // (n_i * n_j)
    if array.shape[0] % (n_i * n_j) or rows % (2 * SUBLANES):
        return None
    return pl.BlockSpec((rows, array.shape[1]), lambda i, j: (i * n_j + j, 0))


def _cast_slabs(src_refs, dst_refs):
    for src, dst in zip(src_refs, dst_refs):
        dst[...] = src[...].astype(BF16)


def _ffn_body(x_ref, g_ref, wg_ref, wu_ref, wd_ref, o_ref, xn_ref):
    @pl.when(pl.program_id(1) == 0)
    def _():
        x = x_ref[...]
        xn_ref[...] = _rms_norm_rows(x, g_ref[...]).astype(BF16)
        o_ref[...] = x

    xn = xn_ref[...]
    half = wg_ref.shape[1] // 2
    cols = (slice(0, half), slice(half, 2 * half))
    au = [(_mm(xn, wg_ref[:, c]), _mm(xn, wu_ref[:, c])) for c in cols]
    hs = [(a * jax.nn.sigmoid(a) * u).astype(BF16) for a, u in au]
    o_ref[...] += 0.5 * (_mm(hs[0], wd_ref[cols[0], :]) + _mm(hs[1], wd_ref[cols[1], :]))


def _ffn(x, gain, wg, wu, wd, *, tm, tf):
    m, d = x.shape
    f = wg.shape[1]
    assert m % tm == 0 and f % tf == 0
    blocks = (_nbytes((tm, d), F32) * 2 + _nbytes((d, tf), BF16) * 3)
    temps = _nbytes((tm, tf), F32) * 4
    return pl.pallas_call(
        _ffn_body,
        grid=(m // tm, f // tf),
        in_specs=[
            pl.BlockSpec((tm, d), lambda i, j: (i, 0)),
            pl.BlockSpec((1, d), lambda i, j: (0, 0)),
            pl.BlockSpec((d, tf), lambda i, j: (0, j)),
            pl.BlockSpec((d, tf), lambda i, j: (0, j)),
            pl.BlockSpec((tf, d), lambda i, j: (j, 0)),
        ],
        out_specs=pl.BlockSpec((tm, d), lambda i, j: (i, 0)),
        out_shape=jax.ShapeDtypeStruct((m, d), F32),
        scratch_shapes=[pltpu.VMEM((tm, d), BF16)],
        compiler_params=pltpu.CompilerParams(
            dimension_semantics=("parallel", "arbitrary"),
            vmem_limit_bytes=_vmem_limit(blocks, _nbytes((tm, d), BF16), temps)),
        name="ffn",
    )(x, gain.reshape(1, d), wg, wu, wd)


def _proj_body(x_ref, g_ref, w_ref, o_ref, xn_ref):
    @pl.when(pl.program_id(1) == 0)
    def _():
        xn_ref[...] = _rms_norm_rows(x_ref[...], g_ref[...]).astype(BF16)

    o_ref[...] = _mm(xn_ref[...], w_ref[...])


def _proj(x, gain, w, *, tm, tn):
    m, d = x.shape
    n = w.shape[1]
    assert m % tm == 0 and n % tn == 0
    blocks = _nbytes((tm, d), F32) + _nbytes((d, tn), BF16) + _nbytes((tm, tn), F32)
    return pl.pallas_call(
        _proj_body,
        grid=(m // tm, n // tn),
        in_specs=[
            pl.BlockSpec((tm, d), lambda i, j: (i, 0)),
            pl.BlockSpec((1, d), lambda i, j: (0, 0)),
            pl.BlockSpec((d, tn), lambda i, j: (0, j)),
        ],
        out_specs=pl.BlockSpec((tm, tn), lambda i, j: (i, j)),
        out_shape=jax.ShapeDtypeStruct((m, n), F32),
        scratch_shapes=[pltpu.VMEM((tm, d), BF16)],
        compiler_params=pltpu.CompilerParams(
            dimension_semantics=("parallel", "arbitrary"),
            vmem_limit_bytes=_vmem_limit(blocks, _nbytes((tm, d), BF16), 2 * _nbytes((tm, tn), F32))),
        name="proj",
    )(x, gain.reshape(1, d), w)


def _attn_body(*refs, seq, tq, chunk, left_valid, group, n_cast, heads):
    q_ref, k_ref, v_ref, ck_ref, cv_ref, qg_ref, kg_ref, bvec_ref = refs[:8]
    o_ref, kn_ref, vk_ref = refs[8 + n_cast:11 + n_cast]
    kp_ref, vp_ref, bias_ref = refs[11 + 2 * n_cast:]
    _cast_slabs(refs[8:8 + n_cast], refs[11 + n_cast:11 + 2 * n_cast])

    keep = kn_ref.shape[1] // heads
    kn = _rms_norm_rows(k_ref[0], kg_ref[...])
    kept = pl.ds(pl.program_id(1), keep, stride=heads)
    kn_ref[0, kept, :] = kn[seq - keep:, :]
    vk_ref[0, kept, :] = v_ref[0, seq - keep:, :]
    if left_valid:
        mine = pl.ds(pl.program_id(1), LEFT, stride=heads)
        kp_ref[0:LEFT, :] = ck_ref[0, mine, :].astype(BF16)
        vp_ref[0:LEFT, :] = cv_ref[0, mine, :].astype(BF16)
    else:
        kp_ref[0:LEFT, :] = jnp.zeros((LEFT, ATT_HEAD_DIM), BF16)
        vp_ref[0:LEFT, :] = jnp.zeros((LEFT, ATT_HEAD_DIM), BF16)
    kp_ref[LEFT:LEFT + seq, :] = kn.astype(BF16)
    vp_ref[LEFT:LEFT + seq, :] = v_ref[0].astype(BF16)

    width = LEFT + tq
    bw = bvec_ref.shape[-1]
    bias = pltpu.roll(jnp.broadcast_to(bvec_ref[0], (tq, bw)), 0, 1, stride=1, stride_axis=0)[:, :width]
    if tq > chunk:
        row = lax.broadcasted_iota(jnp.int32, (tq, width), 0)
        col = lax.broadcasted_iota(jnp.int32, (tq, width), 1)
        chunk_start = (row // chunk) * chunk
        in_band = (col >= chunk_start) & (col < chunk_start + LEFT + chunk)
        bias = jnp.where(in_band, bias, NEG_INF)
    bias_ref[...] = bias
    scale = ATT_HEAD_DIM ** -0.5

    n_chunks = tq // chunk
    spans = []
    for r in range(n_chunks):
        lo = (r * chunk) // LANES * LANES
        hi = min(width, -(-(r * chunk + LEFT + chunk) // LANES) * LANES)
        spans.append((lo, hi))

    def block_bases(t):
        return [pl.multiple_of((t * group + u) * tq, tq) for u in range(group)]

    def blocks(t, mask_left):
        bases = block_bases(t)
        queries = [_rms_norm_rows(q_ref[0, pl.ds(base, tq), :], qg_ref[...]).astype(BF16) for base in bases]
        scores = [_mm_nt(q, kp_ref[pl.ds(base, width), :]) for q, base in zip(queries, bases)]
        pieces = [(u, r) for u in range(group) for r in range(n_chunks)]
        masked_scores = []
        for u, r in pieces:
            lo, hi = spans[r]
            rs = slice(r * chunk, (r + 1) * chunk)
            tiles = []
            for c0 in range(lo, hi, LANES):
                c1 = min(c0 + LANES, hi)
                band = bias_ref[rs, c0:c1]
                s = scores[u][rs, c0:c1] * scale + band
                inside = r * chunk <= c0 and c1 <= r * chunk + LEFT + chunk
                if mask_left or not inside:
                    ok = band > 0.5 * NEG_INF
                    if mask_left:
                        col = lax.broadcasted_iota(jnp.int32, (chunk, c1 - c0), 1) + c0
                        ok = ok & (col >= LEFT - bases[u])
                    s = jnp.where(ok, s, NEG_INF)
                tiles.append(s)
            masked_scores.append(jnp.concatenate(tiles, axis=1) if len(tiles) > 1 else tiles[0])
        maxes = [jnp.max(s, axis=-1, keepdims=True) for s in masked_scores]
        exps = [jnp.exp(s - m) for s, m in zip(masked_scores, maxes)]
        invs = [1.0 / jnp.sum(e, axis=-1, keepdims=True) for e in exps]
        rows = []
        for (u, r), e in zip(pieces, exps):
            lo, hi = spans[r]
            parts = ([jnp.zeros((chunk, lo), BF16)] if lo else []) + [e.astype(BF16)]
            parts += [jnp.zeros((chunk, width - hi), BF16)] if hi < width else []
            rows.append(jnp.concatenate(parts, axis=1) if len(parts) > 1 else parts[0])
        for u, base in enumerate(bases):
            mine = slice(u * n_chunks, (u + 1) * n_chunks)
            w = jnp.concatenate(rows[mine], axis=0) if n_chunks > 1 else rows[mine][0]
            inv = jnp.concatenate(invs[mine], axis=0) if n_chunks > 1 else invs[mine][0]
            o_ref[0, pl.ds(base, tq), :] = (_mm(w, vp_ref[pl.ds(base, width), :]) * inv).astype(o_ref.dtype)

    n_iter = seq // (tq * group)
    n_left = 0 if left_valid else min(n_iter, -(-LEFT // (tq * group)))

    def masked(t, carry):
        blocks(t, True)
        return carry

    def plain(t, carry):
        blocks(t, False)
        return carry

    lax.fori_loop(0, n_left, masked, 0)
    lax.fori_loop(n_left, n_iter, plain, 0)


def _attention(proj, ctx_k, ctx_v, q_gain, k_gain, bias_vec, *, heads, tq, chunk, left_valid, group, casts=()):
    b, seq, _ = proj.shape
    dh = ATT_HEAD_DIM
    bw = bias_vec.shape[-1]
    keep = min(seq, LEFT)
    assert seq % (tq * group) == 0 and tq % chunk == 0 and ctx_k.shape[1:] == (LEFT, heads, dh)
    per_batch_ctx = ctx_k.shape[0] == b
    ctx_map = (lambda bi, h: (bi, 0, 0)) if per_batch_ctx else (lambda bi, h: (0, 0, 0))
    ctx_k, ctx_v = (c.reshape(c.shape[0], LEFT * heads, dh) for c in (ctx_k, ctx_v))
    slabs = [_slab_spec(a, (b, heads)) for a in casts]
    blocks = (4 * _nbytes((seq, dh), F32) + 2 * _nbytes((LEFT, heads, dh), F32) + _nbytes((8, bw), F32)
              + 2 * _nbytes((keep * heads, dh), F32)
              + sum(_nbytes(s.block_shape, F32) + _nbytes(s.block_shape, BF16) for s in slabs))
    scratch = (2 * _nbytes((LEFT + seq, dh), BF16)
               + _nbytes((tq, LEFT + tq), F32))
    outs = pl.pallas_call(
        functools.partial(_attn_body, seq=seq, tq=tq, chunk=chunk, left_valid=left_valid, group=group,
                          n_cast=len(casts), heads=heads),
        grid=(b, heads),
        in_specs=[
            pl.BlockSpec((1, seq, dh), lambda bi, h: (bi, 0, h)),
            pl.BlockSpec((1, seq, dh), lambda bi, h: (bi, 0, heads + h)),
            pl.BlockSpec((1, seq, dh), lambda bi, h: (bi, 0, 2 * heads + h)),
            pl.BlockSpec((1, LEFT * heads, dh), ctx_map),
            pl.BlockSpec((1, LEFT * heads, dh), ctx_map),
            pl.BlockSpec((1, dh), lambda bi, h: (0, 0)),
            pl.BlockSpec((1, dh), lambda bi, h: (0, 0)),
            pl.BlockSpec((1, 1, bw), lambda bi, h: (h, 0, 0)),
        ] + slabs,
        out_specs=[
            pl.BlockSpec((1, seq, dh), lambda bi, h: (bi, 0, h)),
            pl.BlockSpec((1, keep * heads, dh), lambda bi, h: (bi, 0, 0)),
            pl.BlockSpec((1, keep * heads, dh), lambda bi, h: (bi, 0, 0)),
        ] + [_slab_spec(a, (b, heads)) for a in casts],
        out_shape=[jax.ShapeDtypeStruct((b, seq, heads * dh), BF16),
                   jax.ShapeDtypeStruct((b, keep * heads, dh), F32),
                   jax.ShapeDtypeStruct((b, keep * heads, dh), F32)]
                  + [jax.ShapeDtypeStruct(a.shape, BF16) for a in casts],
        scratch_shapes=[pltpu.VMEM((LEFT + seq, dh), BF16),
                        pltpu.VMEM((LEFT + seq, dh), BF16), pltpu.VMEM((tq, LEFT + tq), F32)],
        compiler_params=pltpu.CompilerParams(
            dimension_semantics=("parallel", "arbitrary"),
            vmem_limit_bytes=_vmem_limit(blocks, scratch, 8 * group * _nbytes((tq, LEFT + tq), F32)
                                         + 3 * _nbytes((seq, dh), F32))),
        name="attn",
    )(proj, proj, proj, ctx_k, ctx_v, q_gain.reshape(1, dh), k_gain.reshape(1, dh), bias_vec, *casts)
    kept = [o.reshape(b, keep, heads, dh) for o in outs[1:3]]
    return outs[0], kept[0], kept[1], list(outs[3:])


def _band_bias_vec(rel_bias, tq):
    width = LEFT + tq
    bw = -(-(width + tq) // LANES) * LANES
    m = jnp.arange(bw)
    d = jnp.where(m < width, m, m - bw)
    idx = jnp.clip(LEFT - d, -REL_CLIP, REL_CLIP) + REL_CLIP
    return rel_bias[idx].T.astype(F32)[:, None, :]


def _split3(x):
    h1 = x.astype(BF16)
    r1 = x - h1.astype(F32)
    h2 = r1.astype(BF16)
    h3 = (r1 - h2.astype(F32)).astype(BF16)
    return h1, h2, h3


def _wkv_body(*refs, clen, heads, col0, nseq, n_cast):
    (x_ref, sp_ref, s0_ref, mu_ref, w0_ref, wl_ref, a0_ref, al_ref, gl_ref, kk_ref, ka_ref, rk_ref, lnw_ref,
     lnb_ref) = refs[:14]
    o_ref, so_ref = refs[14 + n_cast:16 + n_cast]
    carry_ref, s_ref = refs[16 + 2 * n_cast:]
    _cast_slabs(refs[14:14 + n_cast], refs[16 + n_cast:16 + 2 * n_cast])
    c = pl.program_id(1)
    n = RWKV_HEAD_DIM
    width = heads * n
    pairs = heads // 2
    pw = 2 * n
    tw = 2 * clen
    bf = lambda t: t.astype(BF16)

    @pl.when(c == 0)
    def _():
        zeros = jnp.zeros((n, n), F32)
        for i in range(nseq):
            carry_ref[i] = sp_ref[i]
            for q in range(pairs):
                s_ref[i * pairs + q] = jnp.concatenate(
                    [jnp.concatenate([s0_ref[i, 2 * q], zeros], axis=1),
                     jnp.concatenate([zeros, s0_ref[i, 2 * q + 1]], axis=1)], axis=0)

    trow = lax.broadcasted_iota(jnp.int32, (SUBLANES, 1), 0)
    ti = lax.broadcasted_iota(jnp.int32, (clen, clen), 0)
    si = lax.broadcasted_iota(jnp.int32, (clen, clen), 1)
    tri = (si <= ti).astype(BF16)
    levels = max(1, (clen - 1).bit_length())

    def prepare(i):
        xb = x_ref[i, :, col0:]
        rolled = pltpu.roll(xb, 1, 0)
        shifted = jnp.concatenate([jnp.where(trow == 0, carry_ref[i], rolled[:SUBLANES]), rolled[SUBLANES:]],
                                  axis=0)
        carry_ref[i] = xb[clen - 1:clen, :]
        xm = xb + mu_ref[...] * (shifted - xb)

        r = xm[:, 0:width]
        k = xm[:, width:2 * width]
        v = xm[:, 2 * width:3 * width]
        o3 = 3 * width
        w_lo = xm[:, o3:o3 + W_LORA]
        a_lo = xm[:, o3 + W_LORA:o3 + W_LORA + A_LORA]
        g_lo = xm[:, o3 + W_LORA + A_LORA:o3 + W_LORA + A_LORA + G_LORA]

        log_decay = -math.exp(-0.5) * jax.nn.sigmoid(w0_ref[...] + _mm(bf(jnp.tanh(w_lo)), wl_ref[...]))
        a = jax.nn.sigmoid(a0_ref[...] + _mm(bf(a_lo), al_ref[...]))
        g = _mm(bf(jax.nn.sigmoid(g_lo)), gl_ref[...])
        kk = k * kk_ref[...]
        k = k * (1.0 + (a - 1.0) * ka_ref[...])

        d1, d2, d3 = _split3(log_decay)
        cum = _mm(tri, d1) + _mm(tri, d2) + _mm(tri, d3)
        g_in = jnp.exp(cum)
        g_inv = jnp.exp(-cum)
        g_last = g_in[clen - 1:clen, :]
        k_h = k * g_inv
        return dict(v=v, a=a, g=g, kk=kk, rk=r * k * rk_ref[...], g_inv=g_inv, g_ex=jnp.exp(cum - log_decay),
                    g_last=g_last, r_t=r * g_in, k_h=k_h, k_l=k_h * g_last)

    pre = [prepare(i) for i in range(nseq)]

    first = lax.broadcasted_iota(jnp.int32, (1, pw), 1) < n
    first_t = lax.broadcasted_iota(jnp.int32, (1, tw), 1) < clen
    row2 = lax.broadcasted_iota(jnp.int32, (2 * clen, 2 * tw), 0)
    spos = lax.broadcasted_iota(jnp.int32, (2 * clen, 2 * tw), 1) % clen
    causal = spos < jnp.where(row2 < clen, row2, row2 - clen + 1)
    same_head = ((lax.broadcasted_iota(jnp.int32, (pw, pw), 0) < n)
                 == (lax.broadcasted_iota(jnp.int32, (pw, pw), 1) < n))

    def head_sums(x):
        s0 = jnp.sum(jnp.where(first, x, 0.0), axis=-1, keepdims=True)
        s1 = jnp.sum(jnp.where(first, 0.0, x), axis=-1, keepdims=True)
        return jnp.where(first, s0, s1)

    def block_diag(x, mask):
        zero = jnp.zeros_like(x)
        return jnp.concatenate([jnp.where(mask, x, zero), jnp.where(mask, zero, x)], axis=0)

    units = [(i, q) for i in range(nseq) for q in range(pairs)]
    us = range(len(units))
    cols = [slice(q * pw, (q + 1) * pw) for _, q in units]
    val = lambda name, u: pre[units[u][0]][name][:, cols[u]]

    b_hat, a_t, v_bd, s_prev, s_bf = [], [], [], [], []
    for u in us:
        kk_u = val("kk", u)
        kk_u = kk_u * jnp.minimum(lax.rsqrt(head_sums(kk_u * kk_u)), 1e12)
        b_hat.append(kk_u * val("a", u) * val("g_inv", u))
        a_t.append(-kk_u * val("g_ex", u))
        v_bd.append(block_diag(bf(val("v", u)), first))
        s_prev.append(s_ref[u])
        s_bf.append(bf(s_prev[u]))

    ar = [bf(jnp.concatenate([a_t[u], val("r_t", u)], axis=0)) for u in us]
    p = [_mm_nt(ar[u], jnp.concatenate([block_diag(bf(b_hat[u]), first),
                                        block_diag(bf(val("k_h", u)), first)], axis=0)) for u in us]
    pm = [bf(jnp.where(causal, p[u], 0.0)) for u in us]
    nk = [pm[u][:clen, :tw] for u in us]
    a_rb = [pm[u][clen:, :tw] for u in us]

    sv = [_mm_nt(ar[u], s_bf[u]) + _mm(pm[u][:, tw:], v_bd[u]) for u in us]
    sa = [sv[u][:clen] for u in us]
    for lvl in range(levels):
        if lvl + 1 < levels:
            both = [_mm(nk[u], jnp.concatenate([block_diag(bf(sa[u]), first), block_diag(nk[u], first_t)],
                                               axis=1)) for u in us]
            sa = [sa[u] + both[u][:, :pw] for u in us]
            nk = [bf(both[u][:, pw:]) for u in us]
        else:
            sa = [sa[u] + _mm(nk[u], block_diag(bf(sa[u]), first)) for u in us]

    sa_bf = [bf(sa[u]) for u in us]
    ys = [sv[u][clen:] + _mm(a_rb[u], block_diag(sa_bf[u], first)) for u in us]
    for u in us:
        g_last = val("g_last", u)
        upd = _mm_tn(jnp.concatenate([sa_bf[u], bf(val("v", u))], axis=0),
                     jnp.concatenate([bf(b_hat[u] * g_last), bf(val("k_l", u))], axis=0))
        s_ref[u] = s_prev[u] * g_last + jnp.where(same_head, upd, 0.0)

    outs = []
    inv_n = 1.0 / n
    for u in us:
        y = ys[u]
        mean = head_sums(y) * inv_n
        var = head_sums(jnp.square(y - mean)) * inv_n
        y = (y - mean) * lax.rsqrt(var + GN_EPS)
        y = y * lnw_ref[:, cols[u]] + lnb_ref[:, cols[u]]
        y = y + head_sums(val("rk", u)) * val("v", u)
        outs.append(y)

    for i in range(nseq):
        o_ref[i] = (jnp.concatenate(outs[i * pairs:(i + 1) * pairs], axis=1) * pre[i]["g"]).astype(o_ref.dtype)

    @pl.when(c == pl.num_programs(1) - 1)
    def _():
        for i in range(nseq):
            for q in range(pairs):
                so_ref[i, 2 * q] = s_ref[i * pairs + q, 0:n, 0:n]
                so_ref[i, 2 * q + 1] = s_ref[i * pairs + q, n:pw, n:pw]


def _wkv(proj, shift_prev, state0, mu, w0, w_lora, a0, a_lora, g_lora, k_k, k_a, r_k, lnx_w, lnx_b, *, clen,
         nseq, casts=()):
    b, t, full_w = proj.shape
    heads = state0.shape[1]
    n = RWKV_HEAD_DIM
    width = heads * n
    pw = shift_prev.shape[-1]
    col0 = full_w - pw
    assert t % clen == 0 and pw == 3 * width + W_LORA + A_LORA + G_LORA and heads % 2 == 0
    assert col0 % LANES == 0 and 2 * n == LANES and b % nseq == 0
    row = lambda a: a.reshape(1, -1)
    full = lambda shape: pl.BlockSpec(shape, lambda bi, c: (0,) * len(shape))
    grid = (b // nseq, t // clen)
    slabs = [_slab_spec(a, grid) for a in casts]
    blocks = (nseq * (_nbytes((clen, full_w), F32) + _nbytes((clen, width), F32)
                      + 2 * _nbytes((heads, n, LANES), F32))
              + sum(_nbytes(s.block_shape, F32) + _nbytes(s.block_shape, BF16) for s in slabs))
    state_scratch = _nbytes((nseq * heads // 2, 2 * n, 2 * n), F32)
    outs = pl.pallas_call(
        functools.partial(_wkv_body, clen=clen, heads=heads, col0=col0, nseq=nseq, n_cast=len(casts)),
        grid=grid,
        in_specs=[
            pl.BlockSpec((nseq, clen, full_w), lambda bi, c: (bi, c, 0)),
            pl.BlockSpec((nseq, 1, pw), lambda bi, c: (bi, 0, 0)),
            pl.BlockSpec((nseq, heads, n, n), lambda bi, c: (bi, 0, 0, 0)),
            full((1, pw)), full((1, width)), full((W_LORA, width)), full((1, width)), full((A_LORA, width)),
            full((G_LORA, width)), full((1, width)), full((1, width)), full((1, width)), full((1, width)),
            full((1, width)),
        ] + slabs,
        out_specs=[
            pl.BlockSpec((nseq, clen, width), lambda bi, c: (bi, c, 0)),
            pl.BlockSpec((nseq, heads, n, n), lambda bi, c: (bi, 0, 0, 0)),
        ] + [_slab_spec(a, grid) for a in casts],
        out_shape=[jax.ShapeDtypeStruct((b, t, width), BF16),
                   jax.ShapeDtypeStruct((b, heads, n, n), F32)]
                  + [jax.ShapeDtypeStruct(a.shape, BF16) for a in casts],
        scratch_shapes=[pltpu.VMEM((nseq, 1, pw), F32), pltpu.VMEM((nseq * heads // 2, 2 * n, 2 * n), F32)],
        compiler_params=pltpu.CompilerParams(
            dimension_semantics=("parallel", "arbitrary"),
            vmem_limit_bytes=_vmem_limit(blocks, state_scratch, 40 * nseq * _nbytes((clen, width), F32))),
        name="wkv",
    )(proj, shift_prev, state0, row(mu), row(w0), w_lora, row(a0), a_lora, g_lora, row(k_k), row(k_a),
      row(r_k), row(lnx_w), row(lnx_b), *casts)
    return outs[0], outs[1], list(outs[2:])


def _outproj_body(h_ref, att_ref, rw_ref, wa_ref, wr_ref, o_ref):
    o_ref[...] = h_ref[...] + _mm(att_ref[...], wa_ref[...]) + _mm(rw_ref[...], wr_ref[...])


def _outproj(h, att, rw, w_out, *, tm):
    m, d = h.shape
    ka, kr = att.shape[1], rw.shape[1]
    assert m % tm == 0 and ka == kr and w_out.shape == (ka + kr, d)
    blocks = (2 * _nbytes((tm, d), F32) + _nbytes((tm, ka), BF16) + _nbytes((tm, kr), BF16)
              + _nbytes((ka, d), BF16) + _nbytes((kr, d), BF16))
    return pl.pallas_call(
        _outproj_body,
        grid=(m // tm,),
        in_specs=[
            pl.BlockSpec((tm, d), lambda i: (i, 0)),
            pl.BlockSpec((tm, ka), lambda i: (i, 0)),
            pl.BlockSpec((tm, kr), lambda i: (i, 0)),
            pl.BlockSpec((ka, d), lambda i: (0, 0)),
            pl.BlockSpec((kr, d), lambda i: (1, 0)),
        ],
        out_specs=pl.BlockSpec((tm, d), lambda i: (i, 0)),
        out_shape=jax.ShapeDtypeStruct((m, d), F32),
        compiler_params=pltpu.CompilerParams(
            dimension_semantics=("parallel",),
            vmem_limit_bytes=_vmem_limit(blocks, 0, 2 * _nbytes((tm, d), F32))),
        name="outproj",
    )(h, att, rw, w_out, w_out)


def _ple_body(x_ref, g_ref, wg_ref, p_ref, wp_ref, o_ref):
    x = x_ref[...]
    gate = jax.nn.sigmoid(_mm(_rms_norm_rows(x, g_ref[...]).astype(BF16), wg_ref[...]))
    o_ref[...] = x + gate * _mm(p_ref[...].astype(BF16), wp_ref[...])


def _ple(x, gain, w_gate, p, w_proj, *, tm):
    m, d = x.shape
    pd = p.shape[1]
    assert m % tm == 0
    blocks = (2 * _nbytes((tm, d), F32) + _nbytes((d, d), BF16) + _nbytes((tm, pd), F32)
              + _nbytes((pd, d), BF16))
    return pl.pallas_call(
        _ple_body,
        grid=(m // tm,),
        in_specs=[
            pl.BlockSpec((tm, d), lambda i: (i, 0)),
            pl.BlockSpec((1, d), lambda i: (0, 0)),
            pl.BlockSpec((d, d), lambda i: (0, 0)),
            pl.BlockSpec((tm, pd), lambda i: (i, 0)),
            pl.BlockSpec((pd, d), lambda i: (0, 0)),
        ],
        out_specs=pl.BlockSpec((tm, d), lambda i: (i, 0)),
        out_shape=jax.ShapeDtypeStruct((m, d), F32),
        compiler_params=pltpu.CompilerParams(
            dimension_semantics=("parallel",),
            vmem_limit_bytes=_vmem_limit(blocks, 0, 4 * _nbytes((tm, d), F32))),
        name="ple",
    )(x, gain.reshape(1, d), w_gate, p, w_proj)


def _pick(m, candidates):
    for c in candidates:
        if m % c == 0:
            return c
    raise ValueError(f"no tile for {m}")


LATE_WEIGHTS = ("ffn2_gate", "ffn2_up", "ffn2_down", "w_out", "ple_gate")


def _trunk_layer(x, p, lw, ctx_k, ctx_v, left_valid, shift_prev, wkv_prev):
    b, t, d = x.shape
    m = b * t
    tm = _pick(m, (512, 256, 128))
    tm_big = _pick(m, (1024, 512, 256, 128))
    tf = 512
    x2 = x.reshape(m, d)
    att_w = lw["w_out"].shape[0] // 2
    heads = att_w // ATT_HEAD_DIM

    h = _ffn(x2, lw["norm_ffn1"], lw["ffn1_gate"], lw["ffn1_up"], lw["ffn1_down"], tm=tm_big, tf=tf)
    proj = _proj(h, lw["norm_mix"], lw["w_in"], tm=tm_big, tn=lw["w_in"].shape[1] // 5)
    proj3 = proj.reshape(b, t, -1)

    clen = _pick(t, (CHUNK, 16))
    nseq = _pick(b, (4, 2, 1))
    tq = _pick(t, (2 * CHUNK, CHUNK, 16))
    group = _pick(t // tq, (4, 2, 1))
    att_grid, wkv_grid = (b, heads), (b // nseq, t // clen)
    pending = [name for name in LATE_WEIGHTS if lw[name].dtype != BF16]
    in_wkv = [name for name in pending if _slab_spec(lw[name], wkv_grid)]
    in_att = [name for name in pending if name not in in_wkv and _slab_spec(lw[name], att_grid)]
    lw = {**lw, **{name: lw[name].astype(BF16) for name in pending if name not in in_wkv + in_att}}

    att, k_keep, v_keep, copies = _attention(
        proj3, ctx_k, ctx_v, lw["q_norm"], lw["k_norm"], _band_bias_vec(lw["rel_bias"], tq), heads=heads, tq=tq,
        chunk=min(CHUNK, tq), left_valid=left_valid, group=group, casts=[lw[name] for name in in_att])
    lw = {**lw, **dict(zip(in_att, copies))}
    rw, wkv_new, copies = _wkv(proj3, shift_prev, wkv_prev, lw["rwkv_mu"], lw["rwkv_w0"], lw["rwkv_w_lora"],
                               lw["rwkv_a0"], lw["rwkv_a_lora"], lw["rwkv_g_lora"], lw["rwkv_k_k"], lw["rwkv_k_a"],
                               lw["rwkv_r_k"], lw["rwkv_lnx_w"], lw["rwkv_lnx_b"], clen=clen, nseq=nseq,
                               casts=[lw[name] for name in in_wkv])
    lw = {**lw, **dict(zip(in_wkv, copies))}

    h = _outproj(h, att.reshape(m, att_w), rw.reshape(m, -1), lw["w_out"], tm=tm)
    h = _ffn(h, lw["norm_ffn2"], lw["ffn2_gate"], lw["ffn2_up"], lw["ffn2_down"], tm=tm_big, tf=tf)
    h = _ple(h, lw["norm_ple"], lw["ple_gate"], p.reshape(m, -1), lw["ple_proj"], tm=tm)

    return h.reshape(b, t, d), (k_keep, v_keep, wkv_new, proj3[:, -1:, 3 * att_w:]), lw


def kernel(x_prompt, x_sample, cache_att_k, cache_att_v, state_wkv, state_shift, p_prompt, p_sample, norm_ffn1, ffn1_gate, ffn1_up, ffn1_down, norm_mix, w_in, q_norm, k_norm, rel_bias, rwkv_mu, rwkv_w0, rwkv_w_lora, rwkv_a0, rwkv_a_lora, rwkv_g_lora, rwkv_k_k, rwkv_k_a, rwkv_r_k, rwkv_lnx_w, rwkv_lnx_b, w_out, norm_ffn2, ffn2_gate, ffn2_up, ffn2_down, norm_ple, ple_gate, ple_proj):
    depth = norm_ffn1.shape[0]
    hp, hs = x_prompt, x_sample
    bp = hp.shape[0]
    att_w = w_out.shape[1] // 2
    rwkv_heads = rwkv_r_k.shape[1]
    rwkv_pw = rwkv_mu.shape[1]
    prompt_states, sample_states = [], []
    for i in range(depth):
        lw = {
            "norm_ffn1": norm_ffn1[i], "ffn1_gate": ffn1_gate[i].astype(BF16), "ffn1_up": ffn1_up[i].astype(BF16),
            "ffn1_down": ffn1_down[i].astype(BF16), "norm_mix": norm_mix[i], "w_in": w_in[i].astype(BF16),
            "q_norm": q_norm[i], "k_norm": k_norm[i],
            "rel_bias": rel_bias[i], "rwkv_mu": rwkv_mu[i], "rwkv_w0": rwkv_w0[i],
            "rwkv_w_lora": rwkv_w_lora[i].astype(BF16), "rwkv_a0": rwkv_a0[i],
            "rwkv_a_lora": rwkv_a_lora[i].astype(BF16), "rwkv_g_lora": rwkv_g_lora[i].astype(BF16),
            "rwkv_k_k": rwkv_k_k[i], "rwkv_k_a": rwkv_k_a[i], "rwkv_r_k": rwkv_r_k[i].reshape(-1),
            "rwkv_lnx_w": rwkv_lnx_w[i], "rwkv_lnx_b": rwkv_lnx_b[i],
            "w_out": w_out[i], "norm_ffn2": norm_ffn2[i], "ffn2_gate": ffn2_gate[i], "ffn2_up": ffn2_up[i],
            "ffn2_down": ffn2_down[i], "norm_ple": norm_ple[i], "ple_gate": ple_gate[i],
            "ple_proj": ple_proj[i].astype(BF16),
        }
        zero_ctx = jnp.zeros((1, LEFT, att_w // ATT_HEAD_DIM, ATT_HEAD_DIM), F32)
        zero_shift = jnp.zeros((bp, 1, rwkv_pw), hp.dtype)
        zero_wkv = jnp.zeros((bp, rwkv_heads, RWKV_HEAD_DIM, RWKV_HEAD_DIM), hp.dtype)
        hp, sp, lw = _trunk_layer(hp, p_prompt[i], lw, zero_ctx, zero_ctx, False, zero_shift, zero_wkv)
        hs, ss, _ = _trunk_layer(hs, p_sample[i], lw, cache_att_k[i], cache_att_v[i], True, state_shift[i],
                                 state_wkv[i])
        prompt_states.append(sp)
        sample_states.append(ss)
    stack = lambda states, j: states[0][j][None] if depth == 1 else jnp.stack([s[j] for s in states])
    return (hp, hs, stack(prompt_states, 0), stack(prompt_states, 1), stack(prompt_states, 2),
            stack(prompt_states, 3), stack(sample_states, 0), stack(sample_states, 1),
            stack(sample_states, 2), stack(sample_states, 3))
```
